```python
import math
import jax
import jax.numpy as jnp
from jax import lax
import numpy as np

D_MODEL = 1024
BATCH = 8
SEQ = 4096
DEPTH = 2

GRID_W = 64
CTX_LEN = 256
N_BRANCH = 4
MIX_WIDTH = D_MODEL // N_BRANCH
POOL_WINDOWS = (2, 4, 8, 16)
POOL_GROUPS = len(POOL_WINDOWS)
FNET_GROUPS = 4
HYENA_ORDER = 2
HYENA_BANDS = 16
HYENA_POS_DIM = 1 + 2 * HYENA_BANDS
HYENA_FILTER_HIDDEN = 64
HYENA_DECAY_TARGET = 1e-2
HYENA_FAST_DECAY = 0.3
HYENA_SLOW_DECAY = 1.5
HYENA_DECAY_SHIFT = 0.05
SHORT_CONV = 3
NA_HEAD_DIM = 64
NA_HEADS = MIX_WIDTH // NA_HEAD_DIM
NA_WIN_H = 8
NA_WIN_W = 16
FFN_HIDDEN = 2816
EPS = 1e-6
NEG_INF = -1e30

POOL_OFF = 0
FNET_OFF = POOL_OFF + MIX_WIDTH
HYENA_OFF = FNET_OFF + MIX_WIDTH
NA_OFF = HYENA_OFF + 3 * MIX_WIDTH
GATE_OFF = NA_OFF + 3 * MIX_WIDTH
IN_WIDTH = GATE_OFF + N_BRANCH * D_MODEL

kernel_name = 'hybrid_pool_fnet_hyena_natten_dit_block'


def rmsnorm(x, g):
    xf = x.astype(jnp.float32)
    y = xf * lax.rsqrt(jnp.mean(xf * xf, axis=-1, keepdims=True) + EPS)
    return (y * g.astype(jnp.float32)).astype(x.dtype)


def ada_norm(x, g, shift, scale):
    return rmsnorm(x, g) * (1 + scale) + shift


def depthwise_conv(x, w, b):
    K = w.shape[0]
    pad = K // 2
    L = x.shape[1]
    xp = jnp.pad(x, ((0, 0), (pad, K - 1 - pad), (0, 0)))
    return sum(xp[:, j:j + L] * w[j] for j in range(K)) + b


def pool_mixer(u, pool_w, pool_scale):
    B, L, _ = u.shape
    ug = u.reshape(B, L, POOL_GROUPS, -1).astype(jnp.float32)
    csum = jnp.concatenate([jnp.zeros_like(ug[:, :1]), jnp.cumsum(ug, axis=1)], axis=1)
    t = np.arange(L)[:, None]
    win = np.array(POOL_WINDOWS)[None, :]
    lo = np.clip(t - win // 2, 0, L)
    hi = np.clip(t - win // 2 + win, 0, L)
    grp = np.arange(POOL_GROUPS)[None, :]
    cnt = (hi - lo).astype(np.float32)[None, :, :, None]
    y = (csum[:, hi, grp] - csum[:, lo, grp]) / cnt - ug
    y = jnp.einsum('blgc,gcd->blgd', y.astype(u.dtype), pool_w)
    return y.reshape(B, L, -1) * pool_scale


def fourier_mixer(u):
    B, L, _ = u.shape
    ug = u.reshape(B, L, FNET_GROUPS, -1).astype(jnp.float32)
    y = jnp.fft.fft2(ug, axes=(1, 3), norm='ortho').real
    return y.reshape(B, L, -1).astype(u.dtype)


def hyena_filters(L, p):
    f32 = jnp.float32
    t = jnp.linspace(0.0, 1.0, L, dtype=f32)[:, None]
    bands = jnp.linspace(1e-4, HYENA_BANDS - 1, HYENA_BANDS, dtype=f32)[None, :]
    ang = (2.0 * math.pi / L) * jnp.arange(L, dtype=f32)[:, None] * bands
    feats = jnp.concatenate([t, jnp.cos(ang), -jnp.sin(ang)], axis=-1)
    freq = p['hyena_freq'].astype(f32)
    h = jnp.sin(freq * (feats @ p['hyena_filt_w1'].astype(f32) + p['hyena_filt_b1'].astype(f32)))
    h = jnp.sin(freq * (h @ p['hyena_filt_w2'].astype(f32) + p['hyena_filt_b2'].astype(f32)))
    h = (h @ p['hyena_filt_w3'].astype(f32) + p['hyena_filt_b3'].astype(f32))
    h = h.reshape(L, HYENA_ORDER, 2, MIX_WIDTH)
    deltas = jnp.linspace(math.log(HYENA_DECAY_TARGET) / HYENA_SLOW_DECAY,
                          math.log(HYENA_DECAY_TARGET) / HYENA_FAST_DECAY, MIX_WIDTH, dtype=f32)
    decay = jnp.exp(-t[:, :, None, None] * jnp.abs(deltas))
    h = h * (decay + HYENA_DECAY_SHIFT)
    return h / (jnp.sum(jnp.abs(h), axis=(0, 2), keepdims=True) + EPS)


def long_conv(z, hf, hb, skip):
    L, C = hf.shape
    k = jnp.concatenate([hf, jnp.zeros((1, C), hf.dtype), hb[1:][::-1]], axis=0)
    zf = jnp.fft.rfft(z, n=2 * L, axis=1)
    kf = jnp.fft.rfft(k, axis=0)
    y = jnp.fft.irfft(zf * kf[None], n=2 * L, axis=1)[:, :L]
    return y + skip * z


def hyena_mixer(u, p):
    u_c = depthwise_conv(u, p['hyena_conv_w'], p['hyena_conv_b']).astype(jnp.float32)
    v, x1, x2 = jnp.split(u_c, 3, axis=-1)
    h = hyena_filters(u.shape[1], p)
    skip = p['hyena_skip'].astype(jnp.float32)
    z = v
    for n, gate in enumerate((x1, x2)):
        z = gate * long_conv(z, h[:, n, 0], h[:, n, 1], skip[n])
    return z.astype(u.dtype)


def neighbourhood_attention(q, k, v, k_ctx, v_ctx, rpb):
    B, L, _ = q.shape
    rows = L // GRID_W
    kh = min(NA_WIN_H, rows)
    grid = (B, rows, GRID_W, NA_HEADS, NA_HEAD_DIM)
    q, k, v = q.reshape(grid), k.reshape(grid), v.reshape(grid)
    kc = k_ctx.reshape(B, -1, NA_HEADS, NA_HEAD_DIM)
    vc = v_ctx.reshape(B, -1, NA_HEADS, NA_HEAD_DIM)
    r = np.arange(rows)
    r0 = np.clip(r - kh // 2, 0, rows - kh)
    row_idx = r0[:, None] + np.arange(kh)[None, :]
    col = np.arange(GRID_W)
    c0 = np.clip(col - NA_WIN_W // 2, 0, GRID_W - NA_WIN_W)
    col_ok = (col[None, :] >= c0[:, None]) & (col[None, :] < c0[:, None] + NA_WIN_W)
    dr = row_idx - r[:, None] + (NA_WIN_H - 1)
    dc = np.clip(col[None, :] - col[:, None], 1 - NA_WIN_W, NA_WIN_W - 1) + (NA_WIN_W - 1)
    bias = rpb.astype(jnp.float32)[:, dr[:, None, :, None], dc[None, :, None, :]]
    bias = jnp.where(col_ok[:, None, :], bias, NEG_INF)
    kb = k[:, row_idx]
    vb = v[:, row_idx]
    scale = NA_HEAD_DIM ** -0.5
    s_nb = jnp.einsum('brqhd,brikhd->bhrqik', q, kb).astype(jnp.float32) * scale + bias
    s_cx = jnp.einsum('brqhd,bchd->bhrqc', q, kc).astype(jnp.float32) * scale
    n_nb = kh * GRID_W
    s = jnp.concatenate([s_nb.reshape(B, NA_HEADS, rows, GRID_W, n_nb), s_cx], axis=-1)
    prob = jax.nn.softmax(s, axis=-1).astype(v.dtype)
    p_nb = prob[..., :n_nb].reshape(s_nb.shape)
    p_cx = prob[..., n_nb:]
    o = (jnp.einsum('bhrqik,brikhd->brqhd', p_nb, vb)
         + jnp.einsum('bhrqc,bchd->brqhd', p_cx, vc))
    return o.reshape(B, L, MIX_WIDTH)


def context_attention(q, k, v):
    B, C, _ = q.shape
    shp = (B, C, NA_HEADS, NA_HEAD_DIM)
    q, k, v = q.reshape(shp), k.reshape(shp), v.reshape(shp)
    s = jnp.einsum('bqhd,bkhd->bhqk', q, k).astype(jnp.float32) * (NA_HEAD_DIM ** -0.5)
    prob = jax.nn.softmax(s, axis=-1).astype(v.dtype)
    return jnp.einsum('bhqk,bkhd->bqhd', prob, v).reshape(B, C, MIX_WIDTH)


def token_mixer(proj, k_ctx, v_ctx, p, on_grid):
    def cols(off, width):
        return proj[..., off:off + width]
    y_pool = pool_mixer(cols(POOL_OFF, MIX_WIDTH), p['pool_w'], p['pool_scale'])
    y_fnet = fourier_mixer(cols(FNET_OFF, MIX_WIDTH))
    y_hyena = hyena_mixer(cols(HYENA_OFF, 3 * MIX_WIDTH), p)
    q = cols(NA_OFF, MIX_WIDTH)
    if on_grid:
        y_na = neighbourhood_attention(q, cols(NA_OFF + MIX_WIDTH, MIX_WIDTH),
                                       cols(NA_OFF + 2 * MIX_WIDTH, MIX_WIDTH),
                                       k_ctx, v_ctx, p['na_rpb'])
    else:
        y_na = context_attention(q, k_ctx, v_ctx)
    gates = jax.nn.sigmoid(cols(GATE_OFF, N_BRANCH * D_MODEL))
    merged = 0
    for b, y in enumerate((y_pool, y_fnet, y_hyena, y_na)):
        merged = merged + gates[..., b * D_MODEL:(b + 1) * D_MODEL] * (y @ p['w_branch'][b])
    return merged @ p['w_out']


def conv_ffn(h, p):
    u, g = jnp.split(h @ p['ffn_w_up'], 2, axis=-1)
    g = depthwise_conv(g, p['ffn_conv_w'], p['ffn_conv_b'])
    return (jax.nn.silu(g) * u) @ p['ffn_w_down']


def setup_inputs(seed: int = 0) -> dict:
    key = jax.random.key(seed)
    ks = iter(jax.random.split(key, 32))

    def nrm(shape, s):
        return jax.random.normal(next(ks), shape, jnp.float32) * s

    D, F, M, Hd = D_MODEL, FFN_HIDDEN, MIX_WIDTH, HYENA_FILTER_HIDDEN
    return {
        'x': nrm((BATCH, SEQ, D), 1.0),
        'c': nrm((BATCH, D), 1.0),
        'ctx': nrm((BATCH, CTX_LEN, D), 1.0),
        'c_ctx': nrm((D,), 1.0),
        'w_mod': nrm((DEPTH, D, 6 * D), 0.5 * D ** -0.5),
        'b_mod': nrm((DEPTH, 6 * D), 0.01),
        'norm1_g': 1.0 + nrm((DEPTH, D), 0.02),
        'norm2_g': 1.0 + nrm((DEPTH, D), 0.02),
        'w_in': nrm((DEPTH, D, IN_WIDTH), D ** -0.5),
        'pool_w': nrm((DEPTH, POOL_GROUPS, M // POOL_GROUPS, M // POOL_GROUPS), (M // POOL_GROUPS) ** -0.5),
        'pool_scale': 1.0 + nrm((DEPTH, M), 0.02),
        'hyena_conv_w': nrm((DEPTH, SHORT_CONV, 3 * M), SHORT_CONV ** -0.5),
        'hyena_conv_b': nrm((DEPTH, 3 * M), 0.01),
        'hyena_filt_w1': nrm((DEPTH, HYENA_POS_DIM, Hd), HYENA_POS_DIM ** -0.5),
        'hyena_filt_b1': nrm((DEPTH, Hd), 0.01),
        'hyena_filt_w2': nrm((DEPTH, Hd, Hd), Hd ** -0.5),
        'hyena_filt_b2': nrm((DEPTH, Hd), 0.01),
        'hyena_filt_w3': nrm((DEPTH, Hd, HYENA_ORDER * 2 * M), Hd ** -0.5),
        'hyena_filt_b3': nrm((DEPTH, HYENA_ORDER * 2 * M), 0.01),
        'hyena_freq': 1.0 + nrm((DEPTH, Hd), 0.1),
        'hyena_skip': nrm((DEPTH, HYENA_ORDER, M), 0.5),
        'na_rpb': nrm((DEPTH, NA_HEADS, 2 * NA_WIN_H - 1, 2 * NA_WIN_W - 1), 0.1),
        'w_branch': nrm((DEPTH, N_BRANCH, M, D), M ** -0.5),
        'w_out': nrm((DEPTH, D, D), D ** -0.5),
        'ffn_w_up': nrm((DEPTH, D, 2 * F), D ** -0.5),
        'ffn_conv_w': nrm((DEPTH, SHORT_CONV, F), SHORT_CONV ** -0.5),
        'ffn_conv_b': nrm((DEPTH, F), 0.01),
        'ffn_w_down': nrm((DEPTH, F, D), F ** -0.5),
        'final_norm_g': 1.0 + nrm((D,), 0.02),
    }


def reference(x, c, ctx, c_ctx, w_mod, b_mod, norm1_g, norm2_g, w_in, pool_w, pool_scale,
              hyena_conv_w, hyena_conv_b, hyena_filt_w1, hyena_filt_b1, hyena_filt_w2,
              hyena_filt_b2, hyena_filt_w3, hyena_filt_b3, hyena_freq, hyena_skip, na_rpb,
              w_branch, w_out, ffn_w_up, ffn_conv_w, ffn_conv_b, ffn_w_down, final_norm_g):
    xc = ctx
    c_act = jax.nn.silu(c)
    cc_act = jax.nn.silu(c_ctx)[None]
    for l in range(DEPTH):
        last = l == DEPTH - 1
        p = {
            'pool_w': pool_w[l], 'pool_scale': pool_scale[l],
            'hyena_conv_w': hyena_conv_w[l], 'hyena_conv_b': hyena_conv_b[l],
            'hyena_filt_w1': hyena_filt_w1[l], 'hyena_filt_b1': hyena_filt_b1[l],
            'hyena_filt_w2': hyena_filt_w2[l], 'hyena_filt_b2': hyena_filt_b2[l],
            'hyena_filt_w3': hyena_filt_w3[l], 'hyena_filt_b3': hyena_filt_b3[l],
            'hyena_freq': hyena_freq[l], 'hyena_skip': hyena_skip[l], 'na_rpb': na_rpb[l],
            'w_branch': w_branch[l], 'w_out': w_out[l],
            'ffn_w_up': ffn_w_up[l], 'ffn_conv_w': ffn_conv_w[l],
            'ffn_conv_b': ffn_conv_b[l], 'ffn_w_down': ffn_w_down[l],
        }
        mod = (c_act @ w_mod[l] + b_mod[l])[:, None]
        mod_c = (cc_act @ w_mod[l] + b_mod[l])[:, None]
        sh1, sc1, g1, sh2, sc2, g2 = jnp.split(mod, 6, axis=-1)
        sh1c, sc1c, g1c, sh2c, sc2c, g2c = jnp.split(mod_c, 6, axis=-1)
        hc = ada_norm(xc, norm1_g[l], sh1c, sc1c)
        if last:
            kv_c = hc @ w_in[l][:, NA_OFF + MIX_WIDTH:NA_OFF + 3 * MIX_WIDTH]
        else:
            proj_c = hc @ w_in[l]
            kv_c = proj_c[..., NA_OFF + MIX_WIDTH:NA_OFF + 3 * MIX_WIDTH]
        k_c, v_c = jnp.split(kv_c, 2, axis=-1)
        hx = ada_norm(x, norm1_g[l], sh1, sc1)
        proj_x = hx @ w_in[l]
        x = x + g1 * token_mixer(proj_x, k_c, v_c, p, True)
        x = x + g2 * conv_ffn(ada_norm(x, norm2_g[l], sh2, sc2), p)
        if not last:
            xc = xc + g1c * token_mixer(proj_c, k_c, v_c, p, False)
            xc = xc + g2c * conv_ffn(ada_norm(xc, norm2_g[l], sh2c, sc2c), p)
    return rmsnorm(x, final_norm_g)
```

```python
import functools
import math

import numpy as np
import jax
import jax.numpy as jnp
from jax import lax
from jax.experimental import pallas as pl
from jax.experimental.pallas import tpu as pltpu

F32 = jnp.float32
BF16 = jnp.bfloat16

GRID_W = 64
N_BRANCH = 4
POOL_WINDOWS = (2, 4, 8, 16)
FNET_GROUPS = 4
HYENA_ORDER = 2
HYENA_BANDS = 16
HYENA_DECAY_TARGET = 1e-2
HYENA_FAST_DECAY = 0.3
HYENA_SLOW_DECAY = 1.5
HYENA_DECAY_SHIFT = 0.05
NA_HEAD_DIM = 64
NA_WIN_H = 8
NA_WIN_W = 16
EPS = 1e-6
NEG_INF = -1e30

HALO = 16
PAD = 8
VMEM_LIMIT = 56 * 1024 * 1024


def _params(sem):
    return pltpu.CompilerParams(dimension_semantics=sem, vmem_limit_bytes=VMEM_LIMIT)


def _dot(a, b):
    return jnp.dot(a, b, preferred_element_type=F32)


def _dot_t(a, b):
    return lax.dot_general(a, b, (((1,), (1,)), ((), ())), preferred_element_type=F32)


def _split_bf16(a):
    hi = a.astype(BF16)
    lo = (a - hi.astype(F32)).astype(BF16)
    return hi, lo


def _mod_kernel(c_ref, w_ref, b_ref, o_ref):
    a = c_ref[...]
    a = a * jax.nn.sigmoid(a)
    a_hi, a_lo = _split_bf16(a)
    w_hi, w_lo = _split_bf16(w_ref[0])
    o_ref[0] = _dot(a_hi, w_hi) + _dot(a_lo, w_hi) + _dot(a_hi, w_lo) + b_ref[0]


def _mod_call(cvec, w_mod, b_mod):
    depth, d, n = w_mod.shape
    rows = cvec.shape[0]
    tn = 1024
    return pl.pallas_call(
        _mod_kernel,
        grid=(depth, n // tn),
        in_specs=[
            pl.BlockSpec((rows, d), lambda l, j: (0, 0)),
            pl.BlockSpec((1, d, tn), lambda l, j: (l, 0, j)),
            pl.BlockSpec((1, 1, tn), lambda l, j: (l, 0, j)),
        ],
        out_specs=pl.BlockSpec((1, rows, tn), lambda l, j: (l, 0, j)),
        out_shape=jax.ShapeDtypeStruct((depth, rows, n), F32),
        compiler_params=_params(("arbitrary", "arbitrary")),
        name="mod",
    )(cvec, w_mod, b_mod.reshape(depth, 1, n))


def _ada_norm(x, g, shift, scale):
    y = x * lax.rsqrt(jnp.mean(x * x, axis=-1, keepdims=True) + EPS)
    return (y * g) * (1.0 + scale) + shift


def _proj_kernel(x_ref, sh_ref, sc_ref, g_ref, wa_ref, wb_ref, cs_ref, oa_ref, ob_ref, of_ref):
    m = cs_ref.shape[0]
    h = _ada_norm(x_ref[0], g_ref[...], sh_ref[0, 0], sc_ref[0, 0]).astype(BF16)
    oa_ref[0] = _dot(h, wa_ref[...])
    pb = _dot(h, wb_ref[...])
    ob_ref[0] = pb[:, m:].astype(BF16)
    of_ref[0] = _dot(pb[:, :m].astype(BF16), cs_ref[...]).astype(BF16)


def _proj_call(x, mod4, g, wa, wb, cs, tm):
    b, l, d = x.shape
    m = cs.shape[0]
    na, nb = wa.shape[1], wb.shape[1]
    const = lambda bi, i: (0, 0)
    return pl.pallas_call(
        _proj_kernel,
        grid=(b, l // tm),
        in_specs=[
            pl.BlockSpec((1, tm, d), lambda bi, i: (bi, i, 0)),
            pl.BlockSpec((1, 1, 1, d), lambda bi, i: (bi, 0, 0, 0)),
            pl.BlockSpec((1, 1, 1, d), lambda bi, i: (bi, 1, 0, 0)),
            pl.BlockSpec((1, d), const),
            pl.BlockSpec((d, na), const),
            pl.BlockSpec((d, nb), const),
            pl.BlockSpec((m, 2 * m), const),
        ],
        out_specs=[
            pl.BlockSpec((1, tm, na), lambda bi, i: (bi, i, 0)),
            pl.BlockSpec((1, tm, nb - m), lambda bi, i: (bi, i, 0)),
            pl.BlockSpec((1, tm, 2 * m), lambda bi, i: (bi, i, 0)),
        ],
        out_shape=[
            jax.ShapeDtypeStruct((b, l, na), F32),
            jax.ShapeDtypeStruct((b, l, nb - m), BF16),
            jax.ShapeDtypeStruct((b, l, 2 * m), BF16),
        ],
        compiler_params=_params(("parallel", "parallel")),
        name="proj",
    )(x, mod4, mod4, g, wa, wb, cs)


def _fill_padded(src_ref, pad_ref, l, rows):
    c = pad_ref.shape[1]
    pad_ref[0:PAD, :] = jnp.zeros((PAD, c), F32)
    pad_ref[l + PAD:l + 2 * PAD, :] = jnp.zeros((PAD, c), F32)

    def copy(i, carry):
        r = pl.multiple_of(i * rows, rows)
        pad_ref[pl.ds(r + PAD, rows), :] = src_ref[0, pl.ds(r, rows), :]
        return carry

    lax.fori_loop(0, l // rows, copy, 0)


def _pool_kernel(u_ref, w_ref, s_ref, o_ref, pad_ref, *, rows):
    l, c = u_ref.shape[1], u_ref.shape[2]
    gw = c // len(POOL_WINDOWS)
    _fill_padded(u_ref, pad_ref, l, rows)
    n = rows + 2 * PAD
    lane = lax.broadcasted_iota(jnp.int32, (n, c), 1)
    row = lax.broadcasted_iota(jnp.int32, (n, c), 0)
    grp = lax.shift_right_logical(lane, int(math.log2(gw)))
    half = jnp.where(grp == 0, 1, jnp.where(grp == 1, 2, jnp.where(grp == 2, 4, 8)))
    w = w_ref[...]
    scale = s_ref[...]

    def body(i, carry):
        r = pl.multiple_of(i * rows, rows)
        win = pad_ref[pl.ds(r, n), :]
        a2 = win + pltpu.roll(win, 1, 0)
        w4 = pltpu.roll(a2, 1, 0) + pltpu.roll(a2, n - 1, 0)
        w8 = pltpu.roll(w4, 2, 0) + pltpu.roll(w4, n - 2, 0)
        w16 = pltpu.roll(w8, 4, 0) + pltpu.roll(w8, n - 4, 0)
        s = jnp.where(grp == 0, a2, jnp.where(grp == 1, w4, jnp.where(grp == 2, w8, w16)))
        t = row + (r - PAD)
        cnt = (jnp.minimum(t + half, l) - jnp.maximum(t - half, 0)).astype(F32)
        y = (s / jnp.maximum(cnt, 1.0) - win)[PAD:PAD + rows]
        o_ref[0, pl.ds(r, rows), :] = (_dot(y.astype(BF16), w) * scale).astype(BF16)
        return carry

    lax.fori_loop(0, l // rows, body, 0)


def _pool_call(oa, w_blk, scale, rows):
    b, l, _ = oa.shape
    c = w_blk.shape[0]
    return pl.pallas_call(
        functools.partial(_pool_kernel, rows=rows),
        grid=(b,),
        in_specs=[
            pl.BlockSpec((1, l, c), lambda bi: (bi, 0, 0)),
            pl.BlockSpec((c, c), lambda bi: (0, 0)),
            pl.BlockSpec((1, c), lambda bi: (0, 0)),
        ],
        out_specs=pl.BlockSpec((1, l, c), lambda bi: (bi, 0, 0)),
        out_shape=jax.ShapeDtypeStruct((b, l, c), BF16),
        scratch_shapes=[pltpu.VMEM((l + 2 * PAD, c), F32)],
        compiler_params=_params(("parallel",)),
        name="pool",
    )(oa, w_blk, scale)


def _dwconv_kernel(u_ref, w_ref, b_ref, o_ref, ob_ref, pad_ref, *, rows):
    l = u_ref.shape[1]
    _fill_padded(u_ref, pad_ref, l, rows)
    n = rows + 2 * PAD
    w = w_ref[...]
    bias = b_ref[...]

    def body(i, carry):
        r = pl.multiple_of(i * rows, rows)
        win = pad_ref[pl.ds(r, n), :]
        y = pltpu.roll(win, 1, 0) * w[0:1] + win * w[1:2] + pltpu.roll(win, n - 1, 0) * w[2:3] + bias
        y = y[PAD:PAD + rows]
        o_ref[0, pl.ds(r, rows), :] = y
        ob_ref[0, pl.ds(r, rows), :] = y.astype(BF16)
        return carry

    lax.fori_loop(0, l // rows, body, 0)


def _dwconv_call(oa, w, bias, c, col0, rows):
    b, l, _ = oa.shape
    nblk = w.shape[1] // c
    return pl.pallas_call(
        functools.partial(_dwconv_kernel, rows=rows),
        grid=(b, nblk),
        in_specs=[
            pl.BlockSpec((1, l, c), lambda bi, j: (bi, 0, j + col0)),
            pl.BlockSpec((w.shape[0], c), lambda bi, j: (0, j)),
            pl.BlockSpec((1, c), lambda bi, j: (0, j)),
        ],
        out_specs=[
            pl.BlockSpec((1, l, c), lambda bi, j: (bi, 0, j)),
            pl.BlockSpec((1, l, c), lambda bi, j: (bi, 0, j)),
        ],
        out_shape=[
            jax.ShapeDtypeStruct((b, l, nblk * c), F32),
            jax.ShapeDtypeStruct((b, l, nblk * c), BF16),
        ],
        scratch_shapes=[pltpu.VMEM((l + 2 * PAD, c), F32)],
        compiler_params=_params(("parallel", "parallel")),
        name="dwconv",
    )(oa, w, bias)


def _filt_kernel(f_ref, w1_ref, b1_ref, w2_ref, b2_ref, w3_ref, b3_ref, fr_ref, dl_ref,
                 hi_ref, lo_ref, row0_ref, asum_ref):
    hp = lax.Precision.HIGHEST
    feats = f_ref[...]
    freq = fr_ref[...]
    h = jnp.sin(freq * (jnp.dot(feats, w1_ref[...], precision=hp, preferred_element_type=F32) + b1_ref[...]))
    h = jnp.sin(freq * (jnp.dot(h, w2_ref[...], precision=hp, preferred_element_type=F32) + b2_ref[...]))
    h = jnp.dot(h, w3_ref[...], precision=hp, preferred_element_type=F32) + b3_ref[...]
    t = feats[:, 0:1]
    win = jnp.exp(-t * jnp.abs(dl_ref[...])) + HYENA_DECAY_SHIFT
    h = h * jnp.concatenate([win] * (h.shape[1] // win.shape[1]), axis=1)
    hi, lo = _split_bf16(h)
    hi_ref[...] = hi
    lo_ref[...] = lo
    part = jnp.sum(jnp.abs(h), axis=0, keepdims=True)

    @pl.when(pl.program_id(0) == 0)
    def _():
        asum_ref[...] = jnp.zeros_like(asum_ref)
        row0_ref[...] = h[0:8]

    asum_ref[...] += part


def _filt_call(feats, w1, b1, w2, b2, w3, b3, freq, deltas, rows):
    l, fd = feats.shape
    hd = w2.shape[0]
    n = w3.shape[1]
    c = deltas.shape[1]
    const = lambda i: (0, 0)
    return pl.pallas_call(
        _filt_kernel,
        grid=(l // rows,),
        in_specs=[
            pl.BlockSpec((rows, fd), lambda i: (i, 0)),
            pl.BlockSpec((fd, hd), const), pl.BlockSpec((1, hd), const),
            pl.BlockSpec((hd, hd), const), pl.BlockSpec((1, hd), const),
            pl.BlockSpec((hd, n), const), pl.BlockSpec((1, n), const),
            pl.BlockSpec((1, hd), const), pl.BlockSpec((1, c), const),
        ],
        out_specs=[
            pl.BlockSpec((rows, n), lambda i: (i, 0)),
            pl.BlockSpec((rows, n), lambda i: (i, 0)),
            pl.BlockSpec((8, n), const),
            pl.BlockSpec((1, n), const),
        ],
        out_shape=[
            jax.ShapeDtypeStruct((l, n), BF16),
            jax.ShapeDtypeStruct((l, n), BF16),
            jax.ShapeDtypeStruct((8, n), F32),
            jax.ShapeDtypeStruct((1, n), F32),
        ],
        compiler_params=_params(("arbitrary",)),
        name="hyena_filt",
    )(feats, w1, b1, w2, b2, w3, b3, freq, deltas)


def _alt_signs(l):
    t = lax.broadcasted_iota(jnp.int32, (16, l), 1)
    return jnp.where((t & 1) == 0, 1.0, -1.0).astype(BF16)


def _kspec_kernel(fc_ref, fs_ref, hi_ref, lo_ref, row0_ref, asum_ref, kr_ref, ks_ref, kn_ref):
    i = pl.program_id(1)
    tm, l = fc_ref.shape
    c = kr_ref.shape[2]
    fc, fs = fc_ref[...], fs_ref[...]
    hi, lo = hi_ref[...], lo_ref[...]
    hc = _dot(fc, hi) + _dot(fc, lo)
    hs = _dot(fs, hi) + _dot(fs, lo)
    asum = asum_ref[...]
    inv = 1.0 / (asum[:, :c] + asum[:, c:] + EPS)
    hb0 = row0_ref[0:1, c:]
    row = lax.broadcasted_iota(jnp.int32, (tm, c), 0) + i * tm
    wj = jnp.where(row == 0, 0.5 / l, 1.0 / l)
    kr_ref[0] = (hc[:, :c] + hc[:, c:] - hb0) * inv * wj
    ks_ref[0] = (hs[:, :c] - hs[:, c:]) * inv * wj

    @pl.when(i == 0)
    def _():
        sg = _alt_signs(l)
        alt = (_dot(sg, hi) + _dot(sg, lo))[0:8]
        kn_ref[0] = (alt[:, :c] + alt[:, c:] - hb0) * inv * (0.5 / l)


def _kspec_call(fc, fs, h_hi, h_lo, row0, asum, c, tm):
    l = fc.shape[0]
    orders = h_hi.shape[1] // (2 * c)
    return pl.pallas_call(
        _kspec_kernel,
        grid=(orders, l // tm),
        in_specs=[
            pl.BlockSpec((tm, l), lambda o, i: (i, 0)),
            pl.BlockSpec((tm, l), lambda o, i: (i, 0)),
            pl.BlockSpec((l, 2 * c), lambda o, i: (0, o)),
            pl.BlockSpec((l, 2 * c), lambda o, i: (0, o)),
            pl.BlockSpec((8, 2 * c), lambda o, i: (0, o)),
            pl.BlockSpec((1, 2 * c), lambda o, i: (0, o)),
        ],
        out_specs=[
            pl.BlockSpec((1, tm, c), lambda o, i: (o, i, 0)),
            pl.BlockSpec((1, tm, c), lambda o, i: (o, i, 0)),
            pl.BlockSpec((1, 8, c), lambda o, i: (o, 0, 0)),
        ],
        out_shape=[
            jax.ShapeDtypeStruct((orders, l, c), F32),
            jax.ShapeDtypeStruct((orders, l, c), F32),
            jax.ShapeDtypeStruct((orders, 8, c), F32),
        ],
        compiler_params=_params(("arbitrary", "arbitrary")),
        name="hyena_kspec",
    )(fc, fs, h_hi, h_lo, row0, asum)


def _hfwd_kernel(fc_ref, fs_ref, z_ref, kr_ref, ks_ref, kn_ref, spec_ref, yn_ref):
    nb = z_ref.shape[0]
    c = z_ref.shape[2]
    fc, fs = fc_ref[...], fs_ref[...]
    kr, ks = kr_ref[0], ks_ref[0]
    for bb in range(nb):
        z = z_ref[bb]
        zr = _dot(fc, z)
        zs = _dot(fs, z)
        spec_ref[bb, :, 0:c] = (zr * kr - zs * ks).astype(BF16)
        spec_ref[bb, :, c:2 * c] = (zr * ks + zs * kr).astype(BF16)

    @pl.when(pl.program_id(1) == 0)
    def _():
        sg = _alt_signs(z_ref.shape[1])
        for bb in range(nb):
            yn_ref[bb] = _dot(sg, z_ref[bb])[0:8] * kn_ref[0]


def _hfwd_call(fc, fs, zb, zcol, kr, ks, kn, order, c, nb, tm):
    b, l, _ = zb.shape
    return pl.pallas_call(
        _hfwd_kernel,
        grid=(b // nb, l // tm),
        in_specs=[
            pl.BlockSpec((tm, l), lambda g, i: (i, 0)),
            pl.BlockSpec((tm, l), lambda g, i: (i, 0)),
            pl.BlockSpec((nb, l, c), lambda g, i: (g, 0, zcol)),
            pl.BlockSpec((1, tm, c), lambda g, i: (order, i, 0)),
            pl.BlockSpec((1, tm, c), lambda g, i: (order, i, 0)),
            pl.BlockSpec((1, 8, c), lambda g, i: (order, 0, 0)),
        ],
        out_specs=[
            pl.BlockSpec((nb, tm, 2 * c), lambda g, i: (g, i, 0)),
            pl.BlockSpec((nb, 8, c), lambda g, i: (g, 0, 0)),
        ],
        out_shape=[
            jax.ShapeDtypeStruct((b, l, 2 * c), BF16),
            jax.ShapeDtypeStruct((b, 8, c), F32),
        ],
        compiler_params=_params(("arbitrary", "arbitrary")),
        name="hyena_fwd",
    )(fc, fs, zb, kr, ks, kn)


def _hinv_kernel(fc_ref, fs_ref, spec_ref, yn_ref, z_ref, gate_ref, skip_ref, *out_refs):
    nb, tm, c = z_ref.shape
    fc, fs = fc_ref[...], fs_ref[...]
    row = lax.broadcasted_iota(jnp.int32, (tm, c), 0)
    skip = skip_ref[0]
    for bb in range(nb):
        y = _dot(fc, spec_ref[bb, :, 0:c]) + _dot(fs, spec_ref[bb, :, c:2 * c])
        yn = yn_ref[bb, 0:1, :]
        y = y + jnp.where((row & 1) == 0, yn, -yn)
        out = gate_ref[bb] * (y + skip * z_ref[bb])
        for o_ref in out_refs:
            o_ref[bb] = out.astype(o_ref.dtype)


def _hinv_call(fc, fs, spec, yn, zf, zcol, uc, gcol, skip, order, out_dtypes, c, nb, tm):
    b, l, _ = spec.shape
    return pl.pallas_call(
        _hinv_kernel,
        grid=(b // nb, l // tm),
        in_specs=[
            pl.BlockSpec((tm, l), lambda g, i: (i, 0)),
            pl.BlockSpec((tm, l), lambda g, i: (i, 0)),
            pl.BlockSpec((nb, l, 2 * c), lambda g, i: (g, 0, 0)),
            pl.BlockSpec((nb, 8, c), lambda g, i: (g, 0, 0)),
            pl.BlockSpec((nb, tm, c), lambda g, i: (g, i, zcol)),
            pl.BlockSpec((nb, tm, c), lambda g, i: (g, i, gcol)),
            pl.BlockSpec((1, 1, c), lambda g, i: (order, 0, 0)),
        ],
        out_specs=[pl.BlockSpec((nb, tm, c), lambda g, i: (g, i, 0)) for _ in out_dtypes],
        out_shape=[jax.ShapeDtypeStruct((b, l, c), dt) for dt in out_dtypes],
        compiler_params=_params(("arbitrary", "arbitrary")),
        name="hyena_inv",
    )(fc, fs, spec, yn, zf, uc, skip)


def _fseq_kernel(mc_ref, ms_ref, u_ref, o_ref):
    nb = u_ref.shape[0]
    c = o_ref.shape[2]
    mc, ms = mc_ref[...], ms_ref[...]
    for bb in range(nb):
        o_ref[bb] = (_dot(mc, u_ref[bb, :, 0:c]) + _dot(ms, u_ref[bb, :, c:2 * c])).astype(BF16)


def _fseq_call(mc, ms, ucs, nb, tm):
    b, l, c2 = ucs.shape
    c = c2 // 2
    return pl.pallas_call(
        _fseq_kernel,
        grid=(b // nb, l // tm),
        in_specs=[
            pl.BlockSpec((tm, l), lambda g, i: (i, 0)),
            pl.BlockSpec((tm, l), lambda g, i: (i, 0)),
            pl.BlockSpec((nb, l, c2), lambda g, i: (g, 0, 0)),
        ],
        out_specs=pl.BlockSpec((nb, tm, c), lambda g, i: (g, i, 0)),
        out_shape=jax.ShapeDtypeStruct((b, l, c), BF16),
        compiler_params=_params(("arbitrary", "arbitrary")),
        name="fnet_seq",
    )(mc, ms, ucs)


def _stack_heads(q, heads):
    lane = lax.broadcasted_iota(jnp.int32, q.shape, 1)
    zero = jnp.zeros_like(q)
    return jnp.concatenate([jnp.where(lax.shift_right_logical(lane, int(math.log2(NA_HEAD_DIM))) == h, q, zero) for h in range(heads)], axis=0)


def _merge_heads(o, heads, n):
    lane = lax.broadcasted_iota(jnp.int32, (n, o.shape[1]), 1)
    out = jnp.zeros((n, o.shape[1]), F32)
    for h in range(heads):
        out = out + jnp.where(lax.shift_right_logical(lane, int(math.log2(NA_HEAD_DIM))) == h, o[h * n:(h + 1) * n], 0.0)
    return out


def _natt_kernel(q_ref, k_ref, v_ref, kc_ref, vc_ref, bias_ref, o_ref, *, rows, kh):
    r = pl.program_id(1)
    w = q_ref.shape[1]
    heads = q_ref.shape[2] // NA_HEAD_DIM
    r0 = jnp.clip(r - kh // 2, 0, rows - kh)
    start = pl.multiple_of(r0 * w, w)
    q = q_ref[0] * jnp.asarray(NA_HEAD_DIM ** -0.5, BF16)
    q4 = _stack_heads(q, heads)
    ks = k_ref[0, pl.ds(start, kh * w), :]
    vs = v_ref[0, pl.ds(start, kh * w), :]
    s_nb = _dot_t(q4, ks) + bias_ref[0]
    s_cx = _dot_t(q4, kc_ref[0])
    m = jnp.maximum(jnp.max(s_nb, axis=-1, keepdims=True), jnp.max(s_cx, axis=-1, keepdims=True))
    p_nb = jnp.exp(s_nb - m)
    p_cx = jnp.exp(s_cx - m)
    den = jnp.sum(p_nb, axis=-1, keepdims=True) + jnp.sum(p_cx, axis=-1, keepdims=True)
    o4 = (_dot(p_nb.astype(BF16), vs) + _dot(p_cx.astype(BF16), vc_ref[0])) / den
    o_ref[0] = _merge_heads(o4, heads, w).astype(BF16)


def _natt_call(ob, obc, bias, c):
    b, l, _ = ob.shape
    lc = obc.shape[1]
    w = GRID_W
    rows = l // w
    kh = min(NA_WIN_H, rows)
    n_edge_lo = kh // 2

    def cls(r):
        return jnp.minimum(r, n_edge_lo) + jnp.maximum(r - (rows - kh + n_edge_lo), 0)

    return pl.pallas_call(
        functools.partial(_natt_kernel, rows=rows, kh=kh),
        grid=(b, rows),
        in_specs=[
            pl.BlockSpec((1, w, c), lambda bi, r: (bi, r, 0)),
            pl.BlockSpec((1, l, c), lambda bi, r: (bi, 0, 1)),
            pl.BlockSpec((1, l, c), lambda bi, r: (bi, 0, 2)),
            pl.BlockSpec((1, lc, c), lambda bi, r: (bi, 0, 1)),
            pl.BlockSpec((1, lc, c), lambda bi, r: (bi, 0, 2)),
            pl.BlockSpec((1,) + bias.shape[1:], lambda bi, r: (cls(r), 0, 0)),
        ],
        out_specs=pl.BlockSpec((1, w, c), lambda bi, r: (bi, r, 0)),
        out_shape=jax.ShapeDtypeStruct((b, l, c), BF16),
        compiler_params=_params(("parallel", "arbitrary")),
        name="nb_attn",
    )(ob, ob, ob, obc, obc, bias)


def _catt_kernel(q_ref, k_ref, v_ref, o_ref):
    n = q_ref.shape[1]
    heads = q_ref.shape[2] // NA_HEAD_DIM
    q = q_ref[0] * jnp.asarray(NA_HEAD_DIM ** -0.5, BF16)
    q4 = _stack_heads(q, heads)
    s = _dot_t(q4, k_ref[0])
    p = jnp.exp(s - jnp.max(s, axis=-1, keepdims=True))
    den = jnp.sum(p, axis=-1, keepdims=True)
    o4 = _dot(p.astype(BF16), v_ref[0]) / den
    o_ref[0] = _merge_heads(o4, heads, n).astype(BF16)


def _catt_call(obc, c):
    b, lc, _ = obc.shape
    return pl.pallas_call(
        _catt_kernel,
        grid=(b,),
        in_specs=[pl.BlockSpec((1, lc, c), lambda bi, j=j: (bi, 0, j)) for j in range(3)],
        out_specs=pl.BlockSpec((1, lc, c), lambda bi: (bi, 0, 0)),
        out_shape=jax.ShapeDtypeStruct((b, lc, c), BF16),
        compiler_params=_params(("parallel",)),
        name="ctx_attn",
    )(obc, obc, obc)


def _merge_kernel(x_ref, sh_ref, sc_ref, gt_ref, g_ref, y0_ref, y1_ref, y2_ref, y3_ref,
                  wg_ref, wb_ref, wo_ref, o_ref):
    d = x_ref.shape[2]
    x = x_ref[0]
    h = _ada_norm(x, g_ref[...], sh_ref[0, 0], sc_ref[0, 0]).astype(BF16)
    merged = None
    for bi, y_ref in enumerate((y0_ref, y1_ref, y2_ref, y3_ref)):
        gate = jax.nn.sigmoid(_dot(h, wg_ref[:, bi * d:(bi + 1) * d]))
        term = gate * _dot(y_ref[0], wb_ref[bi])
        merged = term if merged is None else merged + term
    o_ref[0] = x + gt_ref[0, 0] * _dot(merged.astype(BF16), wo_ref[...])


def _merge_call(x, mod4, g, ys, wg, wb, wo, tm):
    b, l, d = x.shape
    c = ys[0].shape[2]
    const2 = lambda bi, i: (0, 0)
    mod_spec = lambda k: pl.BlockSpec((1, 1, 1, d), lambda bi, i: (bi, k, 0, 0))
    return pl.pallas_call(
        _merge_kernel,
        grid=(b, l // tm),
        in_specs=[
            pl.BlockSpec((1, tm, d), lambda bi, i: (bi, i, 0)),
            mod_spec(0), mod_spec(1), mod_spec(2),
            pl.BlockSpec((1, d), const2),
        ] + [pl.BlockSpec((1, tm, c), lambda bi, i: (bi, i, 0)) for _ in ys] + [
            pl.BlockSpec(wg.shape, const2, pipeline_mode=pl.Buffered(1)),
            pl.BlockSpec(wb.shape, lambda bi, i: (0, 0, 0), pipeline_mode=pl.Buffered(1)),
            pl.BlockSpec(wo.shape, const2, pipeline_mode=pl.Buffered(1)),
        ],
        out_specs=pl.BlockSpec((1, tm, d), lambda bi, i: (bi, i, 0)),
        out_shape=jax.ShapeDtypeStruct((b, l, d), F32),
        compiler_params=_params(("parallel", "parallel")),
        name="merge",
    )(x, mod4, mod4, mod4, g, *ys, wg, wb, wo)


def _ffn_kernel(x_ref, xp_ref, xn_ref, sh_ref, sc_ref, gt_ref, g_ref, wu_ref, wgt_ref, cw_ref, cb_ref,
                wd_ref, fg_ref, o_ref, a_ref, *, chunk, final_norm):
    i = pl.program_id(1)
    last = pl.num_programs(1) - 1
    tm = x_ref.shape[1]
    f = wu_ref.shape[1]
    n = tm + 2 * HALO
    g, sh, sc = g_ref[...], sh_ref[0, 0], sc_ref[0, 0]
    x = x_ref[0]
    h = _ada_norm(x, g, sh, sc).astype(BF16)
    hp = _ada_norm(xp_ref[0], g, sh, sc).astype(BF16)
    hn = _ada_norm(xn_ref[0], g, sh, sc).astype(BF16)
    hext = jnp.concatenate([hp, h, hn], axis=0)
    row = lax.broadcasted_iota(jnp.int32, (n, chunk), 0)
    valid = jnp.logical_and(jnp.logical_or(row >= HALO, i > 0), jnp.logical_or(row < HALO + tm, i < last))
    for j in range(f // chunk):
        cols = slice(j * chunk, (j + 1) * chunk)
        gp = jnp.where(valid, _dot(hext, wgt_ref[:, cols]), 0.0)
        cw = cw_ref[:, cols]
        gc = (pltpu.roll(gp, 1, 0) * cw[0:1] + gp * cw[1:2] + pltpu.roll(gp, n - 1, 0) * cw[2:3]
              + cb_ref[:, cols])[HALO:HALO + tm]
        u = _dot(h, wu_ref[:, cols])
        a_ref[:, cols] = (gc * jax.nn.sigmoid(gc) * u).astype(BF16)
    y = x + gt_ref[0, 0] * _dot(a_ref[...], wd_ref[...])
    if final_norm:
        y = y * lax.rsqrt(jnp.mean(y * y, axis=-1, keepdims=True) + EPS) * fg_ref[...]
    o_ref[0] = y


def _ffn_call(x, mod4, g, wu, wgt, cw, cb, wd, fg, tm, chunk, final_norm):
    b, l, d = x.shape
    f = wu.shape[1]
    hb = tm // HALO
    nblk = l // HALO
    const2 = lambda bi, i: (0, 0)
    mod_spec = lambda k: pl.BlockSpec((1, 1, 1, d), lambda bi, i: (bi, k, 0, 0))
    single = dict(pipeline_mode=pl.Buffered(1))
    return pl.pallas_call(
        functools.partial(_ffn_kernel, chunk=chunk, final_norm=final_norm),
        grid=(b, l // tm),
        in_specs=[
            pl.BlockSpec((1, tm, d), lambda bi, i: (bi, i, 0)),
            pl.BlockSpec((1, HALO, d), lambda bi, i: (bi, jnp.maximum(i * hb - 1, 0), 0)),
            pl.BlockSpec((1, HALO, d), lambda bi, i: (bi, jnp.minimum((i + 1) * hb, nblk - 1), 0)),
            mod_spec(3), mod_spec(4), mod_spec(5),
            pl.BlockSpec((1, d), const2),
            pl.BlockSpec((d, f), const2, **single),
            pl.BlockSpec((d, f), const2, **single),
            pl.BlockSpec((cw.shape[0], f), const2),
            pl.BlockSpec((1, f), const2),
            pl.BlockSpec((f, d), const2, **single),
            pl.BlockSpec((1, d), const2),
        ],
        out_specs=pl.BlockSpec((1, tm, d), lambda bi, i: (bi, i, 0)),
        out_shape=jax.ShapeDtypeStruct((b, l, d), F32),
        scratch_shapes=[pltpu.VMEM((tm, f), BF16)],
        compiler_params=_params(("parallel", "arbitrary")),
        name="ffn",
    )(x, x, x, mod4, mod4, mod4, g, wu, wgt, cw, cb, wd, fg)


def _trig_tables(l, period):
    a_sz = 64 if l % 64 == 0 else 1
    t = jnp.arange(l, dtype=jnp.int32)[None, :]
    ja = (jnp.arange(l // a_sz, dtype=jnp.int32) * a_sz)[:, None]
    jb = jnp.arange(a_sz, dtype=jnp.int32)[:, None]
    w = 2.0 * math.pi / period
    ang_a = ((ja * t) % period).astype(F32) * w
    ang_b = ((jb * t) % period).astype(F32) * w
    ca, sa = jnp.cos(ang_a)[:, None, :], jnp.sin(ang_a)[:, None, :]
    cb, sb = jnp.cos(ang_b)[None, :, :], jnp.sin(ang_b)[None, :, :]
    cos = (ca * cb - sa * sb).reshape(l, l)
    sin = (sa * cb + ca * sb).reshape(l, l)
    return cos, sin


def _pad2(a, rows, cols):
    return jnp.pad(a, ((0, rows - a.shape[0]), (0, cols - a.shape[1])))


def _hyena_features(l, pad_to):
    t = jnp.linspace(0.0, 1.0, l, dtype=F32)[:, None]
    bands = jnp.linspace(1e-4, HYENA_BANDS - 1, HYENA_BANDS, dtype=F32)[None, :]
    ang = (2.0 * math.pi / l) * jnp.arange(l, dtype=F32)[:, None] * bands
    feats = jnp.concatenate([t, jnp.cos(ang), -jnp.sin(ang)], axis=-1)
    return jnp.pad(feats, ((0, 0), (0, pad_to - feats.shape[1])))


def _bias_classes(rows, kh):
    r = np.arange(rows)
    r0 = np.clip(r - kh // 2, 0, rows - kh)
    off = r0 - r
    lo = kh // 2
    reps = list(range(lo)) + [lo] + list(range(rows - kh + lo + 1, rows))
    return [int(off[i]) for i in reps]


def _attention_bias(rpb, rows, kh):
    heads = rpb.shape[0]
    w = GRID_W
    col = np.arange(w)
    c0 = np.clip(col - NA_WIN_W // 2, 0, w - NA_WIN_W)
    col_ok = (col[None, :] >= c0[:, None]) & (col[None, :] < c0[:, None] + NA_WIN_W)
    dc = np.clip(col[None, :] - col[:, None], 1 - NA_WIN_W, NA_WIN_W - 1) + (NA_WIN_W - 1)
    out = []
    for off in _bias_classes(rows, kh):
        dr = off + np.arange(kh) + (NA_WIN_H - 1)
        bias = rpb.astype(F32)[:, dr[None, :, None], dc[:, None, :]]
        bias = jnp.where(col_ok[None, :, None, :], bias, NEG_INF)
        out.append(bias.reshape(heads * w, kh * w))
    return jnp.stack(out)


def _mixer_inputs(x, mod4, lw, tm):
    return _proj_call(x, mod4, lw['norm1_g'], lw['wa'], lw['wb'], lw['cs'], tm)


def _hyena(oa, lw, tabs, c, nb, tm, rows):
    fc, fs = tabs['hy_cos'], tabs['hy_sin']
    uc, ucb = _dwconv_call(oa, lw['hyena_conv_w'], lw['hyena_conv_b'], c, 1, rows)
    h_hi, h_lo, row0, asum = _filt_call(tabs['feats'], lw['fw1'], lw['fb1'], lw['fw2'], lw['fb2'], lw['fw3'],
                                        lw['fb3'], lw['freq'], tabs['deltas'], min(rows * 2, oa.shape[1]))
    kr, ks, kn = _kspec_call(fc, fs, h_hi, h_lo, row0, asum, c, tm)
    spec, yn = _hfwd_call(fc, fs, ucb, 0, kr, ks, kn, 0, c, nb, tm)
    z2, z2b = _hinv_call(fc, fs, spec, yn, uc, 0, uc, 1, lw['skip'], 0, (F32, BF16), c, nb, tm)
    spec, yn = _hfwd_call(fc, fs, z2b, 0, kr, ks, kn, 1, c, nb, tm)
    (y,) = _hinv_call(fc, fs, spec, yn, z2, 0, uc, 2, lw['skip'], 1, (BF16,), c, nb, tm)
    return y


def _stream_layer(x, mod4, lw, tabs, proj, y_att, cfg, final_norm):
    c = cfg['c']
    oa, ob, of = proj
    y_pool = _pool_call(oa, lw['pool_blk'], lw['pool_scale'], cfg['rows'])
    y_fnet = _fseq_call(tabs['fn_cos'], tabs['fn_sin'], of, cfg['nb'], cfg['tseq'])
    y_hyena = _hyena(oa, lw, tabs, c, cfg['nb'], cfg['tseq'], cfg['rows'])
    x = _merge_call(x, mod4, lw['norm1_g'], (y_pool, y_fnet, y_hyena, y_att), lw['wg'], lw['w_branch'],
                    lw['w_out'], cfg['tm'])
    return _ffn_call(x, mod4, lw['norm2_g'], lw['wu'], lw['wgt'], lw['ffn_conv_w'], lw['ffn_conv_b'],
                     lw['wd'], lw['final_g'], cfg['tm'], cfg['chunk'], final_norm)


def _seq_tables(l, c):
    hy_cos, hy_sin = _trig_tables(l, 2 * l)
    fn_cos, fn_sin = _trig_tables(l, l)
    scale = 1.0 / math.sqrt(l * (c // FNET_GROUPS))
    deltas = jnp.linspace(math.log(HYENA_DECAY_TARGET) / HYENA_SLOW_DECAY,
                          math.log(HYENA_DECAY_TARGET) / HYENA_FAST_DECAY, c, dtype=F32)[None, :]
    return {
        'hy_cos': hy_cos.astype(BF16), 'hy_sin': hy_sin.astype(BF16),
        'fn_cos': (fn_cos * scale).astype(BF16), 'fn_sin': (-fn_sin * scale).astype(BF16),
        'feats': _hyena_features(l, 128), 'deltas': deltas,
    }


def kernel(x, c, ctx, c_ctx, w_mod, b_mod, norm1_g, norm2_g, w_in, pool_w, pool_scale, hyena_conv_w,
           hyena_conv_b, hyena_filt_w1, hyena_filt_b1, hyena_filt_w2, hyena_filt_b2, hyena_filt_w3,
           hyena_filt_b3, hyena_freq, hyena_skip, na_rpb, w_branch, w_out, ffn_w_up, ffn_conv_w, ffn_conv_b,
           ffn_w_down, final_norm_g):
    batch, seq, d = x.shape
    lc = ctx.shape[1]
    depth = w_mod.shape[0]
    m = d // N_BRANCH
    f = ffn_w_down.shape[1]
    rows = seq // GRID_W
    kh = min(NA_WIN_H, rows)

    cvec = jnp.concatenate([c, c_ctx[None], jnp.zeros((16 - batch - 1, d), F32)], axis=0)
    mod = _mod_call(cvec, w_mod, b_mod)
    mod_x = mod[:, :batch].reshape(depth, batch, 6, 1, d)
    mod_c = jnp.broadcast_to(mod[:, batch:batch + 1], (depth, batch, 6 * d)).reshape(depth, batch, 6, 1, d)

    gsz = m // FNET_GROUPS
    cc, ss = _trig_tables(gsz, gsz)
    eye = jnp.eye(FNET_GROUPS, dtype=F32)
    cs = jnp.concatenate([jnp.kron(eye, cc), jnp.kron(eye, ss)], axis=1).astype(BF16)

    tabs_x = _seq_tables(seq, m)
    tabs_c = _seq_tables(lc, m)
    cfg_x = dict(c=m, tm=512, rows=256, nb=2, tseq=512, chunk=256)
    cfg_c = dict(c=m, tm=lc, rows=lc, nb=2, tseq=lc, chunk=256)

    pool_off, fnet_off, hy_off, na_off = 0, m, 2 * m, 5 * m
    gate_off = 8 * m

    for l in range(depth):
        last = l == depth - 1
        wi = w_in[l]
        lw = {
            'norm1_g': norm1_g[l][None], 'norm2_g': norm2_g[l][None], 'final_g': final_norm_g[None],
            'wa': jnp.concatenate([wi[:, pool_off:pool_off + m], wi[:, hy_off:hy_off + 3 * m]], axis=1).astype(BF16),
            'wb': jnp.concatenate([wi[:, fnet_off:fnet_off + m], wi[:, na_off:na_off + 3 * m]], axis=1).astype(BF16),
            'cs': cs,
            'wg': wi[:, gate_off:].astype(BF16),
            'pool_blk': jax.scipy.linalg.block_diag(*[pool_w[l, gi] for gi in range(pool_w.shape[1])]).astype(BF16),
            'pool_scale': pool_scale[l][None],
            'hyena_conv_w': hyena_conv_w[l], 'hyena_conv_b': hyena_conv_b[l][None],
            'fw1': _pad2(hyena_filt_w1[l], 128, 128), 'fb1': _pad2(hyena_filt_b1[l][None], 1, 128),
            'fw2': _pad2(hyena_filt_w2[l], 128, 128), 'fb2': _pad2(hyena_filt_b2[l][None], 1, 128),
            'fw3': _pad2(hyena_filt_w3[l], 128, hyena_filt_w3.shape[2]), 'fb3': hyena_filt_b3[l][None],
            'freq': _pad2(hyena_freq[l][None], 1, 128),
            'skip': hyena_skip[l][:, None, :],
            'w_branch': w_branch[l].astype(BF16), 'w_out': w_out[l].astype(BF16),
            'wu': ffn_w_up[l][:, :f].astype(BF16), 'wgt': ffn_w_up[l][:, f:].astype(BF16),
            'ffn_conv_w': ffn_conv_w[l], 'ffn_conv_b': ffn_conv_b[l][None],
            'wd': ffn_w_down[l].astype(BF16),
        }
        bias = _attention_bias(na_rpb[l], rows, kh)

        proj_c = _mixer_inputs(ctx, mod_c[l], lw, cfg_c['tm'])
        proj_x = _mixer_inputs(x, mod_x[l], lw, cfg_x['tm'])
        y_att = _natt_call(proj_x[1], proj_c[1], bias, m)
        x = _stream_layer(x, mod_x[l], lw, tabs_x, proj_x, y_att, cfg_x, last)
        if not last:
            ctx = _stream_layer(ctx, mod_c[l], lw, tabs_c, proj_c, _catt_call(proj_c[1], m), cfg_c, False)
    return x
```

```python
import functools
import math

import numpy as np
import jax
import jax.numpy as jnp
from jax import lax
from jax.experimental import pallas as pl
from jax.experimental.pallas import tpu as pltpu

F32 = jnp.float32
BF16 = jnp.bfloat16

GRID_W = 64
N_BRANCH = 4
POOL_WINDOWS = (2, 4, 8, 16)
FNET_GROUPS = 4
HYENA_ORDER = 2
HYENA_BANDS = 16
HYENA_DECAY_TARGET = 1e-2
HYENA_FAST_DECAY = 0.3
HYENA_SLOW_DECAY = 1.5
HYENA_DECAY_SHIFT = 0.05
NA_HEAD_DIM = 64
NA_WIN_H = 8
NA_WIN_W = 16
EPS = 1e-6
NEG_INF = -1e30

HALO = 16
PAD = 8
VMEM_LIMIT = 56 * 1024 * 1024


def _params(sem):
    return pltpu.CompilerParams(dimension_semantics=sem, vmem_limit_bytes=VMEM_LIMIT)


def _dot(a, b):
    return jnp.dot(a, b, preferred_element_type=F32)


def _dot_t(a, b):
    return lax.dot_general(a, b, (((1,), (1,)), ((), ())), preferred_element_type=F32)


def _split_bf16(a):
    hi = a.astype(BF16)
    lo = (a - hi.astype(F32)).astype(BF16)
    return hi, lo


def _mod_kernel(c_ref, w_ref, b_ref, o_ref):
    a = c_ref[...]
    a = a * jax.nn.sigmoid(a)
    a_hi, a_lo = _split_bf16(a)
    w_hi, w_lo = _split_bf16(w_ref[0])
    o_ref[0] = _dot(a_hi, w_hi) + _dot(a_lo, w_hi) + _dot(a_hi, w_lo) + b_ref[0]


def _mod_call(cvec, w_mod, b_mod):
    depth, d, n = w_mod.shape
    rows = cvec.shape[0]
    tn = 1024
    return pl.pallas_call(
        _mod_kernel,
        grid=(depth, n // tn),
        in_specs=[
            pl.BlockSpec((rows, d), lambda l, j: (0, 0)),
            pl.BlockSpec((1, d, tn), lambda l, j: (l, 0, j)),
            pl.BlockSpec((1, 1, tn), lambda l, j: (l, 0, j)),
        ],
        out_specs=pl.BlockSpec((1, rows, tn), lambda l, j: (l, 0, j)),
        out_shape=jax.ShapeDtypeStruct((depth, rows, n), F32),
        compiler_params=_params(("arbitrary", "arbitrary")),
        name="mod",
    )(cvec, w_mod, b_mod.reshape(depth, 1, n))


def _ada_norm(x, g, shift, scale):
    y = x * lax.rsqrt(jnp.mean(x * x, axis=-1, keepdims=True) + EPS)
    return (y * g) * (1.0 + scale) + shift


def _proj_kernel(x_ref, sh_ref, sc_ref, g_ref, wa_ref, wb_ref, cs_ref, oa_ref, ob_ref, of_ref):
    m = cs_ref.shape[0]
    h = _ada_norm(x_ref[0], g_ref[...], sh_ref[0, 0], sc_ref[0, 0]).astype(BF16)
    oa_ref[0] = _dot(h, wa_ref[...])
    pb = _dot(h, wb_ref[...])
    ob_ref[0] = pb[:, m:].astype(BF16)
    of_ref[0] = _dot(pb[:, :m].astype(BF16), cs_ref[...]).astype(BF16)


def _proj_call(x, mod4, g, wa, wb, cs, tm):
    b, l, d = x.shape
    m = cs.shape[0]
    na, nb = wa.shape[1], wb.shape[1]
    const = lambda bi, i: (0, 0)
    return pl.pallas_call(
        _proj_kernel,
        grid=(b, l // tm),
        in_specs=[
            pl.BlockSpec((1, tm, d), lambda bi, i: (bi, i, 0)),
            pl.BlockSpec((1, 1, 1, d), lambda bi, i: (bi, 0, 0, 0)),
            pl.BlockSpec((1, 1, 1, d), lambda bi, i: (bi, 1, 0, 0)),
            pl.BlockSpec((1, d), const),
            pl.BlockSpec((d, na), const),
            pl.BlockSpec((d, nb), const),
            pl.BlockSpec((m, 2 * m), const),
        ],
        out_specs=[
            pl.BlockSpec((1, tm, na), lambda bi, i: (bi, i, 0)),
            pl.BlockSpec((1, tm, nb - m), lambda bi, i: (bi, i, 0)),
            pl.BlockSpec((1, tm, 2 * m), lambda bi, i: (bi, i, 0)),
        ],
        out_shape=[
            jax.ShapeDtypeStruct((b, l, na), F32),
            jax.ShapeDtypeStruct((b, l, nb - m), BF16),
            jax.ShapeDtypeStruct((b, l, 2 * m), BF16),
        ],
        compiler_params=_params(("parallel", "parallel")),
        name="proj",
    )(x, mod4, mod4, g, wa, wb, cs)


def _fill_padded(src_ref, pad_ref, l, rows):
    c = pad_ref.shape[1]
    pad_ref[0:PAD, :] = jnp.zeros((PAD, c), F32)
    pad_ref[l + PAD:l + 2 * PAD, :] = jnp.zeros((PAD, c), F32)

    def copy(i, carry):
        r = pl.multiple_of(i * rows, rows)
        pad_ref[pl.ds(r + PAD, rows), :] = src_ref[0, pl.ds(r, rows), :]
        return carry

    lax.fori_loop(0, l // rows, copy, 0)


def _pool_kernel(u_ref, w_ref, s_ref, o_ref, pad_ref, *, rows):
    l, c = u_ref.shape[1], u_ref.shape[2]
    gw = c // len(POOL_WINDOWS)
    _fill_padded(u_ref, pad_ref, l, rows)
    n = rows + 2 * PAD
    lane = lax.broadcasted_iota(jnp.int32, (n, c), 1)
    row = lax.broadcasted_iota(jnp.int32, (n, c), 0)
    grp = lax.shift_right_logical(lane, int(math.log2(gw)))
    half = jnp.where(grp == 0, 1, jnp.where(grp == 1, 2, jnp.where(grp == 2, 4, 8)))
    w = w_ref[...]
    scale = s_ref[...]

    def body(i, carry):
        r = pl.multiple_of(i * rows, rows)
        win = pad_ref[pl.ds(r, n), :]
        a2 = win + pltpu.roll(win, 1, 0)
        w4 = pltpu.roll(a2, 1, 0) + pltpu.roll(a2, n - 1, 0)
        w8 = pltpu.roll(w4, 2, 0) + pltpu.roll(w4, n - 2, 0)
        w16 = pltpu.roll(w8, 4, 0) + pltpu.roll(w8, n - 4, 0)
        s = jnp.where(grp == 0, a2, jnp.where(grp == 1, w4, jnp.where(grp == 2, w8, w16)))
        t = row + (r - PAD)
        cnt = (jnp.minimum(t + half, l) - jnp.maximum(t - half, 0)).astype(F32)
        y = (s / jnp.maximum(cnt, 1.0) - win)[PAD:PAD + rows]
        o_ref[0, pl.ds(r, rows), :] = (_dot(y.astype(BF16), w) * scale).astype(BF16)
        return carry

    lax.fori_loop(0, l // rows, body, 0)


def _pool_call(oa, w_blk, scale, rows):
    b, l, _ = oa.shape
    c = w_blk.shape[0]
    return pl.pallas_call(
        functools.partial(_pool_kernel, rows=rows),
        grid=(b,),
        in_specs=[
            pl.BlockSpec((1, l, c), lambda bi: (bi, 0, 0)),
            pl.BlockSpec((c, c), lambda bi: (0, 0)),
            pl.BlockSpec((1, c), lambda bi: (0, 0)),
        ],
        out_specs=pl.BlockSpec((1, l, c), lambda bi: (bi, 0, 0)),
        out_shape=jax.ShapeDtypeStruct((b, l, c), BF16),
        scratch_shapes=[pltpu.VMEM((l + 2 * PAD, c), F32)],
        compiler_params=_params(("parallel",)),
        name="pool",
    )(oa, w_blk, scale)


def _dwconv_kernel(u_ref, w_ref, b_ref, o_ref, ob_ref, pad_ref, *, rows):
    l = u_ref.shape[1]
    _fill_padded(u_ref, pad_ref, l, rows)
    n = rows + 2 * PAD
    w = w_ref[...]
    bias = b_ref[...]

    def body(i, carry):
        r = pl.multiple_of(i * rows, rows)
        win = pad_ref[pl.ds(r, n), :]
        y = pltpu.roll(win, 1, 0) * w[0:1] + win * w[1:2] + pltpu.roll(win, n - 1, 0) * w[2:3] + bias
        y = y[PAD:PAD + rows]
        o_ref[0, pl.ds(r, rows), :] = y
        ob_ref[0, pl.ds(r, rows), :] = y.astype(BF16)
        return carry

    lax.fori_loop(0, l // rows, body, 0)


def _dwconv_call(oa, w, bias, c, col0, rows):
    b, l, _ = oa.shape
    nblk = w.shape[1] // c
    return pl.pallas_call(
        functools.partial(_dwconv_kernel, rows=rows),
        grid=(b, nblk),
        in_specs=[
            pl.BlockSpec((1, l, c), lambda bi, j: (bi, 0, j + col0)),
            pl.BlockSpec((w.shape[0], c), lambda bi, j: (0, j)),
            pl.BlockSpec((1, c), lambda bi, j: (0, j)),
        ],
        out_specs=[
            pl.BlockSpec((1, l, c), lambda bi, j: (bi, 0, j)),
            pl.BlockSpec((1, l, c), lambda bi, j: (bi, 0, j)),
        ],
        out_shape=[
            jax.ShapeDtypeStruct((b, l, nblk * c), F32),
            jax.ShapeDtypeStruct((b, l, nblk * c), BF16),
        ],
        scratch_shapes=[pltpu.VMEM((l + 2 * PAD, c), F32)],
        compiler_params=_params(("parallel", "parallel")),
        name="dwconv",
    )(oa, w, bias)


def _filt_kernel(f_ref, w1_ref, b1_ref, w2_ref, b2_ref, w3_ref, b3_ref, fr_ref, dl_ref,
                 hi_ref, lo_ref, row0_ref, asum_ref):
    hp = lax.Precision.HIGHEST
    feats = f_ref[...]
    freq = fr_ref[...]
    h = jnp.sin(freq * (jnp.dot(feats, w1_ref[...], precision=hp, preferred_element_type=F32) + b1_ref[...]))
    h = jnp.sin(freq * (jnp.dot(h, w2_ref[...], precision=hp, preferred_element_type=F32) + b2_ref[...]))
    h = jnp.dot(h, w3_ref[...], precision=hp, preferred_element_type=F32) + b3_ref[...]
    t = feats[:, 0:1]
    win = jnp.exp(-t * jnp.abs(dl_ref[...])) + HYENA_DECAY_SHIFT
    h = h * jnp.concatenate([win] * (h.shape[1] // win.shape[1]), axis=1)
    hi, lo = _split_bf16(h)
    hi_ref[...] = hi
    lo_ref[...] = lo
    part = jnp.sum(jnp.abs(h), axis=0, keepdims=True)

    @pl.when(pl.program_id(0) == 0)
    def _():
        asum_ref[...] = jnp.zeros_like(asum_ref)
        row0_ref[...] = h[0:8]

    asum_ref[...] += part


def _filt_call(feats, w1, b1, w2, b2, w3, b3, freq, deltas, rows):
    l, fd = feats.shape
    hd = w2.shape[0]
    n = w3.shape[1]
    c = deltas.shape[1]
    const = lambda i: (0, 0)
    return pl.pallas_call(
        _filt_kernel,
        grid=(l // rows,),
        in_specs=[
            pl.BlockSpec((rows, fd), lambda i: (i, 0)),
            pl.BlockSpec((fd, hd), const), pl.BlockSpec((1, hd), const),
            pl.BlockSpec((hd, hd), const), pl.BlockSpec((1, hd), const),
            pl.BlockSpec((hd, n), const), pl.BlockSpec((1, n), const),
            pl.BlockSpec((1, hd), const), pl.BlockSpec((1, c), const),
        ],
        out_specs=[
            pl.BlockSpec((rows, n), lambda i: (i, 0)),
            pl.BlockSpec((rows, n), lambda i: (i, 0)),
            pl.BlockSpec((8, n), const),
            pl.BlockSpec((1, n), const),
        ],
        out_shape=[
            jax.ShapeDtypeStruct((l, n), BF16),
            jax.ShapeDtypeStruct((l, n), BF16),
            jax.ShapeDtypeStruct((8, n), F32),
            jax.ShapeDtypeStruct((1, n), F32),
        ],
        compiler_params=_params(("arbitrary",)),
        name="hyena_filt",
    )(feats, w1, b1, w2, b2, w3, b3, freq, deltas)


def _alt_signs(l):
    t = lax.broadcasted_iota(jnp.int32, (16, l), 1)
    return jnp.where((t & 1) == 0, 1.0, -1.0).astype(BF16)


def _kspec_kernel(fc_ref, fs_ref, hi_ref, lo_ref, row0_ref, asum_ref, kr_ref, ks_ref, kn_ref):
    i = pl.program_id(1)
    tm, l = fc_ref.shape
    c = kr_ref.shape[2]
    fc, fs = fc_ref[...], fs_ref[...]
    hi, lo = hi_ref[...], lo_ref[...]
    hc = _dot(fc, hi) + _dot(fc, lo)
    hs = _dot(fs, hi) + _dot(fs, lo)
    asum = asum_ref[...]
    inv = 1.0 / (asum[:, :c] + asum[:, c:] + EPS)
    hb0 = row0_ref[0:1, c:]
    row = lax.broadcasted_iota(jnp.int32, (tm, c), 0) + i * tm
    wj = jnp.where(row == 0, 0.5 / l, 1.0 / l)
    kr_ref[0] = (hc[:, :c] + hc[:, c:] - hb0) * inv * wj
    ks_ref[0] = (hs[:, :c] - hs[:, c:]) * inv * wj

    @pl.when(i == 0)
    def _():
        sg = _alt_signs(l)
        alt = (_dot(sg, hi) + _dot(sg, lo))[0:8]
        kn_ref[0] = (alt[:, :c] + alt[:, c:] - hb0) * inv * (0.5 / l)


def _kspec_call(fc, fs, h_hi, h_lo, row0, asum, c, tm):
    l = fc.shape[0]
    orders = h_hi.shape[1] // (2 * c)
    return pl.pallas_call(
        _kspec_kernel,
        grid=(orders, l // tm),
        in_specs=[
            pl.BlockSpec((tm, l), lambda o, i: (i, 0)),
            pl.BlockSpec((tm, l), lambda o, i: (i, 0)),
            pl.BlockSpec((l, 2 * c), lambda o, i: (0, o)),
            pl.BlockSpec((l, 2 * c), lambda o, i: (0, o)),
            pl.BlockSpec((8, 2 * c), lambda o, i: (0, o)),
            pl.BlockSpec((1, 2 * c), lambda o, i: (0, o)),
        ],
        out_specs=[
            pl.BlockSpec((1, tm, c), lambda o, i: (o, i, 0)),
            pl.BlockSpec((1, tm, c), lambda o, i: (o, i, 0)),
            pl.BlockSpec((1, 8, c), lambda o, i: (o, 0, 0)),
        ],
        out_shape=[
            jax.ShapeDtypeStruct((orders, l, c), F32),
            jax.ShapeDtypeStruct((orders, l, c), F32),
            jax.ShapeDtypeStruct((orders, 8, c), F32),
        ],
        compiler_params=_params(("arbitrary", "arbitrary")),
        name="hyena_kspec",
    )(fc, fs, h_hi, h_lo, row0, asum)


def _hfwd_kernel(fc_ref, fs_ref, z_ref, kr_ref, ks_ref, kn_ref, spec_ref, yn_ref):
    nb = z_ref.shape[0]
    c = z_ref.shape[2]
    fc, fs = fc_ref[...], fs_ref[...]
    kr, ks = kr_ref[0], ks_ref[0]
    for bb in range(nb):
        z = z_ref[bb]
        zr = _dot(fc, z)
        zs = _dot(fs, z)
        spec_ref[bb, :, 0:c] = (zr * kr - zs * ks).astype(BF16)
        spec_ref[bb, :, c:2 * c] = (zr * ks + zs * kr).astype(BF16)

    @pl.when(pl.program_id(1) == 0)
    def _():
        sg = _alt_signs(z_ref.shape[1])
        for bb in range(nb):
            yn_ref[bb] = _dot(sg, z_ref[bb])[0:8] * kn_ref[0]


def _hfwd_call(fc, fs, zb, zcol, kr, ks, kn, order, c, nb, tm):
    b, l, _ = zb.shape
    return pl.pallas_call(
        _hfwd_kernel,
        grid=(b // nb, l // tm),
        in_specs=[
            pl.BlockSpec((tm, l), lambda g, i: (i, 0)),
            pl.BlockSpec((tm, l), lambda g, i: (i, 0)),
            pl.BlockSpec((nb, l, c), lambda g, i: (g, 0, zcol)),
            pl.BlockSpec((1, tm, c), lambda g, i: (order, i, 0)),
            pl.BlockSpec((1, tm, c), lambda g, i: (order, i, 0)),
            pl.BlockSpec((1, 8, c), lambda g, i: (order, 0, 0)),
        ],
        out_specs=[
            pl.BlockSpec((nb, tm, 2 * c), lambda g, i: (g, i, 0)),
            pl.BlockSpec((nb, 8, c), lambda g, i: (g, 0, 0)),
        ],
        out_shape=[
            jax.ShapeDtypeStruct((b, l, 2 * c), BF16),
            jax.ShapeDtypeStruct((b, 8, c), F32),
        ],
        compiler_params=_params(("arbitrary", "arbitrary")),
        name="hyena_fwd",
    )(fc, fs, zb, kr, ks, kn)


def _hinv_kernel(fc_ref, fs_ref, spec_ref, yn_ref, z_ref, gate_ref, skip_ref, *out_refs):
    nb, tm, c = z_ref.shape
    fc, fs = fc_ref[...], fs_ref[...]
    row = lax.broadcasted_iota(jnp.int32, (tm, c), 0)
    skip = skip_ref[0]
    for bb in range(nb):
        y = _dot(fc, spec_ref[bb, :, 0:c]) + _dot(fs, spec_ref[bb, :, c:2 * c])
        yn = yn_ref[bb, 0:1, :]
        y = y + jnp.where((row & 1) == 0, yn, -yn)
        out = gate_ref[bb] * (y + skip * z_ref[bb])
        for o_ref in out_refs:
            o_ref[bb] = out.astype(o_ref.dtype)


def _hinv_call(fc, fs, spec, yn, zf, zcol, uc, gcol, skip, order, out_dtypes, c, nb, tm):
    b, l, _ = spec.shape
    return pl.pallas_call(
        _hinv_kernel,
        grid=(b // nb, l // tm),
        in_specs=[
            pl.BlockSpec((tm, l), lambda g, i: (i, 0)),
            pl.BlockSpec((tm, l), lambda g, i: (i, 0)),
            pl.BlockSpec((nb, l, 2 * c), lambda g, i: (g, 0, 0)),
            pl.BlockSpec((nb, 8, c), lambda g, i: (g, 0, 0)),
            pl.BlockSpec((nb, tm, c), lambda g, i: (g, i, zcol)),
            pl.BlockSpec((nb, tm, c), lambda g, i: (g, i, gcol)),
            pl.BlockSpec((1, 1, c), lambda g, i: (order, 0, 0)),
        ],
        out_specs=[pl.BlockSpec((nb, tm, c), lambda g, i: (g, i, 0)) for _ in out_dtypes],
        out_shape=[jax.ShapeDtypeStruct((b, l, c), dt) for dt in out_dtypes],
        compiler_params=_params(("arbitrary", "arbitrary")),
        name="hyena_inv",
    )(fc, fs, spec, yn, zf, uc, skip)


def _fseq_kernel(mc_ref, ms_ref, u_ref, o_ref):
    nb = u_ref.shape[0]
    c = o_ref.shape[2]
    mc, ms = mc_ref[...], ms_ref[...]
    for bb in range(nb):
        o_ref[bb] = (_dot(mc, u_ref[bb, :, 0:c]) + _dot(ms, u_ref[bb, :, c:2 * c])).astype(BF16)


def _fseq_call(mc, ms, ucs, nb, tm):
    b, l, c2 = ucs.shape
    c = c2 // 2
    return pl.pallas_call(
        _fseq_kernel,
        grid=(b // nb, l // tm),
        in_specs=[
            pl.BlockSpec((tm, l), lambda g, i: (i, 0)),
            pl.BlockSpec((tm, l), lambda g, i: (i, 0)),
            pl.BlockSpec((nb, l, c2), lambda g, i: (g, 0, 0)),
        ],
        out_specs=pl.BlockSpec((nb, tm, c), lambda g, i: (g, i, 0)),
        out_shape=jax.ShapeDtypeStruct((b, l, c), BF16),
        compiler_params=_params(("arbitrary", "arbitrary")),
        name="fnet_seq",
    )(mc, ms, ucs)


def _stack_heads(q, heads):
    lane = lax.broadcasted_iota(jnp.int32, q.shape, 1)
    zero = jnp.zeros_like(q)
    return jnp.concatenate([jnp.where(lax.shift_right_logical(lane, int(math.log2(NA_HEAD_DIM))) == h, q, zero) for h in range(heads)], axis=0)


def _merge_heads(o, heads, n):
    lane = lax.broadcasted_iota(jnp.int32, (n, o.shape[1]), 1)
    out = jnp.zeros((n, o.shape[1]), F32)
    for h in range(heads):
        out = out + jnp.where(lax.shift_right_logical(lane, int(math.log2(NA_HEAD_DIM))) == h, o[h * n:(h + 1) * n], 0.0)
    return out


def _natt_kernel(q_ref, k_ref, v_ref, kc_ref, vc_ref, bias_ref, o_ref, *, rows, kh, group):
    w = GRID_W
    heads = q_ref.shape[2] // NA_HEAD_DIM
    base = pl.program_id(1) * group
    lo = kh // 2
    kc, vc = kc_ref[0], vc_ref[0]
    scale = jnp.asarray(NA_HEAD_DIM ** -0.5, BF16)

    def body(j, carry):
        r = base + j
        r0 = jnp.clip(r - lo, 0, rows - kh)
        start = pl.multiple_of(r0 * w, w)
        cls = jnp.minimum(r, lo) + jnp.maximum(r - (rows - kh + lo), 0)
        q4 = _stack_heads(q_ref[0, pl.ds(pl.multiple_of(j * w, w), w), :] * scale, heads)
        ks = k_ref[0, pl.ds(start, kh * w), :]
        vs = v_ref[0, pl.ds(start, kh * w), :]
        s_nb = _dot_t(q4, ks) + bias_ref[cls]
        s_cx = _dot_t(q4, kc)
        m = jnp.maximum(jnp.max(s_nb, axis=-1, keepdims=True), jnp.max(s_cx, axis=-1, keepdims=True))
        p_nb = jnp.exp(s_nb - m)
        p_cx = jnp.exp(s_cx - m)
        den = jnp.sum(p_nb, axis=-1, keepdims=True) + jnp.sum(p_cx, axis=-1, keepdims=True)
        o4 = (_dot(p_nb.astype(BF16), vs) + _dot(p_cx.astype(BF16), vc)) / den
        o_ref[0, pl.ds(pl.multiple_of(j * w, w), w), :] = _merge_heads(o4, heads, w).astype(BF16)
        return carry

    lax.fori_loop(0, group, body, 0, unroll=4)


def _natt_call(ob, obc, bias, c, group):
    b, l, _ = ob.shape
    lc = obc.shape[1]
    w = GRID_W
    rows = l // w
    kh = min(NA_WIN_H, rows)
    return pl.pallas_call(
        functools.partial(_natt_kernel, rows=rows, kh=kh, group=group),
        grid=(b, rows // group),
        in_specs=[
            pl.BlockSpec((1, group * w, c), lambda bi, r: (bi, r, 0)),
            pl.BlockSpec((1, l, c), lambda bi, r: (bi, 0, 1)),
            pl.BlockSpec((1, l, c), lambda bi, r: (bi, 0, 2)),
            pl.BlockSpec((1, lc, c), lambda bi, r: (bi, 0, 1)),
            pl.BlockSpec((1, lc, c), lambda bi, r: (bi, 0, 2)),
            pl.BlockSpec(bias.shape, lambda bi, r: (0, 0, 0)),
        ],
        out_specs=pl.BlockSpec((1, group * w, c), lambda bi, r: (bi, r, 0)),
        out_shape=jax.ShapeDtypeStruct((b, l, c), BF16),
        compiler_params=_params(("parallel", "arbitrary")),
        name="nb_attn",
    )(ob, ob, ob, obc, obc, bias)


def _bias_kernel(r_ref, o_ref):
    w = GRID_W
    shift = int(math.log2(w))
    n = o_ref.shape[1]
    r = r_ref[...]
    hi = r.astype(BF16)
    mid = (r - hi.astype(F32)).astype(BF16)
    lo = (r - hi.astype(F32) - mid.astype(F32)).astype(BF16)
    dc = lax.broadcasted_iota(jnp.int32, (r.shape[1], n), 0)
    p = lax.broadcasted_iota(jnp.int32, (r.shape[1], n), 1)
    idx = jnp.clip((p & (w - 1)) - lax.shift_right_logical(p, shift), 1 - NA_WIN_W, NA_WIN_W - 1) + (NA_WIN_W - 1)
    onehot = jnp.where(dc == idx, 1.0, 0.0).astype(BF16)
    val = _dot(hi, onehot) + _dot(mid, onehot) + _dot(lo, onehot)
    po = lax.broadcasted_iota(jnp.int32, val.shape, 1)
    cq = lax.shift_right_logical(po, shift)
    ck = po & (w - 1)
    c0 = jnp.clip(cq - NA_WIN_W // 2, 0, w - NA_WIN_W)
    inside = jnp.logical_and(ck >= c0, ck < c0 + NA_WIN_W)
    o_ref[...] = jnp.where(inside, val, NEG_INF)


def _bias_call(rpb2d):
    rows, cols = rpb2d.shape
    n = GRID_W * GRID_W
    return pl.pallas_call(
        _bias_kernel,
        grid=(1,),
        in_specs=[pl.BlockSpec((rows, cols), lambda i: (0, 0))],
        out_specs=pl.BlockSpec((rows, n), lambda i: (0, 0)),
        out_shape=jax.ShapeDtypeStruct((rows, n), F32),
        compiler_params=_params(("arbitrary",)),
        name="rpb_table",
    )(rpb2d)


def _catt_kernel(q_ref, k_ref, v_ref, o_ref):
    n = q_ref.shape[1]
    heads = q_ref.shape[2] // NA_HEAD_DIM
    q = q_ref[0] * jnp.asarray(NA_HEAD_DIM ** -0.5, BF16)
    q4 = _stack_heads(q, heads)
    s = _dot_t(q4, k_ref[0])
    p = jnp.exp(s - jnp.max(s, axis=-1, keepdims=True))
    den = jnp.sum(p, axis=-1, keepdims=True)
    o4 = _dot(p.astype(BF16), v_ref[0]) / den
    o_ref[0] = _merge_heads(o4, heads, n).astype(BF16)


def _catt_call(obc, c):
    b, lc, _ = obc.shape
    return pl.pallas_call(
        _catt_kernel,
        grid=(b,),
        in_specs=[pl.BlockSpec((1, lc, c), lambda bi, j=j: (bi, 0, j)) for j in range(3)],
        out_specs=pl.BlockSpec((1, lc, c), lambda bi: (bi, 0, 0)),
        out_shape=jax.ShapeDtypeStruct((b, lc, c), BF16),
        compiler_params=_params(("parallel",)),
        name="ctx_attn",
    )(obc, obc, obc)


def _merge_kernel(x_ref, sh_ref, sc_ref, gt_ref, g_ref, y0_ref, y1_ref, y2_ref, y3_ref,
                  wg_ref, wb_ref, wo_ref, o_ref):
    d = x_ref.shape[2]
    x = x_ref[0]
    h = _ada_norm(x, g_ref[...], sh_ref[0, 0], sc_ref[0, 0]).astype(BF16)
    merged = None
    for bi, y_ref in enumerate((y0_ref, y1_ref, y2_ref, y3_ref)):
        gate = jax.nn.sigmoid(_dot(h, wg_ref[:, bi * d:(bi + 1) * d]))
        term = gate * _dot(y_ref[0], wb_ref[bi])
        merged = term if merged is None else merged + term
    o_ref[0] = x + gt_ref[0, 0] * _dot(merged.astype(BF16), wo_ref[...])


def _merge_call(x, mod4, g, ys, wg, wb, wo, tm):
    b, l, d = x.shape
    c = ys[0].shape[2]
    const2 = lambda bi, i: (0, 0)
    mod_spec = lambda k: pl.BlockSpec((1, 1, 1, d), lambda bi, i: (bi, k, 0, 0))
    return pl.pallas_call(
        _merge_kernel,
        grid=(b, l // tm),
        in_specs=[
            pl.BlockSpec((1, tm, d), lambda bi, i: (bi, i, 0)),
            mod_spec(0), mod_spec(1), mod_spec(2),
            pl.BlockSpec((1, d), const2),
        ] + [pl.BlockSpec((1, tm, c), lambda bi, i: (bi, i, 0)) for _ in ys] + [
            pl.BlockSpec(wg.shape, const2, pipeline_mode=pl.Buffered(1)),
            pl.BlockSpec(wb.shape, lambda bi, i: (0, 0, 0), pipeline_mode=pl.Buffered(1)),
            pl.BlockSpec(wo.shape, const2, pipeline_mode=pl.Buffered(1)),
        ],
        out_specs=pl.BlockSpec((1, tm, d), lambda bi, i: (bi, i, 0)),
        out_shape=jax.ShapeDtypeStruct((b, l, d), F32),
        compiler_params=_params(("parallel", "parallel")),
        name="merge",
    )(x, mod4, mod4, mod4, g, *ys, wg, wb, wo)


def _ffn_kernel(x_ref, xp_ref, xn_ref, sh_ref, sc_ref, gt_ref, g_ref, wu_ref, wgt_ref, cw_ref, cb_ref,
                wd_ref, fg_ref, o_ref, a_ref, *, chunk, final_norm):
    i = pl.program_id(1)
    last = pl.num_programs(1) - 1
    tm = x_ref.shape[1]
    f = wu_ref.shape[1]
    n = tm + 2 * HALO
    g, sh, sc = g_ref[...], sh_ref[0, 0], sc_ref[0, 0]
    x = x_ref[0]
    h = _ada_norm(x, g, sh, sc).astype(BF16)
    hp = _ada_norm(xp_ref[0], g, sh, sc).astype(BF16)
    hn = _ada_norm(xn_ref[0], g, sh, sc).astype(BF16)
    hext = jnp.concatenate([hp, h, hn], axis=0)
    row = lax.broadcasted_iota(jnp.int32, (n, chunk), 0)
    valid = jnp.logical_and(jnp.logical_or(row >= HALO, i > 0), jnp.logical_or(row < HALO + tm, i < last))
    for j in range(f // chunk):
        cols = slice(j * chunk, (j + 1) * chunk)
        gp = jnp.where(valid, _dot(hext, wgt_ref[:, cols]), 0.0)
        cw = cw_ref[:, cols]
        gc = (pltpu.roll(gp, 1, 0) * cw[0:1] + gp * cw[1:2] + pltpu.roll(gp, n - 1, 0) * cw[2:3]
              + cb_ref[:, cols])[HALO:HALO + tm]
        u = _dot(h, wu_ref[:, cols])
        a_ref[:, cols] = (gc * jax.nn.sigmoid(gc) * u).astype(BF16)
    y = x + gt_ref[0, 0] * _dot(a_ref[...], wd_ref[...])
    if final_norm:
        y = y * lax.rsqrt(jnp.mean(y * y, axis=-1, keepdims=True) + EPS) * fg_ref[...]
    o_ref[0] = y


def _ffn_call(x, mod4, g, wu, wgt, cw, cb, wd, fg, tm, chunk, final_norm):
    b, l, d = x.shape
    f = wu.shape[1]
    hb = tm // HALO
    nblk = l // HALO
    const2 = lambda bi, i: (0, 0)
    mod_spec = lambda k: pl.BlockSpec((1, 1, 1, d), lambda bi, i: (bi, k, 0, 0))
    single = dict(pipeline_mode=pl.Buffered(1))
    return pl.pallas_call(
        functools.partial(_ffn_kernel, chunk=chunk, final_norm=final_norm),
        grid=(b, l // tm),
        in_specs=[
            pl.BlockSpec((1, tm, d), lambda bi, i: (bi, i, 0)),
            pl.BlockSpec((1, HALO, d), lambda bi, i: (bi, jnp.maximum(i * hb - 1, 0), 0)),
            pl.BlockSpec((1, HALO, d), lambda bi, i: (bi, jnp.minimum((i + 1) * hb, nblk - 1), 0)),
            mod_spec(3), mod_spec(4), mod_spec(5),
            pl.BlockSpec((1, d), const2),
            pl.BlockSpec((d, f), const2, **single),
            pl.BlockSpec((d, f), const2, **single),
            pl.BlockSpec((cw.shape[0], f), const2),
            pl.BlockSpec((1, f), const2),
            pl.BlockSpec((f, d), const2, **single),
            pl.BlockSpec((1, d), const2),
        ],
        out_specs=pl.BlockSpec((1, tm, d), lambda bi, i: (bi, i, 0)),
        out_shape=jax.ShapeDtypeStruct((b, l, d), F32),
        scratch_shapes=[pltpu.VMEM((tm, f), BF16)],
        compiler_params=_params(("parallel", "arbitrary")),
        name="ffn",
    )(x, x, x, mod4, mod4, mod4, g, wu, wgt, cw, cb, wd, fg)


def _trig_tables(l, period):
    a_sz = 64 if l % 64 == 0 else 1
    t = jnp.arange(l, dtype=jnp.int32)[None, :]
    ja = (jnp.arange(l // a_sz, dtype=jnp.int32) * a_sz)[:, None]
    jb = jnp.arange(a_sz, dtype=jnp.int32)[:, None]
    w = 2.0 * math.pi / period
    ang_a = ((ja * t) % period).astype(F32) * w
    ang_b = ((jb * t) % period).astype(F32) * w
    ca, sa = jnp.cos(ang_a)[:, None, :], jnp.sin(ang_a)[:, None, :]
    cb, sb = jnp.cos(ang_b)[None, :, :], jnp.sin(ang_b)[None, :, :]
    cos = (ca * cb - sa * sb).reshape(l, l)
    sin = (sa * cb + ca * sb).reshape(l, l)
    return cos, sin


def _pad2(a, rows, cols):
    return jnp.pad(a, ((0, rows - a.shape[0]), (0, cols - a.shape[1])))


def _hyena_features(l, pad_to):
    t = jnp.linspace(0.0, 1.0, l, dtype=F32)[:, None]
    bands = jnp.linspace(1e-4, HYENA_BANDS - 1, HYENA_BANDS, dtype=F32)[None, :]
    ang = (2.0 * math.pi / l) * jnp.arange(l, dtype=F32)[:, None] * bands
    feats = jnp.concatenate([t, jnp.cos(ang), -jnp.sin(ang)], axis=-1)
    return jnp.pad(feats, ((0, 0), (0, pad_to - feats.shape[1])))


def _bias_classes(rows, kh):
    r = np.arange(rows)
    r0 = np.clip(r - kh // 2, 0, rows - kh)
    off = r0 - r
    lo = kh // 2
    reps = list(range(lo)) + [lo] + list(range(rows - kh + lo + 1, rows))
    return [int(off[i]) for i in reps]


def _attention_bias(rpb, rows, kh):
    heads, nr, nc = rpb.shape
    w = GRID_W
    flat = rpb.astype(F32).reshape(heads * nr, nc)
    table = _bias_call(_pad2(flat, -(-heads * nr // 16) * 16, 128))
    table = table[:heads * nr].reshape(heads, nr, w, w)
    out = []
    for off in _bias_classes(rows, kh):
        first = off + NA_WIN_H - 1
        sl = table[:, first:first + kh]
        out.append(jnp.transpose(sl, (0, 2, 1, 3)).reshape(heads * w, kh * w))
    return jnp.stack(out)


def _mixer_inputs(x, mod4, lw, tm):
    return _proj_call(x, mod4, lw['norm1_g'], lw['wa'], lw['wb'], lw['cs'], tm)


def _hyena(oa, lw, tabs, c, nb, tm, rows):
    fc, fs = tabs['hy_cos'], tabs['hy_sin']
    uc, ucb = _dwconv_call(oa, lw['hyena_conv_w'], lw['hyena_conv_b'], c, 1, rows)
    h_hi, h_lo, row0, asum = _filt_call(tabs['feats'], lw['fw1'], lw['fb1'], lw['fw2'], lw['fb2'], lw['fw3'],
                                        lw['fb3'], lw['freq'], tabs['deltas'], min(rows * 2, oa.shape[1]))
    kr, ks, kn = _kspec_call(fc, fs, h_hi, h_lo, row0, asum, c, tm)
    spec, yn = _hfwd_call(fc, fs, ucb, 0, kr, ks, kn, 0, c, nb, tm)
    z2, z2b = _hinv_call(fc, fs, spec, yn, uc, 0, uc, 1, lw['skip'], 0, (F32, BF16), c, nb, tm)
    spec, yn = _hfwd_call(fc, fs, z2b, 0, kr, ks, kn, 1, c, nb, tm)
    (y,) = _hinv_call(fc, fs, spec, yn, z2, 0, uc, 2, lw['skip'], 1, (BF16,), c, nb, tm)
    return y


def _stream_layer(x, mod4, lw, tabs, proj, y_att, cfg, final_norm):
    c = cfg['c']
    oa, ob, of = proj
    y_pool = _pool_call(oa, lw['pool_blk'], lw['pool_scale'], cfg['rows'])
    y_fnet = _fseq_call(tabs['fn_cos'], tabs['fn_sin'], of, cfg['nb'], cfg['tseq'])
    y_hyena = _hyena(oa, lw, tabs, c, cfg['nb'], cfg['tseq'], cfg['rows'])
    x = _merge_call(x, mod4, lw['norm1_g'], (y_pool, y_fnet, y_hyena, y_att), lw['wg'], lw['w_branch'],
                    lw['w_out'], cfg['tm'])
    return _ffn_call(x, mod4, lw['norm2_g'], lw['wu'], lw['wgt'], lw['ffn_conv_w'], lw['ffn_conv_b'],
                     lw['wd'], lw['final_g'], cfg['tm'], cfg['chunk'], final_norm)


def _seq_tables(l, c):
    hy_cos, hy_sin = _trig_tables(l, 2 * l)
    fn_cos, fn_sin = _trig_tables(l, l)
    scale = 1.0 / math.sqrt(l * (c // FNET_GROUPS))
    deltas = jnp.linspace(math.log(HYENA_DECAY_TARGET) / HYENA_SLOW_DECAY,
                          math.log(HYENA_DECAY_TARGET) / HYENA_FAST_DECAY, c, dtype=F32)[None, :]
    return {
        'hy_cos': hy_cos.astype(BF16), 'hy_sin': hy_sin.astype(BF16),
        'fn_cos': (fn_cos * scale).astype(BF16), 'fn_sin': (-fn_sin * scale).astype(BF16),
        'feats': _hyena_features(l, 128), 'deltas': deltas,
    }


def kernel(x, c, ctx, c_ctx, w_mod, b_mod, norm1_g, norm2_g, w_in, pool_w, pool_scale, hyena_conv_w,
           hyena_conv_b, hyena_filt_w1, hyena_filt_b1, hyena_filt_w2, hyena_filt_b2, hyena_filt_w3,
           hyena_filt_b3, hyena_freq, hyena_skip, na_rpb, w_branch, w_out, ffn_w_up, ffn_conv_w, ffn_conv_b,
           ffn_w_down, final_norm_g):
    batch, seq, d = x.shape
    lc = ctx.shape[1]
    depth = w_mod.shape[0]
    m = d // N_BRANCH
    f = ffn_w_down.shape[1]
    rows = seq // GRID_W
    kh = min(NA_WIN_H, rows)

    cvec = jnp.concatenate([c, c_ctx[None], jnp.zeros((16 - batch - 1, d), F32)], axis=0)
    mod = _mod_call(cvec, w_mod, b_mod)
    mod_x = mod[:, :batch].reshape(depth, batch, 6, 1, d)
    mod_c = jnp.broadcast_to(mod[:, batch:batch + 1], (depth, batch, 6 * d)).reshape(depth, batch, 6, 1, d)

    gsz = m // FNET_GROUPS
    cc, ss = _trig_tables(gsz, gsz)
    eye = jnp.eye(FNET_GROUPS, dtype=F32)
    cs = jnp.concatenate([jnp.kron(eye, cc), jnp.kron(eye, ss)], axis=1).astype(BF16)

    tabs_x = _seq_tables(seq, m)
    tabs_c = _seq_tables(lc, m)
    cfg_x = dict(c=m, tm=512, rows=256, nb=2, tseq=512, chunk=256, group=8)
    cfg_c = dict(c=m, tm=lc, rows=lc, nb=2, tseq=lc, chunk=256)

    pool_off, fnet_off, hy_off, na_off = 0, m, 2 * m, 5 * m
    gate_off = 8 * m

    for l in range(depth):
        last = l == depth - 1
        wi = w_in[l]
        lw = {
            'norm1_g': norm1_g[l][None], 'norm2_g': norm2_g[l][None], 'final_g': final_norm_g[None],
            'wa': jnp.concatenate([wi[:, pool_off:pool_off + m], wi[:, hy_off:hy_off + 3 * m]], axis=1).astype(BF16),
            'wb': jnp.concatenate([wi[:, fnet_off:fnet_off + m], wi[:, na_off:na_off + 3 * m]], axis=1).astype(BF16),
            'cs': cs,
            'wg': wi[:, gate_off:].astype(BF16),
            'pool_blk': jax.scipy.linalg.block_diag(*[pool_w[l, gi] for gi in range(pool_w.shape[1])]).astype(BF16),
            'pool_scale': pool_scale[l][None],
            'hyena_conv_w': hyena_conv_w[l], 'hyena_conv_b': hyena_conv_b[l][None],
            'fw1': _pad2(hyena_filt_w1[l], 128, 128), 'fb1': _pad2(hyena_filt_b1[l][None], 1, 128),
            'fw2': _pad2(hyena_filt_w2[l], 128, 128), 'fb2': _pad2(hyena_filt_b2[l][None], 1, 128),
            'fw3': _pad2(hyena_filt_w3[l], 128, hyena_filt_w3.shape[2]), 'fb3': hyena_filt_b3[l][None],
            'freq': _pad2(hyena_freq[l][None], 1, 128),
            'skip': hyena_skip[l][:, None, :],
            'w_branch': w_branch[l].astype(BF16), 'w_out': w_out[l].astype(BF16),
            'wu': ffn_w_up[l][:, :f].astype(BF16), 'wgt': ffn_w_up[l][:, f:].astype(BF16),
            'ffn_conv_w': ffn_conv_w[l], 'ffn_conv_b': ffn_conv_b[l][None],
            'wd': ffn_w_down[l].astype(BF16),
        }
        bias = _attention_bias(na_rpb[l], rows, kh)

        proj_c = _mixer_inputs(ctx, mod_c[l], lw, cfg_c['tm'])
        proj_x = _mixer_inputs(x, mod_x[l], lw, cfg_x['tm'])
        y_att = _natt_call(proj_x[1], proj_c[1], bias, m, cfg_x['group'])
        x = _stream_layer(x, mod_x[l], lw, tabs_x, proj_x, y_att, cfg_x, last)
        if not last:
            ctx = _stream_layer(ctx, mod_c[l], lw, tabs_c, proj_c, _catt_call(proj_c[1], m), cfg_c, False)
    return x
```

```python
import functools
import math

import numpy as np
import jax
import jax.numpy as jnp
from jax import lax
from jax.experimental import pallas as pl
from jax.experimental.pallas import tpu as pltpu

F32 = jnp.float32
BF16 = jnp.bfloat16

GRID_W = 64
N_BRANCH = 4
POOL_WINDOWS = (2, 4, 8, 16)
FNET_GROUPS = 4
HYENA_ORDER = 2
HYENA_BANDS = 16
HYENA_DECAY_TARGET = 1e-2
HYENA_FAST_DECAY = 0.3
HYENA_SLOW_DECAY = 1.5
HYENA_DECAY_SHIFT = 0.05
NA_HEAD_DIM = 64
NA_WIN_H = 8
NA_WIN_W = 16
EPS = 1e-6
NEG_INF = -1e30

LANES = 128
HALO = 16
PAD = 8
VMEM_LIMIT = 56 * 1024 * 1024


def _params(sem):
    return pltpu.CompilerParams(dimension_semantics=sem, vmem_limit_bytes=VMEM_LIMIT)


def _dot(a, b):
    return jnp.dot(a, b, preferred_element_type=F32)


def _dot_t(a, b):
    return lax.dot_general(a, b, (((1,), (1,)), ((), ())), preferred_element_type=F32)


def _split_bf16(a):
    hi = a.astype(BF16)
    lo = (a - hi.astype(F32)).astype(BF16)
    return hi, lo


def _mod_kernel(c_ref, w_ref, b_ref, o_ref):
    a = c_ref[...]
    a = a * jax.nn.sigmoid(a)
    a_hi, a_lo = _split_bf16(a)
    w_hi, w_lo = _split_bf16(w_ref[0])
    o_ref[0] = _dot(a_hi, w_hi) + _dot(a_lo, w_hi) + _dot(a_hi, w_lo) + b_ref[0]


def _mod_call(cvec, w_mod, b_mod):
    depth, d, n = w_mod.shape
    rows = cvec.shape[0]
    tn = 1024
    return pl.pallas_call(
        _mod_kernel,
        grid=(depth, n // tn),
        in_specs=[
            pl.BlockSpec((rows, d), lambda l, j: (0, 0)),
            pl.BlockSpec((1, d, tn), lambda l, j: (l, 0, j)),
            pl.BlockSpec((1, 1, tn), lambda l, j: (l, 0, j)),
        ],
        out_specs=pl.BlockSpec((1, rows, tn), lambda l, j: (l, 0, j)),
        out_shape=jax.ShapeDtypeStruct((depth, rows, n), F32),
        compiler_params=_params(("arbitrary", "arbitrary")),
        name="mod",
    )(cvec, w_mod, b_mod.reshape(depth, 1, n))


def _ada_norm(x, g, shift, scale):
    y = x * lax.rsqrt(jnp.mean(x * x, axis=-1, keepdims=True) + EPS)
    return (y * g) * (1.0 + scale) + shift


def _deinterleave(y, scr_ref):
    n = y.shape[0]
    ev, od = [], []
    for cb in range(y.shape[1] // LANES):
        scr_ref[cb] = y[:, cb * LANES:(cb + 1) * LANES]
        ev.append(scr_ref[cb, pl.ds(0, n // 2, stride=2), :])
        od.append(scr_ref[cb, pl.ds(1, n // 2, stride=2), :])
    return jnp.concatenate(ev, axis=1), jnp.concatenate(od, axis=1)


def _interleave(ev, od, scr_ref):
    n2 = ev.shape[0]
    cols = []
    for cb in range(ev.shape[1] // LANES):
        scr_ref[cb, pl.ds(0, n2, stride=2), :] = ev[:, cb * LANES:(cb + 1) * LANES]
        scr_ref[cb, pl.ds(1, n2, stride=2), :] = od[:, cb * LANES:(cb + 1) * LANES]
        cols.append(scr_ref[cb])
    return jnp.concatenate(cols, axis=1)


def _proj_kernel(x_ref, sh_ref, sc_ref, g_ref, wa_ref, wb_ref, cs_ref, oa_ref, ob_ref, of_ref, scr_ref):
    m = cs_ref.shape[0]
    h = _ada_norm(x_ref[0], g_ref[...], sh_ref[0, 0], sc_ref[0, 0]).astype(BF16)
    oa_ref[0] = _dot(h, wa_ref[...])
    pb = _dot(h, wb_ref[...])
    ob_ref[0] = pb[:, m:].astype(BF16)
    ev, od = _deinterleave(_dot(pb[:, :m].astype(BF16), cs_ref[...]), scr_ref)
    of_ref[0, 0] = ev.astype(BF16)
    of_ref[0, 1] = od.astype(BF16)


def _proj_call(x, mod4, g, wa, wb, cs, tm):
    b, l, d = x.shape
    m = cs.shape[0]
    na, nb = wa.shape[1], wb.shape[1]
    const = lambda bi, i: (0, 0)
    return pl.pallas_call(
        _proj_kernel,
        grid=(b, l // tm),
        in_specs=[
            pl.BlockSpec((1, tm, d), lambda bi, i: (bi, i, 0)),
            pl.BlockSpec((1, 1, 1, d), lambda bi, i: (bi, 0, 0, 0)),
            pl.BlockSpec((1, 1, 1, d), lambda bi, i: (bi, 1, 0, 0)),
            pl.BlockSpec((1, d), const),
            pl.BlockSpec((d, na), const),
            pl.BlockSpec((d, nb), const),
            pl.BlockSpec((m, 2 * m), const),
        ],
        out_specs=[
            pl.BlockSpec((1, tm, na), lambda bi, i: (bi, i, 0)),
            pl.BlockSpec((1, tm, nb - m), lambda bi, i: (bi, i, 0)),
            pl.BlockSpec((1, 2, tm // 2, 2 * m), lambda bi, i: (bi, 0, i, 0)),
        ],
        out_shape=[
            jax.ShapeDtypeStruct((b, l, na), F32),
            jax.ShapeDtypeStruct((b, l, nb - m), BF16),
            jax.ShapeDtypeStruct((b, 2, l // 2, 2 * m), BF16),
        ],
        scratch_shapes=[pltpu.VMEM((2 * m // LANES, tm, LANES), F32)],
        compiler_params=_params(("parallel", "parallel")),
        name="proj",
    )(x, mod4, mod4, g, wa, wb, cs)


def _fill_padded(src_ref, pad_ref, l, rows):
    c = pad_ref.shape[1]
    pad_ref[0:PAD, :] = jnp.zeros((PAD, c), F32)
    pad_ref[l + PAD:l + 2 * PAD, :] = jnp.zeros((PAD, c), F32)

    def copy(i, carry):
        r = pl.multiple_of(i * rows, rows)
        pad_ref[pl.ds(r + PAD, rows), :] = src_ref[0, pl.ds(r, rows), :]
        return carry

    lax.fori_loop(0, l // rows, copy, 0)


def _pool_kernel(u_ref, w_ref, s_ref, o_ref, pad_ref, *, rows):
    l, c = u_ref.shape[1], u_ref.shape[2]
    gw = c // len(POOL_WINDOWS)
    _fill_padded(u_ref, pad_ref, l, rows)
    n = rows + 2 * PAD
    lane = lax.broadcasted_iota(jnp.int32, (n, c), 1)
    row = lax.broadcasted_iota(jnp.int32, (n, c), 0)
    grp = lax.shift_right_logical(lane, int(math.log2(gw)))
    half = jnp.where(grp == 0, 1, jnp.where(grp == 1, 2, jnp.where(grp == 2, 4, 8)))
    w = w_ref[...]
    scale = s_ref[...]

    def body(i, carry):
        r = pl.multiple_of(i * rows, rows)
        win = pad_ref[pl.ds(r, n), :]
        a2 = win + pltpu.roll(win, 1, 0)
        w4 = pltpu.roll(a2, 1, 0) + pltpu.roll(a2, n - 1, 0)
        w8 = pltpu.roll(w4, 2, 0) + pltpu.roll(w4, n - 2, 0)
        w16 = pltpu.roll(w8, 4, 0) + pltpu.roll(w8, n - 4, 0)
        s = jnp.where(grp == 0, a2, jnp.where(grp == 1, w4, jnp.where(grp == 2, w8, w16)))
        t = row + (r - PAD)
        cnt = (jnp.minimum(t + half, l) - jnp.maximum(t - half, 0)).astype(F32)
        y = (s / jnp.maximum(cnt, 1.0) - win)[PAD:PAD + rows]
        o_ref[0, pl.ds(r, rows), :] = (_dot(y.astype(BF16), w) * scale).astype(BF16)
        return carry

    lax.fori_loop(0, l // rows, body, 0)


def _pool_call(oa, w_blk, scale, rows):
    b, l, _ = oa.shape
    c = w_blk.shape[0]
    return pl.pallas_call(
        functools.partial(_pool_kernel, rows=rows),
        grid=(b,),
        in_specs=[
            pl.BlockSpec((1, l, c), lambda bi: (bi, 0, 0)),
            pl.BlockSpec((c, c), lambda bi: (0, 0)),
            pl.BlockSpec((1, c), lambda bi: (0, 0)),
        ],
        out_specs=pl.BlockSpec((1, l, c), lambda bi: (bi, 0, 0)),
        out_shape=jax.ShapeDtypeStruct((b, l, c), BF16),
        scratch_shapes=[pltpu.VMEM((l + 2 * PAD, c), F32)],
        compiler_params=_params(("parallel",)),
        name="pool",
    )(oa, w_blk, scale)


def _dwconv_kernel(u_ref, w_ref, b_ref, o_ref, ob_ref, pad_ref, scr_ref, *, rows):
    l = u_ref.shape[1]
    _fill_padded(u_ref, pad_ref, l, rows)
    n = rows + 2 * PAD
    w = w_ref[...]
    bias = b_ref[...]
    first_block = pl.program_id(1) == 0

    def body(i, carry):
        r = pl.multiple_of(i * rows, rows)
        r2 = pl.multiple_of(i * (rows // 2), rows // 2)
        win = pad_ref[pl.ds(r, n), :]
        y = pltpu.roll(win, 1, 0) * w[0:1] + win * w[1:2] + pltpu.roll(win, n - 1, 0) * w[2:3] + bias
        ev, od = _deinterleave(y[PAD:PAD + rows], scr_ref)
        o_ref[0, 0, pl.ds(r2, rows // 2), :] = ev
        o_ref[0, 1, pl.ds(r2, rows // 2), :] = od

        @pl.when(first_block)
        def _():
            ob_ref[0, 0, pl.ds(r2, rows // 2), :] = ev.astype(BF16)
            ob_ref[0, 1, pl.ds(r2, rows // 2), :] = od.astype(BF16)

        return carry

    lax.fori_loop(0, l // rows, body, 0)


def _dwconv_call(oa, w, bias, c, col0, rows):
    b, l, _ = oa.shape
    nblk = w.shape[1] // c
    return pl.pallas_call(
        functools.partial(_dwconv_kernel, rows=rows),
        grid=(b, nblk),
        in_specs=[
            pl.BlockSpec((1, l, c), lambda bi, j: (bi, 0, j + col0)),
            pl.BlockSpec((w.shape[0], c), lambda bi, j: (0, j)),
            pl.BlockSpec((1, c), lambda bi, j: (0, j)),
        ],
        out_specs=[
            pl.BlockSpec((1, 2, l // 2, c), lambda bi, j: (bi, 0, 0, j)),
            pl.BlockSpec((1, 2, l // 2, c), lambda bi, j: (bi, 0, 0, 0)),
        ],
        out_shape=[
            jax.ShapeDtypeStruct((b, 2, l // 2, nblk * c), F32),
            jax.ShapeDtypeStruct((b, 2, l // 2, c), BF16),
        ],
        scratch_shapes=[pltpu.VMEM((l + 2 * PAD, c), F32), pltpu.VMEM((c // LANES, rows, LANES), F32)],
        compiler_params=_params(("parallel", "arbitrary")),
        name="dwconv",
    )(oa, w, bias)


def _filt_kernel(f_ref, w1_ref, b1_ref, w2_ref, b2_ref, w3_ref, b3_ref, fr_ref, dl_ref,
                 hi_ref, lo_ref, row0_ref, asum_ref):
    hp = lax.Precision.HIGHEST
    feats = f_ref[...]
    freq = fr_ref[...]
    h = jnp.sin(freq * (jnp.dot(feats, w1_ref[...], precision=hp, preferred_element_type=F32) + b1_ref[...]))
    h = jnp.sin(freq * (jnp.dot(h, w2_ref[...], precision=hp, preferred_element_type=F32) + b2_ref[...]))
    h = jnp.dot(h, w3_ref[...], precision=hp, preferred_element_type=F32) + b3_ref[...]
    t = feats[:, 0:1]
    win = jnp.exp(-t * jnp.abs(dl_ref[...])) + HYENA_DECAY_SHIFT
    h = h * jnp.concatenate([win] * (h.shape[1] // win.shape[1]), axis=1)
    hi, lo = _split_bf16(h)
    hi_ref[...] = hi
    lo_ref[...] = lo
    part = jnp.sum(jnp.abs(h), axis=0, keepdims=True)

    @pl.when(pl.program_id(0) == 0)
    def _():
        asum_ref[...] = jnp.zeros_like(asum_ref)
        row0_ref[...] = h[0:8]

    asum_ref[...] += part


def _filt_call(feats, w1, b1, w2, b2, w3, b3, freq, deltas, rows):
    l, fd = feats.shape
    hd = w2.shape[0]
    n = w3.shape[1]
    c = deltas.shape[1]
    const = lambda i: (0, 0)
    return pl.pallas_call(
        _filt_kernel,
        grid=(l // rows,),
        in_specs=[
            pl.BlockSpec((rows, fd), lambda i: (i, 0)),
            pl.BlockSpec((fd, hd), const), pl.BlockSpec((1, hd), const),
            pl.BlockSpec((hd, hd), const), pl.BlockSpec((1, hd), const),
            pl.BlockSpec((hd, n), const), pl.BlockSpec((1, n), const),
            pl.BlockSpec((1, hd), const), pl.BlockSpec((1, c), const),
        ],
        out_specs=[
            pl.BlockSpec((rows, n), lambda i: (i, 0)),
            pl.BlockSpec((rows, n), lambda i: (i, 0)),
            pl.BlockSpec((8, n), const),
            pl.BlockSpec((1, n), const),
        ],
        out_shape=[
            jax.ShapeDtypeStruct((l, n), BF16),
            jax.ShapeDtypeStruct((l, n), BF16),
            jax.ShapeDtypeStruct((8, n), F32),
            jax.ShapeDtypeStruct((1, n), F32),
        ],
        compiler_params=_params(("arbitrary",)),
        name="hyena_filt",
    )(feats, w1, b1, w2, b2, w3, b3, freq, deltas)


def _alt_signs(n):
    t = lax.broadcasted_iota(jnp.int32, (16, n), 1)
    return jnp.where((t & 1) == 0, 1.0, -1.0).astype(BF16)


def _pair_transform(ce, co, se, so, ze_parts, zo_parts):
    ae = sum(_dot(ce, p) for p in ze_parts)
    ao = sum(_dot(co, p) for p in zo_parts)
    be = sum(_dot(se, p) for p in ze_parts)
    bo = sum(_dot(so, p) for p in zo_parts)
    return ae + ao, ae - ao, be + bo, bo - be


def _kspec_kernel(ce_ref, co_ref, se_ref, so_ref, hi_ref, lo_ref, row0_ref, asum_ref, kk_ref, km_ref):
    i = pl.program_id(1)
    tm, half = ce_ref.shape
    c = kk_ref.shape[3]
    l = 2 * half
    he = (hi_ref[0:half, :], lo_ref[0:half, :])
    ho = (hi_ref[half:l, :], lo_ref[half:l, :])
    hc_lo, hc_hi, hs_lo, hs_hi = _pair_transform(ce_ref[...], co_ref[...], se_ref[...], so_ref[...], he, ho)
    asum = asum_ref[...]
    inv = 1.0 / (asum[:, :c] + asum[:, c:] + EPS)
    hb0 = row0_ref[0:1, c:]
    row = lax.broadcasted_iota(jnp.int32, (tm, c), 0) + i * tm
    wj = jnp.where(row == 0, 0.5 / l, 1.0 / l) * inv
    kk_ref[0, 0] = (hc_lo[:, :c] + hc_lo[:, c:] - hb0) * wj
    kk_ref[0, 1] = (hs_lo[:, :c] - hs_lo[:, c:]) * wj
    kk_ref[0, 2] = (hc_hi[:, :c] + hc_hi[:, c:] - hb0) * wj
    kk_ref[0, 3] = (hs_hi[:, :c] - hs_hi[:, c:]) * wj

    @pl.when(i == 0)
    def _():
        sg = _alt_signs(half)
        mc = sum(_dot(sg, p) for p in he)[0:8]
        ms = sum(_dot(sg, p) for p in ho)[0:8]
        km_ref[0, :, 0:c] = (mc[:, :c] + mc[:, c:] - hb0) * inv * (1.0 / l)
        km_ref[0, :, c:2 * c] = (ms[:, :c] - ms[:, c:]) * inv * (1.0 / l)


def _mat_specs(tm, half, n):
    return [pl.BlockSpec((tm, half), lambda g, i: (i, 0)) for _ in range(n)]


def _kspec_call(mats, h_hi, h_lo, row0, asum, c, tm):
    half = mats[0].shape[0]
    l = 2 * half
    orders = h_hi.shape[1] // (2 * c)
    return pl.pallas_call(
        _kspec_kernel,
        grid=(orders, half // tm),
        in_specs=_mat_specs(tm, half, 4) + [
            pl.BlockSpec((l, 2 * c), lambda o, i: (0, o)),
            pl.BlockSpec((l, 2 * c), lambda o, i: (0, o)),
            pl.BlockSpec((8, 2 * c), lambda o, i: (0, o)),
            pl.BlockSpec((1, 2 * c), lambda o, i: (0, o)),
        ],
        out_specs=[
            pl.BlockSpec((1, 4, tm, c), lambda o, i: (o, 0, i, 0)),
            pl.BlockSpec((1, 8, 2 * c), lambda o, i: (o, 0, 0)),
        ],
        out_shape=[
            jax.ShapeDtypeStruct((orders, 4, half, c), F32),
            jax.ShapeDtypeStruct((orders, 8, 2 * c), F32),
        ],
        compiler_params=_params(("arbitrary", "arbitrary")),
        name="hyena_kspec",
    )(*mats, h_hi, h_lo, row0, asum)


def _hfwd_kernel(ce_ref, co_ref, se_ref, so_ref, z_ref, kk_ref, km_ref, spec_ref, ym_ref):
    nb = z_ref.shape[0]
    c = z_ref.shape[3]
    ce, co, se, so = ce_ref[...], co_ref[...], se_ref[...], so_ref[...]
    kr_lo, ks_lo, kr_hi, ks_hi = kk_ref[0, 0], kk_ref[0, 1], kk_ref[0, 2], kk_ref[0, 3]
    for bb in range(nb):
        zr_lo, zr_hi, zs_lo, zs_hi = _pair_transform(ce, co, se, so, (z_ref[bb, 0],), (z_ref[bb, 1],))
        yr_lo = zr_lo * kr_lo - zs_lo * ks_lo
        ys_lo = zr_lo * ks_lo + zs_lo * kr_lo
        yr_hi = zr_hi * kr_hi - zs_hi * ks_hi
        ys_hi = zr_hi * ks_hi + zs_hi * kr_hi
        spec_ref[bb, 0] = (yr_lo + yr_hi).astype(BF16)
        spec_ref[bb, 1] = (ys_lo - ys_hi).astype(BF16)
        spec_ref[bb, 2] = (yr_lo - yr_hi).astype(BF16)
        spec_ref[bb, 3] = (ys_lo + ys_hi).astype(BF16)

    @pl.when(pl.program_id(1) == 0)
    def _():
        sg = _alt_signs(z_ref.shape[2])
        km_r, km_s = km_ref[0, :, 0:c], km_ref[0, :, c:2 * c]
        for bb in range(nb):
            zr = _dot(sg, z_ref[bb, 0])[0:8]
            zs = _dot(sg, z_ref[bb, 1])[0:8]
            ym_ref[bb, :, 0:c] = zr * km_r - zs * km_s
            ym_ref[bb, :, c:2 * c] = zr * km_s + zs * km_r


def _hfwd_call(mats, zb, kk, km, order, c, nb, tm):
    b, _, half, _ = zb.shape
    return pl.pallas_call(
        _hfwd_kernel,
        grid=(b // nb, half // tm),
        in_specs=_mat_specs(tm, half, 4) + [
            pl.BlockSpec((nb, 2, half, c), lambda g, i: (g, 0, 0, 0)),
            pl.BlockSpec((1, 4, tm, c), lambda g, i: (order, 0, i, 0)),
            pl.BlockSpec((1, 8, 2 * c), lambda g, i: (order, 0, 0)),
        ],
        out_specs=[
            pl.BlockSpec((nb, 4, tm, c), lambda g, i: (g, 0, i, 0)),
            pl.BlockSpec((nb, 8, 2 * c), lambda g, i: (g, 0, 0)),
        ],
        out_shape=[
            jax.ShapeDtypeStruct((b, 4, half, c), BF16),
            jax.ShapeDtypeStruct((b, 8, 2 * c), F32),
        ],
        compiler_params=_params(("arbitrary", "arbitrary")),
        name="hyena_fwd",
    )(*mats, zb, kk, km)


def _hinv_kernel(ce_ref, se_ref, cot_ref, sot_ref, spec_ref, ym_ref, z_ref, gate_ref, skip_ref, *refs,
                 natural_out):
    nb, _, tm, c = z_ref.shape
    ce, se, cot, sot = ce_ref[...], se_ref[...], cot_ref[...], sot_ref[...]
    row = lax.broadcasted_iota(jnp.int32, (tm, c), 0)
    even_row = (row & 1) == 0
    skip = skip_ref[0]
    for bb in range(nb):
        ym_r, ym_s = ym_ref[bb, 0:1, 0:c], ym_ref[bb, 0:1, c:2 * c]
        ye = _dot(ce, spec_ref[bb, 0]) + _dot(se, spec_ref[bb, 1]) + jnp.where(even_row, ym_r, -ym_r)
        yo = _dot(cot, spec_ref[bb, 2]) + _dot(sot, spec_ref[bb, 3]) + jnp.where(even_row, ym_s, -ym_s)
        oe = gate_ref[bb, 0] * (ye + skip * z_ref[bb, 0])
        oo = gate_ref[bb, 1] * (yo + skip * z_ref[bb, 1])
        if natural_out:
            o_ref, scr_ref = refs
            o_ref[bb] = _interleave(oe, oo, scr_ref)
        else:
            for o_ref in refs:
                o_ref[bb, 0] = oe.astype(o_ref.dtype)
                o_ref[bb, 1] = oo.astype(o_ref.dtype)


def _hinv_call(mats, spec, ym, zf, zcol, uc, gcol, skip, order, natural_out, c, nb, tm):
    b, _, half, _ = spec.shape
    if natural_out:
        out_specs = [pl.BlockSpec((nb, 2 * tm, c), lambda g, i: (g, i, 0))]
        out_shape = [jax.ShapeDtypeStruct((b, 2 * half, c), F32)]
        scratch = [pltpu.VMEM((c // LANES, 2 * tm, LANES), F32)]
    else:
        out_specs = [pl.BlockSpec((nb, 2, tm, c), lambda g, i: (g, 0, i, 0)) for _ in range(2)]
        out_shape = [jax.ShapeDtypeStruct((b, 2, half, c), dt) for dt in (F32, BF16)]
        scratch = []
    return pl.pallas_call(
        functools.partial(_hinv_kernel, natural_out=natural_out),
        grid=(b // nb, half // tm),
        in_specs=_mat_specs(tm, half, 4) + [
            pl.BlockSpec((nb, 4, half, c), lambda g, i: (g, 0, 0, 0)),
            pl.BlockSpec((nb, 8, 2 * c), lambda g, i: (g, 0, 0)),
            pl.BlockSpec((nb, 2, tm, c), lambda g, i: (g, 0, i, zcol)),
            pl.BlockSpec((nb, 2, tm, c), lambda g, i: (g, 0, i, gcol)),
            pl.BlockSpec((1, 1, c), lambda g, i: (order, 0, 0)),
        ],
        out_specs=out_specs,
        out_shape=out_shape,
        scratch_shapes=scratch,
        compiler_params=_params(("arbitrary", "arbitrary")),
        name="hyena_inv",
    )(*mats, spec, ym, zf, uc, skip)


def _fseq_kernel(ce_ref, co_ref, se_ref, so_ref, u_ref, o_ref):
    nb = u_ref.shape[0]
    c = o_ref.shape[3]
    ce, co, se, so = ce_ref[...], co_ref[...], se_ref[...], so_ref[...]
    for bb in range(nb):
        ae = _dot(ce, u_ref[bb, 0, :, 0:c]) + _dot(se, u_ref[bb, 0, :, c:2 * c])
        ao = _dot(co, u_ref[bb, 1, :, 0:c]) + _dot(so, u_ref[bb, 1, :, c:2 * c])
        o_ref[bb, 0] = (ae + ao).astype(BF16)
        o_ref[bb, 1] = (ae - ao).astype(BF16)


def _fseq_call(mats, ucs, nb, tm):
    b, _, half, c2 = ucs.shape
    c = c2 // 2
    out = pl.pallas_call(
        _fseq_kernel,
        grid=(b // nb, half // tm),
        in_specs=_mat_specs(tm, half, 4) + [pl.BlockSpec((nb, 2, half, c2), lambda g, i: (g, 0, 0, 0))],
        out_specs=pl.BlockSpec((nb, 2, tm, c), lambda g, i: (g, 0, i, 0)),
        out_shape=jax.ShapeDtypeStruct((b, 2, half, c), BF16),
        compiler_params=_params(("arbitrary", "arbitrary")),
        name="fnet_seq",
    )(*mats, ucs)
    return out.reshape(b, 2 * half, c)


def _stack_heads(q, heads):
    lane = lax.broadcasted_iota(jnp.int32, q.shape, 1)
    zero = jnp.zeros_like(q)
    return jnp.concatenate([jnp.where(lax.shift_right_logical(lane, int(math.log2(NA_HEAD_DIM))) == h, q, zero) for h in range(heads)], axis=0)


def _merge_heads(o, heads, n):
    lane = lax.broadcasted_iota(jnp.int32, (n, o.shape[1]), 1)
    out = jnp.zeros((n, o.shape[1]), F32)
    for h in range(heads):
        out = out + jnp.where(lax.shift_right_logical(lane, int(math.log2(NA_HEAD_DIM))) == h, o[h * n:(h + 1) * n], 0.0)
    return out


def _natt_kernel(q_ref, k_ref, v_ref, kc_ref, vc_ref, bias_ref, o_ref, *, rows, kh, group):
    w = GRID_W
    heads = q_ref.shape[2] // NA_HEAD_DIM
    base = pl.program_id(1) * group
    lo = kh // 2
    kc, vc = kc_ref[0], vc_ref[0]
    scale = jnp.asarray(NA_HEAD_DIM ** -0.5, BF16)

    def body(j, carry):
        r = base + j
        r0 = jnp.clip(r - lo, 0, rows - kh)
        start = pl.multiple_of(r0 * w, w)
        cls = jnp.minimum(r, lo) + jnp.maximum(r - (rows - kh + lo), 0)
        q4 = _stack_heads(q_ref[0, pl.ds(pl.multiple_of(j * w, w), w), :] * scale, heads)
        ks = k_ref[0, pl.ds(start, kh * w), :]
        vs = v_ref[0, pl.ds(start, kh * w), :]
        s_nb = _dot_t(q4, ks) + bias_ref[cls]
        s_cx = _dot_t(q4, kc)
        m = jnp.maximum(jnp.max(s_nb, axis=-1, keepdims=True), jnp.max(s_cx, axis=-1, keepdims=True))
        p_nb = jnp.exp(s_nb - m)
        p_cx = jnp.exp(s_cx - m)
        den = jnp.sum(p_nb, axis=-1, keepdims=True) + jnp.sum(p_cx, axis=-1, keepdims=True)
        o4 = (_dot(p_nb.astype(BF16), vs) + _dot(p_cx.astype(BF16), vc)) / den
        o_ref[0, pl.ds(pl.multiple_of(j * w, w), w), :] = _merge_heads(o4, heads, w).astype(BF16)
        return carry

    lax.fori_loop(0, group, body, 0, unroll=4)


def _natt_call(ob, obc, bias, c, group):
    b, l, _ = ob.shape
    lc = obc.shape[1]
    w = GRID_W
    rows = l // w
    kh = min(NA_WIN_H, rows)
    return pl.pallas_call(
        functools.partial(_natt_kernel, rows=rows, kh=kh, group=group),
        grid=(b, rows // group),
        in_specs=[
            pl.BlockSpec((1, group * w, c), lambda bi, r: (bi, r, 0)),
            pl.BlockSpec((1, l, c), lambda bi, r: (bi, 0, 1)),
            pl.BlockSpec((1, l, c), lambda bi, r: (bi, 0, 2)),
            pl.BlockSpec((1, lc, c), lambda bi, r: (bi, 0, 1)),
            pl.BlockSpec((1, lc, c), lambda bi, r: (bi, 0, 2)),
            pl.BlockSpec(bias.shape, lambda bi, r: (0, 0, 0)),
        ],
        out_specs=pl.BlockSpec((1, group * w, c), lambda bi, r: (bi, r, 0)),
        out_shape=jax.ShapeDtypeStruct((b, l, c), BF16),
        compiler_params=_params(("parallel", "arbitrary")),
        name="nb_attn",
    )(ob, ob, ob, obc, obc, bias)


def _bias_kernel(r_ref, o_ref):
    w = GRID_W
    shift = int(math.log2(w))
    n = o_ref.shape[1]
    r = r_ref[...]
    hi = r.astype(BF16)
    mid = (r - hi.astype(F32)).astype(BF16)
    lo = (r - hi.astype(F32) - mid.astype(F32)).astype(BF16)
    dc = lax.broadcasted_iota(jnp.int32, (r.shape[1], n), 0)
    p = lax.broadcasted_iota(jnp.int32, (r.shape[1], n), 1)
    idx = jnp.clip((p & (w - 1)) - lax.shift_right_logical(p, shift), 1 - NA_WIN_W, NA_WIN_W - 1) + (NA_WIN_W - 1)
    onehot = jnp.where(dc == idx, 1.0, 0.0).astype(BF16)
    val = _dot(hi, onehot) + _dot(mid, onehot) + _dot(lo, onehot)
    po = lax.broadcasted_iota(jnp.int32, val.shape, 1)
    cq = lax.shift_right_logical(po, shift)
    ck = po & (w - 1)
    c0 = jnp.clip(cq - NA_WIN_W // 2, 0, w - NA_WIN_W)
    inside = jnp.logical_and(ck >= c0, ck < c0 + NA_WIN_W)
    o_ref[...] = jnp.where(inside, val, NEG_INF)


def _bias_call(rpb2d):
    rows, cols = rpb2d.shape
    n = GRID_W * GRID_W
    return pl.pallas_call(
        _bias_kernel,
        grid=(1,),
        in_specs=[pl.BlockSpec((rows, cols), lambda i: (0, 0))],
        out_specs=pl.BlockSpec((rows, n), lambda i: (0, 0)),
        out_shape=jax.ShapeDtypeStruct((rows, n), F32),
        compiler_params=_params(("arbitrary",)),
        name="rpb_table",
    )(rpb2d)


def _catt_kernel(q_ref, k_ref, v_ref, o_ref):
    n = q_ref.shape[1]
    heads = q_ref.shape[2] // NA_HEAD_DIM
    q = q_ref[0] * jnp.asarray(NA_HEAD_DIM ** -0.5, BF16)
    q4 = _stack_heads(q, heads)
    s = _dot_t(q4, k_ref[0])
    p = jnp.exp(s - jnp.max(s, axis=-1, keepdims=True))
    den = jnp.sum(p, axis=-1, keepdims=True)
    o4 = _dot(p.astype(BF16), v_ref[0]) / den
    o_ref[0] = _merge_heads(o4, heads, n).astype(BF16)


def _catt_call(obc, c):
    b, lc, _ = obc.shape
    return pl.pallas_call(
        _catt_kernel,
        grid=(b,),
        in_specs=[pl.BlockSpec((1, lc, c), lambda bi, j=j: (bi, 0, j)) for j in range(3)],
        out_specs=pl.BlockSpec((1, lc, c), lambda bi: (bi, 0, 0)),
        out_shape=jax.ShapeDtypeStruct((b, lc, c), BF16),
        compiler_params=_params(("parallel",)),
        name="ctx_attn",
    )(obc, obc, obc)


def _merge_kernel(x_ref, sh_ref, sc_ref, gt_ref, g_ref, y0_ref, y1_ref, y2_ref, y3_ref,
                  wg_ref, wb_ref, wo_ref, o_ref):
    d = x_ref.shape[2]
    x = x_ref[0]
    h = _ada_norm(x, g_ref[...], sh_ref[0, 0], sc_ref[0, 0]).astype(BF16)
    merged = None
    for bi, y_ref in enumerate((y0_ref, y1_ref, y2_ref, y3_ref)):
        gate = jax.nn.sigmoid(_dot(h, wg_ref[:, bi * d:(bi + 1) * d]))
        term = gate * _dot(y_ref[0].astype(BF16), wb_ref[bi])
        merged = term if merged is None else merged + term
    o_ref[0] = x + gt_ref[0, 0] * _dot(merged.astype(BF16), wo_ref[...])


def _merge_call(x, mod4, g, ys, wg, wb, wo, tm):
    b, l, d = x.shape
    c = ys[0].shape[2]
    const2 = lambda bi, i: (0, 0)
    mod_spec = lambda k: pl.BlockSpec((1, 1, 1, d), lambda bi, i: (bi, k, 0, 0))
    return pl.pallas_call(
        _merge_kernel,
        grid=(b, l // tm),
        in_specs=[
            pl.BlockSpec((1, tm, d), lambda bi, i: (bi, i, 0)),
            mod_spec(0), mod_spec(1), mod_spec(2),
            pl.BlockSpec((1, d), const2),
        ] + [pl.BlockSpec((1, tm, c), lambda bi, i: (bi, i, 0)) for _ in ys] + [
            pl.BlockSpec(wg.shape, const2, pipeline_mode=pl.Buffered(1)),
            pl.BlockSpec(wb.shape, lambda bi, i: (0, 0, 0), pipeline_mode=pl.Buffered(1)),
            pl.BlockSpec(wo.shape, const2, pipeline_mode=pl.Buffered(1)),
        ],
        out_specs=pl.BlockSpec((1, tm, d), lambda bi, i: (bi, i, 0)),
        out_shape=jax.ShapeDtypeStruct((b, l, d), F32),
        compiler_params=_params(("parallel", "parallel")),
        name="merge",
    )(x, mod4, mod4, mod4, g, *ys, wg, wb, wo)


def _ffn_kernel(x_ref, xp_ref, xn_ref, sh_ref, sc_ref, gt_ref, g_ref, wu_ref, wgt_ref, cw_ref, cb_ref,
                wd_ref, fg_ref, o_ref, a_ref, *, chunk, final_norm):
    i = pl.program_id(1)
    last = pl.num_programs(1) - 1
    tm = x_ref.shape[1]
    f = wu_ref.shape[1]
    n = tm + 2 * HALO
    g, sh, sc = g_ref[...], sh_ref[0, 0], sc_ref[0, 0]
    x = x_ref[0]
    h = _ada_norm(x, g, sh, sc).astype(BF16)
    hp = _ada_norm(xp_ref[0], g, sh, sc).astype(BF16)
    hn = _ada_norm(xn_ref[0], g, sh, sc).astype(BF16)
    hext = jnp.concatenate([hp, h, hn], axis=0)
    row = lax.broadcasted_iota(jnp.int32, (n, chunk), 0)
    valid = jnp.logical_and(jnp.logical_or(row >= HALO, i > 0), jnp.logical_or(row < HALO + tm, i < last))
    for j in range(f // chunk):
        cols = slice(j * chunk, (j + 1) * chunk)
        gp = jnp.where(valid, _dot(hext, wgt_ref[:, cols]), 0.0)
        cw = cw_ref[:, cols]
        gc = (pltpu.roll(gp, 1, 0) * cw[0:1] + gp * cw[1:2] + pltpu.roll(gp, n - 1, 0) * cw[2:3]
              + cb_ref[:, cols])[HALO:HALO + tm]
        u = _dot(h, wu_ref[:, cols])
        a_ref[:, cols] = (gc * jax.nn.sigmoid(gc) * u).astype(BF16)
    y = x + gt_ref[0, 0] * _dot(a_ref[...], wd_ref[...])
    if final_norm:
        y = y * lax.rsqrt(jnp.mean(y * y, axis=-1, keepdims=True) + EPS) * fg_ref[...]
    o_ref[0] = y


def _ffn_call(x, mod4, g, wu, wgt, cw, cb, wd, fg, tm, chunk, final_norm):
    b, l, d = x.shape
    f = wu.shape[1]
    hb = tm // HALO
    nblk = l // HALO
    const2 = lambda bi, i: (0, 0)
    mod_spec = lambda k: pl.BlockSpec((1, 1, 1, d), lambda bi, i: (bi, k, 0, 0))
    single = dict(pipeline_mode=pl.Buffered(1))
    return pl.pallas_call(
        functools.partial(_ffn_kernel, chunk=chunk, final_norm=final_norm),
        grid=(b, l // tm),
        in_specs=[
            pl.BlockSpec((1, tm, d), lambda bi, i: (bi, i, 0)),
            pl.BlockSpec((1, HALO, d), lambda bi, i: (bi, jnp.maximum(i * hb - 1, 0), 0)),
            pl.BlockSpec((1, HALO, d), lambda bi, i: (bi, jnp.minimum((i + 1) * hb, nblk - 1), 0)),
            mod_spec(3), mod_spec(4), mod_spec(5),
            pl.BlockSpec((1, d), const2),
            pl.BlockSpec((d, f), const2, **single),
            pl.BlockSpec((d, f), const2, **single),
            pl.BlockSpec((cw.shape[0], f), const2),
            pl.BlockSpec((1, f), const2),
            pl.BlockSpec((f, d), const2, **single),
            pl.BlockSpec((1, d), const2),
        ],
        out_specs=pl.BlockSpec((1, tm, d), lambda bi, i: (bi, i, 0)),
        out_shape=jax.ShapeDtypeStruct((b, l, d), F32),
        scratch_shapes=[pltpu.VMEM((tm, f), BF16)],
        compiler_params=_params(("parallel", "arbitrary")),
        name="ffn",
    )(x, x, x, mod4, mod4, mod4, g, wu, wgt, cw, cb, wd, fg)


def _trig_tables(n, row_mul, row_add, col_mul, col_add, period):
    a_sz = 64 if n % 64 == 0 else 1
    q = (jnp.arange(n, dtype=jnp.int32) * col_mul + col_add)[None, :]
    ra = (jnp.arange(n // a_sz, dtype=jnp.int32) * (a_sz * row_mul))[:, None]
    rb = (jnp.arange(a_sz, dtype=jnp.int32) * row_mul + row_add)[:, None]
    w = 2.0 * math.pi / period
    ang_a = ((ra * q) % period).astype(F32) * w
    ang_b = ((rb * q) % period).astype(F32) * w
    ca, sa = jnp.cos(ang_a)[:, None, :], jnp.sin(ang_a)[:, None, :]
    cb, sb = jnp.cos(ang_b)[None, :, :], jnp.sin(ang_b)[None, :, :]
    cos = (ca * cb - sa * sb).reshape(n, n)
    sin = (sa * cb + ca * sb).reshape(n, n)
    return cos, sin


def _pad2(a, rows, cols):
    return jnp.pad(a, ((0, rows - a.shape[0]), (0, cols - a.shape[1])))


def _hyena_features(l, pad_to):
    t = jnp.linspace(0.0, 1.0, l, dtype=F32)[:, None]
    bands = jnp.linspace(1e-4, HYENA_BANDS - 1, HYENA_BANDS, dtype=F32)[None, :]
    ang = (2.0 * math.pi / l) * jnp.arange(l, dtype=F32)[:, None] * bands
    feats = jnp.concatenate([t, jnp.cos(ang), -jnp.sin(ang)], axis=-1)
    return jnp.pad(feats, ((0, 0), (0, pad_to - feats.shape[1])))


def _bias_classes(rows, kh):
    r = np.arange(rows)
    r0 = np.clip(r - kh // 2, 0, rows - kh)
    off = r0 - r
    lo = kh // 2
    reps = list(range(lo)) + [lo] + list(range(rows - kh + lo + 1, rows))
    return [int(off[i]) for i in reps]


def _attention_bias(rpb, rows, kh):
    heads, nr, nc = rpb.shape
    w = GRID_W
    flat = rpb.astype(F32).reshape(heads * nr, nc)
    table = _bias_call(_pad2(flat, -(-heads * nr // 16) * 16, 128))
    table = table[:heads * nr].reshape(heads, nr, w, w)
    out = []
    for off in _bias_classes(rows, kh):
        first = off + NA_WIN_H - 1
        sl = table[:, first:first + kh]
        out.append(jnp.transpose(sl, (0, 2, 1, 3)).reshape(heads * w, kh * w))
    return jnp.stack(out)


def _mixer_inputs(x, mod4, lw, tm):
    return _proj_call(x, mod4, lw['norm1_g'], lw['wa'], lw['wb'], lw['cs'], tm)


def _hyena(oa, lw, tabs, c, nb, tm, rows):
    fwd, inv = tabs['hy_fwd'], tabs['hy_inv']
    uc, ucb = _dwconv_call(oa, lw['hyena_conv_w'], lw['hyena_conv_b'], c, 1, rows)
    h_hi, h_lo, row0, asum = _filt_call(tabs['feats'], lw['fw1'], lw['fb1'], lw['fw2'], lw['fb2'], lw['fw3'],
                                        lw['fb3'], lw['freq'], tabs['deltas'], min(rows * 2, oa.shape[1]))
    kk, km = _kspec_call(fwd, h_hi, h_lo, row0, asum, c, tm)
    spec, ym = _hfwd_call(fwd, ucb, kk, km, 0, c, nb, tm)
    z2, z2b = _hinv_call(inv, spec, ym, uc, 0, uc, 1, lw['skip'], 0, False, c, nb, tm)
    spec, ym = _hfwd_call(fwd, z2b, kk, km, 1, c, nb, tm)
    (y,) = _hinv_call(inv, spec, ym, z2, 0, uc, 2, lw['skip'], 1, True, c, nb, tm)
    return y


def _stream_layer(x, mod4, lw, tabs, proj, y_att, cfg, final_norm):
    c = cfg['c']
    oa, ob, of = proj
    y_pool = _pool_call(oa, lw['pool_blk'], lw['pool_scale'], cfg['rows'])
    y_fnet = _fseq_call(tabs['fn'], of, cfg['nb'], cfg['tseq'])
    y_hyena = _hyena(oa, lw, tabs, c, cfg['nb'], cfg['tseq'], cfg['rows'])
    x = _merge_call(x, mod4, lw['norm1_g'], (y_pool, y_fnet, y_hyena, y_att), lw['wg'], lw['w_branch'],
                    lw['w_out'], cfg['tm'])
    return _ffn_call(x, mod4, lw['norm2_g'], lw['wu'], lw['wgt'], lw['ffn_conv_w'], lw['ffn_conv_b'],
                     lw['wd'], lw['final_g'], cfg['tm'], cfg['chunk'], final_norm)


def _seq_tables(l, c):
    half = l // 2
    b16 = lambda t: tuple(a.astype(BF16) for a in t)
    ce, se = _trig_tables(half, 1, 0, 2, 0, 2 * l)
    co, so = _trig_tables(half, 1, 0, 2, 1, 2 * l)
    cot, sot = _trig_tables(half, 2, 1, 1, 0, 2 * l)
    fce, fse = _trig_tables(half, 1, 0, 2, 0, l)
    fco, fso = _trig_tables(half, 1, 0, 2, 1, l)
    scale = 1.0 / math.sqrt(l * (c // FNET_GROUPS))
    deltas = jnp.linspace(math.log(HYENA_DECAY_TARGET) / HYENA_SLOW_DECAY,
                          math.log(HYENA_DECAY_TARGET) / HYENA_FAST_DECAY, c, dtype=F32)[None, :]
    feats = _hyena_features(l, 128)
    return {
        'hy_fwd': b16((ce, co, se, so)), 'hy_inv': b16((ce, se, cot, sot)),
        'fn': b16((fce * scale, fco * scale, -fse * scale, -fso * scale)),
        'feats': jnp.concatenate([feats[0::2], feats[1::2]], axis=0), 'deltas': deltas,
    }


def kernel(x, c, ctx, c_ctx, w_mod, b_mod, norm1_g, norm2_g, w_in, pool_w, pool_scale, hyena_conv_w,
           hyena_conv_b, hyena_filt_w1, hyena_filt_b1, hyena_filt_w2, hyena_filt_b2, hyena_filt_w3,
           hyena_filt_b3, hyena_freq, hyena_skip, na_rpb, w_branch, w_out, ffn_w_up, ffn_conv_w, ffn_conv_b,
           ffn_w_down, final_norm_g):
    batch, seq, d = x.shape
    lc = ctx.shape[1]
    depth = w_mod.shape[0]
    m = d // N_BRANCH
    f = ffn_w_down.shape[1]
    rows = seq // GRID_W
    kh = min(NA_WIN_H, rows)

    cvec = jnp.concatenate([c, c_ctx[None], jnp.zeros((16 - batch - 1, d), F32)], axis=0)
    mod = _mod_call(cvec, w_mod, b_mod)
    mod_x = mod[:, :batch].reshape(depth, batch, 6, 1, d)
    mod_c = jnp.broadcast_to(mod[:, batch:batch + 1], (depth, batch, 6 * d)).reshape(depth, batch, 6, 1, d)

    gsz = m // FNET_GROUPS
    cc, ss = _trig_tables(gsz, 1, 0, 1, 0, gsz)
    eye = jnp.eye(FNET_GROUPS, dtype=F32)
    cs = jnp.concatenate([jnp.kron(eye, cc), jnp.kron(eye, ss)], axis=1).astype(BF16)

    tabs_x = _seq_tables(seq, m)
    tabs_c = _seq_tables(lc, m)
    cfg_x = dict(c=m, tm=512, rows=256, nb=2, tseq=min(512, seq // 2), chunk=256, group=8)
    cfg_c = dict(c=m, tm=lc, rows=lc, nb=2, tseq=lc // 2, chunk=256)

    pool_off, fnet_off, hy_off, na_off = 0, m, 2 * m, 5 * m
    gate_off = 8 * m

    for l in range(depth):
        last = l == depth - 1
        wi = w_in[l]
        lw = {
            'norm1_g': norm1_g[l][None], 'norm2_g': norm2_g[l][None], 'final_g': final_norm_g[None],
            'wa': jnp.concatenate([wi[:, pool_off:pool_off + m], wi[:, hy_off:hy_off + 3 * m]], axis=1).astype(BF16),
            'wb': jnp.concatenate([wi[:, fnet_off:fnet_off + m], wi[:, na_off:na_off + 3 * m]], axis=1).astype(BF16),
            'cs': cs,
            'wg': wi[:, gate_off:].astype(BF16),
            'pool_blk': jax.scipy.linalg.block_diag(*[pool_w[l, gi] for gi in range(pool_w.shape[1])]).astype(BF16),
            'pool_scale': pool_scale[l][None],
            'hyena_conv_w': hyena_conv_w[l], 'hyena_conv_b': hyena_conv_b[l][None],
            'fw1': _pad2(hyena_filt_w1[l], 128, 128), 'fb1': _pad2(hyena_filt_b1[l][None], 1, 128),
            'fw2': _pad2(hyena_filt_w2[l], 128, 128), 'fb2': _pad2(hyena_filt_b2[l][None], 1, 128),
            'fw3': _pad2(hyena_filt_w3[l], 128, hyena_filt_w3.shape[2]), 'fb3': hyena_filt_b3[l][None],
            'freq': _pad2(hyena_freq[l][None], 1, 128),
            'skip': hyena_skip[l][:, None, :],
            'w_branch': w_branch[l].astype(BF16), 'w_out': w_out[l].astype(BF16),
            'wu': ffn_w_up[l][:, :f].astype(BF16), 'wgt': ffn_w_up[l][:, f:].astype(BF16),
            'ffn_conv_w': ffn_conv_w[l], 'ffn_conv_b': ffn_conv_b[l][None],
            'wd': ffn_w_down[l].astype(BF16),
        }
        bias = _attention_bias(na_rpb[l], rows, kh)

        proj_c = _mixer_inputs(ctx, mod_c[l], lw, cfg_c['tm'])
        proj_x = _mixer_inputs(x, mod_x[l], lw, cfg_x['tm'])
        y_att = _natt_call(proj_x[1], proj_c[1], bias, m, cfg_x['group'])
        x = _stream_layer(x, mod_x[l], lw, tabs_x, proj_x, y_att, cfg_x, last)
        if not last:
            ctx = _stream_layer(ctx, mod_c[l], lw, tabs_c, proj_c, _catt_call(proj_c[1], m), cfg_c, False)
    return x
```

```python
import functools
import math

import numpy as np
import jax
import jax.numpy as jnp
from jax import lax
from jax.experimental import pallas as pl
from jax.experimental.pallas import tpu as pltpu

F32 = jnp.float32
BF16 = jnp.bfloat16

GRID_W = 64
N_BRANCH = 4
POOL_WINDOWS = (2, 4, 8, 16)
FNET_GROUPS = 4
HYENA_ORDER = 2
HYENA_BANDS = 16
HYENA_DECAY_TARGET = 1e-2
HYENA_FAST_DECAY = 0.3
HYENA_SLOW_DECAY = 1.5
HYENA_DECAY_SHIFT = 0.05
NA_HEAD_DIM = 64
NA_WIN_H = 8
NA_WIN_W = 16
EPS = 1e-6
NEG_INF = -1e30

LANES = 128
HALO = 16
PAD = 8
VMEM_LIMIT = 56 * 1024 * 1024


def _params(sem):
    return pltpu.CompilerParams(dimension_semantics=sem, vmem_limit_bytes=VMEM_LIMIT)


def _dot(a, b):
    return jnp.dot(a, b, preferred_element_type=F32)


def _dot_t(a, b):
    return lax.dot_general(a, b, (((1,), (1,)), ((), ())), preferred_element_type=F32)


def _split_bf16(a):
    hi = a.astype(BF16)
    lo = (a - hi.astype(F32)).astype(BF16)
    return hi, lo


def _mod_kernel(c_ref, w_ref, b_ref, o_ref):
    a = c_ref[...]
    a = a * jax.nn.sigmoid(a)
    a_hi, a_lo = _split_bf16(a)
    w_hi, w_lo = _split_bf16(w_ref[0])
    o_ref[0] = _dot(a_hi, w_hi) + _dot(a_lo, w_hi) + _dot(a_hi, w_lo) + b_ref[0]


def _mod_call(cvec, w_mod, b_mod):
    depth, d, n = w_mod.shape
    rows = cvec.shape[0]
    tn = 1024
    return pl.pallas_call(
        _mod_kernel,
        grid=(depth, n // tn),
        in_specs=[
            pl.BlockSpec((rows, d), lambda l, j: (0, 0)),
            pl.BlockSpec((1, d, tn), lambda l, j: (l, 0, j)),
            pl.BlockSpec((1, 1, tn), lambda l, j: (l, 0, j)),
        ],
        out_specs=pl.BlockSpec((1, rows, tn), lambda l, j: (l, 0, j)),
        out_shape=jax.ShapeDtypeStruct((depth, rows, n), F32),
        compiler_params=_params(("arbitrary", "arbitrary")),
        name="mod",
    )(cvec, w_mod, b_mod.reshape(depth, 1, n))


def _ada_norm(x, g, shift, scale):
    y = x * lax.rsqrt(jnp.mean(x * x, axis=-1, keepdims=True) + EPS)
    return (y * g) * (1.0 + scale) + shift


def _deinterleave(y, scr_ref):
    n = y.shape[0]
    ev, od = [], []
    for cb in range(y.shape[1] // LANES):
        scr_ref[cb] = y[:, cb * LANES:(cb + 1) * LANES]
        ev.append(scr_ref[cb, pl.ds(0, n // 2, stride=2), :])
        od.append(scr_ref[cb, pl.ds(1, n // 2, stride=2), :])
    return jnp.concatenate(ev, axis=1), jnp.concatenate(od, axis=1)


def _interleave(ev, od, scr_ref):
    n2 = ev.shape[0]
    cols = []
    for cb in range(ev.shape[1] // LANES):
        scr_ref[cb, pl.ds(0, n2, stride=2), :] = ev[:, cb * LANES:(cb + 1) * LANES]
        scr_ref[cb, pl.ds(1, n2, stride=2), :] = od[:, cb * LANES:(cb + 1) * LANES]
        cols.append(scr_ref[cb])
    return jnp.concatenate(cols, axis=1)


def _proj_kernel(x_ref, sh_ref, sc_ref, g_ref, wa_ref, wb_ref, cs_ref, oa_ref, oh_ref, ob_ref, of_ref, scr_ref):
    m = cs_ref.shape[0]
    h = _ada_norm(x_ref[0], g_ref[...], sh_ref[0, 0], sc_ref[0, 0]).astype(BF16)
    pa = _dot(h, wa_ref[...])
    oa_ref[0] = pa[:, :m]
    oh_ref[0] = pa[:, m:].astype(BF16)
    pb = _dot(h, wb_ref[...])
    ob_ref[0] = pb[:, m:].astype(BF16)
    ev, od = _deinterleave(_dot(pb[:, :m].astype(BF16), cs_ref[...]), scr_ref)
    of_ref[0, 0] = ev.astype(BF16)
    of_ref[0, 1] = od.astype(BF16)


def _proj_call(x, mod4, g, wa, wb, cs, tm):
    b, l, d = x.shape
    m = cs.shape[0]
    na, nb = wa.shape[1], wb.shape[1]
    const = lambda bi, i: (0, 0)
    return pl.pallas_call(
        _proj_kernel,
        grid=(b, l // tm),
        in_specs=[
            pl.BlockSpec((1, tm, d), lambda bi, i: (bi, i, 0)),
            pl.BlockSpec((1, 1, 1, d), lambda bi, i: (bi, 0, 0, 0)),
            pl.BlockSpec((1, 1, 1, d), lambda bi, i: (bi, 1, 0, 0)),
            pl.BlockSpec((1, d), const),
            pl.BlockSpec((d, na), const),
            pl.BlockSpec((d, nb), const),
            pl.BlockSpec((m, 2 * m), const),
        ],
        out_specs=[
            pl.BlockSpec((1, tm, m), lambda bi, i: (bi, i, 0)),
            pl.BlockSpec((1, tm, na - m), lambda bi, i: (bi, i, 0)),
            pl.BlockSpec((1, tm, nb - m), lambda bi, i: (bi, i, 0)),
            pl.BlockSpec((1, 2, tm // 2, 2 * m), lambda bi, i: (bi, 0, i, 0)),
        ],
        out_shape=[
            jax.ShapeDtypeStruct((b, l, m), F32),
            jax.ShapeDtypeStruct((b, l, na - m), BF16),
            jax.ShapeDtypeStruct((b, l, nb - m), BF16),
            jax.ShapeDtypeStruct((b, 2, l // 2, 2 * m), BF16),
        ],
        scratch_shapes=[pltpu.VMEM((2 * m // LANES, tm, LANES), F32)],
        compiler_params=_params(("parallel", "parallel")),
        name="proj",
    )(x, mod4, mod4, g, wa, wb, cs)


def _fill_padded(src_ref, pad_ref, l, rows):
    c = pad_ref.shape[1]
    pad_ref[0:PAD, :] = jnp.zeros((PAD, c), F32)
    pad_ref[l + PAD:l + 2 * PAD, :] = jnp.zeros((PAD, c), F32)

    def copy(i, carry):
        r = pl.multiple_of(i * rows, rows)
        pad_ref[pl.ds(r + PAD, rows), :] = src_ref[0, pl.ds(r, rows), :].astype(F32)
        return carry

    lax.fori_loop(0, l // rows, copy, 0)


def _pool_kernel(u_ref, w_ref, s_ref, o_ref, pad_ref, *, rows):
    l, c = u_ref.shape[1], u_ref.shape[2]
    gw = c // len(POOL_WINDOWS)
    _fill_padded(u_ref, pad_ref, l, rows)
    n = rows + 2 * PAD
    lane = lax.broadcasted_iota(jnp.int32, (n, c), 1)
    row = lax.broadcasted_iota(jnp.int32, (n, c), 0)
    grp = lax.shift_right_logical(lane, int(math.log2(gw)))
    half = jnp.where(grp == 0, 1, jnp.where(grp == 1, 2, jnp.where(grp == 2, 4, 8)))
    w = w_ref[...]
    scale = s_ref[...]

    def body(i, carry):
        r = pl.multiple_of(i * rows, rows)
        win = pad_ref[pl.ds(r, n), :]
        a2 = win + pltpu.roll(win, 1, 0)
        w4 = pltpu.roll(a2, 1, 0) + pltpu.roll(a2, n - 1, 0)
        w8 = pltpu.roll(w4, 2, 0) + pltpu.roll(w4, n - 2, 0)
        w16 = pltpu.roll(w8, 4, 0) + pltpu.roll(w8, n - 4, 0)
        s = jnp.where(grp == 0, a2, jnp.where(grp == 1, w4, jnp.where(grp == 2, w8, w16)))
        t = row + (r - PAD)
        cnt = (jnp.minimum(t + half, l) - jnp.maximum(t - half, 0)).astype(F32)
        y = (s / jnp.maximum(cnt, 1.0) - win)[PAD:PAD + rows]
        o_ref[0, pl.ds(r, rows), :] = (_dot(y.astype(BF16), w) * scale).astype(BF16)
        return carry

    lax.fori_loop(0, l // rows, body, 0)


def _pool_call(oa, w_blk, scale, rows):
    b, l, _ = oa.shape
    c = w_blk.shape[0]
    return pl.pallas_call(
        functools.partial(_pool_kernel, rows=rows),
        grid=(b,),
        in_specs=[
            pl.BlockSpec((1, l, c), lambda bi: (bi, 0, 0)),
            pl.BlockSpec((c, c), lambda bi: (0, 0)),
            pl.BlockSpec((1, c), lambda bi: (0, 0)),
        ],
        out_specs=pl.BlockSpec((1, l, c), lambda bi: (bi, 0, 0)),
        out_shape=jax.ShapeDtypeStruct((b, l, c), BF16),
        scratch_shapes=[pltpu.VMEM((l + 2 * PAD, c), F32)],
        compiler_params=_params(("parallel",)),
        name="pool",
    )(oa, w_blk, scale)


def _dwconv_kernel(u_ref, w_ref, b_ref, o_ref, pad_ref, scr_ref, *, rows):
    l = u_ref.shape[1]
    _fill_padded(u_ref, pad_ref, l, rows)
    n = rows + 2 * PAD
    w = w_ref[...]
    bias = b_ref[...]

    def body(i, carry):
        r = pl.multiple_of(i * rows, rows)
        r2 = pl.multiple_of(i * (rows // 2), rows // 2)
        win = pad_ref[pl.ds(r, n), :]
        y = pltpu.roll(win, 1, 0) * w[0:1] + win * w[1:2] + pltpu.roll(win, n - 1, 0) * w[2:3] + bias
        ev, od = _deinterleave(y[PAD:PAD + rows], scr_ref)
        o_ref[0, 0, pl.ds(r2, rows // 2), :] = ev.astype(BF16)
        o_ref[0, 1, pl.ds(r2, rows // 2), :] = od.astype(BF16)
        return carry

    lax.fori_loop(0, l // rows, body, 0)


def _dwconv_call(u, w, bias, c, rows):
    b, l, _ = u.shape
    nblk = w.shape[1] // c
    return pl.pallas_call(
        functools.partial(_dwconv_kernel, rows=rows),
        grid=(b, nblk),
        in_specs=[
            pl.BlockSpec((1, l, c), lambda bi, j: (bi, 0, j)),
            pl.BlockSpec((w.shape[0], c), lambda bi, j: (0, j)),
            pl.BlockSpec((1, c), lambda bi, j: (0, j)),
        ],
        out_specs=pl.BlockSpec((1, 2, l // 2, c), lambda bi, j: (bi, 0, 0, j)),
        out_shape=jax.ShapeDtypeStruct((b, 2, l // 2, nblk * c), BF16),
        scratch_shapes=[pltpu.VMEM((l + 2 * PAD, c), F32), pltpu.VMEM((c // LANES, rows, LANES), F32)],
        compiler_params=_params(("parallel", "parallel")),
        name="dwconv",
    )(u, w, bias)


def _filt_kernel(f_ref, w1_ref, b1_ref, w2_ref, b2_ref, w3_ref, b3_ref, fr_ref, dl_ref,
                 hb_ref, row0_ref, asum_ref):
    hp = lax.Precision.HIGHEST
    feats = f_ref[...]
    freq = fr_ref[...]
    h = jnp.sin(freq * (jnp.dot(feats, w1_ref[...], precision=hp, preferred_element_type=F32) + b1_ref[...]))
    h = jnp.sin(freq * (jnp.dot(h, w2_ref[...], precision=hp, preferred_element_type=F32) + b2_ref[...]))
    h = jnp.dot(h, w3_ref[...], precision=hp, preferred_element_type=F32) + b3_ref[...]
    t = feats[:, 0:1]
    win = jnp.exp(-t * jnp.abs(dl_ref[...])) + HYENA_DECAY_SHIFT
    h = h * jnp.concatenate([win] * (h.shape[1] // win.shape[1]), axis=1)
    hb = h.astype(BF16)
    hb_ref[...] = hb
    part = jnp.sum(jnp.abs(h), axis=0, keepdims=True)

    @pl.when(pl.program_id(0) == 0)
    def _():
        asum_ref[...] = jnp.zeros_like(asum_ref)
        row0_ref[...] = hb[0:16].astype(F32)[0:8]

    asum_ref[...] += part


def _filt_call(feats, w1, b1, w2, b2, w3, b3, freq, deltas, rows):
    l, fd = feats.shape
    hd = w2.shape[0]
    n = w3.shape[1]
    c = deltas.shape[1]
    const = lambda i: (0, 0)
    return pl.pallas_call(
        _filt_kernel,
        grid=(l // rows,),
        in_specs=[
            pl.BlockSpec((rows, fd), lambda i: (i, 0)),
            pl.BlockSpec((fd, hd), const), pl.BlockSpec((1, hd), const),
            pl.BlockSpec((hd, hd), const), pl.BlockSpec((1, hd), const),
            pl.BlockSpec((hd, n), const), pl.BlockSpec((1, n), const),
            pl.BlockSpec((1, hd), const), pl.BlockSpec((1, c), const),
        ],
        out_specs=[
            pl.BlockSpec((rows, n), lambda i: (i, 0)),
            pl.BlockSpec((8, n), const),
            pl.BlockSpec((1, n), const),
        ],
        out_shape=[
            jax.ShapeDtypeStruct((l, n), BF16),
            jax.ShapeDtypeStruct((8, n), F32),
            jax.ShapeDtypeStruct((1, n), F32),
        ],
        compiler_params=_params(("arbitrary",)),
        name="hyena_filt",
    )(feats, w1, b1, w2, b2, w3, b3, freq, deltas)


def _alt_signs(n):
    t = lax.broadcasted_iota(jnp.int32, (16, n), 1)
    return jnp.where((t & 1) == 0, 1.0, -1.0).astype(BF16)


def _pair_transform(ce, co, se, so, ze_parts, zo_parts):
    ae = sum(_dot(ce, p) for p in ze_parts)
    ao = sum(_dot(co, p) for p in zo_parts)
    be = sum(_dot(se, p) for p in ze_parts)
    bo = sum(_dot(so, p) for p in zo_parts)
    return ae + ao, ae - ao, be + bo, bo - be


def _kspec_kernel(ce_ref, co_ref, se_ref, so_ref, h_ref, row0_ref, asum_ref, kk_ref, km_ref):
    i = pl.program_id(1)
    tm, half = ce_ref.shape
    c = kk_ref.shape[3]
    l = 2 * half
    he = (h_ref[0:half, :],)
    ho = (h_ref[half:l, :],)
    hc_lo, hc_hi, hs_lo, hs_hi = _pair_transform(ce_ref[...], co_ref[...], se_ref[...], so_ref[...], he, ho)
    asum = asum_ref[...]
    inv = 1.0 / (asum[:, :c] + asum[:, c:] + EPS)
    hb0 = row0_ref[0:1, c:]
    row = lax.broadcasted_iota(jnp.int32, (tm, c), 0) + i * tm
    wj = jnp.where(row == 0, 0.5 / l, 1.0 / l) * inv
    kk_ref[0, 0] = (hc_lo[:, :c] + hc_lo[:, c:] - hb0) * wj
    kk_ref[0, 1] = (hs_lo[:, :c] - hs_lo[:, c:]) * wj
    kk_ref[0, 2] = (hc_hi[:, :c] + hc_hi[:, c:] - hb0) * wj
    kk_ref[0, 3] = (hs_hi[:, :c] - hs_hi[:, c:]) * wj

    @pl.when(i == 0)
    def _():
        sg = _alt_signs(half)
        mc = sum(_dot(sg, p) for p in he)[0:8]
        ms = sum(_dot(sg, p) for p in ho)[0:8]
        km_ref[0, :, 0:c] = (mc[:, :c] + mc[:, c:] - hb0) * inv * (1.0 / l)
        km_ref[0, :, c:2 * c] = (ms[:, :c] - ms[:, c:]) * inv * (1.0 / l)


def _mat_specs(tm, half, n):
    return [pl.BlockSpec((tm, half), lambda g, i: (i, 0)) for _ in range(n)]


def _kspec_call(mats, h, row0, asum, c, tm):
    half = mats[0].shape[0]
    l = 2 * half
    orders = h.shape[1] // (2 * c)
    return pl.pallas_call(
        _kspec_kernel,
        grid=(orders, half // tm),
        in_specs=_mat_specs(tm, half, 4) + [
            pl.BlockSpec((l, 2 * c), lambda o, i: (0, o)),
            pl.BlockSpec((8, 2 * c), lambda o, i: (0, o)),
            pl.BlockSpec((1, 2 * c), lambda o, i: (0, o)),
        ],
        out_specs=[
            pl.BlockSpec((1, 4, tm, c), lambda o, i: (o, 0, i, 0)),
            pl.BlockSpec((1, 8, 2 * c), lambda o, i: (o, 0, 0)),
        ],
        out_shape=[
            jax.ShapeDtypeStruct((orders, 4, half, c), F32),
            jax.ShapeDtypeStruct((orders, 8, 2 * c), F32),
        ],
        compiler_params=_params(("arbitrary", "arbitrary")),
        name="hyena_kspec",
    )(*mats, h, row0, asum)


def _hfwd_kernel(ce_ref, co_ref, se_ref, so_ref, z_ref, kk_ref, km_ref, spec_ref, ym_ref):
    nb = z_ref.shape[0]
    c = z_ref.shape[3]
    ce, co, se, so = ce_ref[...], co_ref[...], se_ref[...], so_ref[...]
    kr_lo, ks_lo, kr_hi, ks_hi = kk_ref[0, 0], kk_ref[0, 1], kk_ref[0, 2], kk_ref[0, 3]
    for bb in range(nb):
        zr_lo, zr_hi, zs_lo, zs_hi = _pair_transform(ce, co, se, so, (z_ref[bb, 0],), (z_ref[bb, 1],))
        yr_lo = zr_lo * kr_lo - zs_lo * ks_lo
        ys_lo = zr_lo * ks_lo + zs_lo * kr_lo
        yr_hi = zr_hi * kr_hi - zs_hi * ks_hi
        ys_hi = zr_hi * ks_hi + zs_hi * kr_hi
        spec_ref[bb, 0] = (yr_lo + yr_hi).astype(BF16)
        spec_ref[bb, 1] = (ys_lo - ys_hi).astype(BF16)
        spec_ref[bb, 2] = (yr_lo - yr_hi).astype(BF16)
        spec_ref[bb, 3] = (ys_lo + ys_hi).astype(BF16)

    @pl.when(pl.program_id(1) == 0)
    def _():
        sg = _alt_signs(z_ref.shape[2])
        km_r, km_s = km_ref[0, :, 0:c], km_ref[0, :, c:2 * c]
        for bb in range(nb):
            zr = _dot(sg, z_ref[bb, 0])[0:8]
            zs = _dot(sg, z_ref[bb, 1])[0:8]
            ym_ref[bb, :, 0:c] = zr * km_r - zs * km_s
            ym_ref[bb, :, c:2 * c] = zr * km_s + zs * km_r


def _hfwd_call(mats, zb, kk, km, order, c, nb, tm):
    b, _, half, _ = zb.shape
    return pl.pallas_call(
        _hfwd_kernel,
        grid=(b // nb, half // tm),
        in_specs=_mat_specs(tm, half, 4) + [
            pl.BlockSpec((nb, 2, half, c), lambda g, i: (g, 0, 0, 0)),
            pl.BlockSpec((1, 4, tm, c), lambda g, i: (order, 0, i, 0)),
            pl.BlockSpec((1, 8, 2 * c), lambda g, i: (order, 0, 0)),
        ],
        out_specs=[
            pl.BlockSpec((nb, 4, tm, c), lambda g, i: (g, 0, i, 0)),
            pl.BlockSpec((nb, 8, 2 * c), lambda g, i: (g, 0, 0)),
        ],
        out_shape=[
            jax.ShapeDtypeStruct((b, 4, half, c), BF16),
            jax.ShapeDtypeStruct((b, 8, 2 * c), F32),
        ],
        compiler_params=_params(("arbitrary", "arbitrary")),
        name="hyena_fwd",
    )(*mats, zb, kk, km)


def _hinv_kernel(ce_ref, se_ref, cot_ref, sot_ref, spec_ref, ym_ref, z_ref, gate_ref, skip_ref, *refs,
                 natural_out):
    nb, _, tm, c = z_ref.shape
    ce, se, cot, sot = ce_ref[...], se_ref[...], cot_ref[...], sot_ref[...]
    row = lax.broadcasted_iota(jnp.int32, (tm, c), 0)
    even_row = (row & 1) == 0
    skip = skip_ref[0]
    for bb in range(nb):
        ym_r, ym_s = ym_ref[bb, 0:1, 0:c], ym_ref[bb, 0:1, c:2 * c]
        ye = _dot(ce, spec_ref[bb, 0]) + _dot(se, spec_ref[bb, 1]) + jnp.where(even_row, ym_r, -ym_r)
        yo = _dot(cot, spec_ref[bb, 2]) + _dot(sot, spec_ref[bb, 3]) + jnp.where(even_row, ym_s, -ym_s)
        oe = gate_ref[bb, 0].astype(F32) * (ye + skip * z_ref[bb, 0].astype(F32))
        oo = gate_ref[bb, 1].astype(F32) * (yo + skip * z_ref[bb, 1].astype(F32))
        if natural_out:
            o_ref, scr_ref = refs
            o_ref[bb] = _interleave(oe, oo, scr_ref).astype(BF16)
        else:
            (o_ref,) = refs
            o_ref[bb, 0] = oe.astype(BF16)
            o_ref[bb, 1] = oo.astype(BF16)


def _hinv_call(mats, spec, ym, zf, zcol, uc, gcol, skip, order, natural_out, c, nb, tm):
    b, _, half, _ = spec.shape
    if natural_out:
        out_specs = pl.BlockSpec((nb, 2 * tm, c), lambda g, i: (g, i, 0))
        out_shape = jax.ShapeDtypeStruct((b, 2 * half, c), BF16)
        scratch = [pltpu.VMEM((c // LANES, 2 * tm, LANES), F32)]
    else:
        out_specs = pl.BlockSpec((nb, 2, tm, c), lambda g, i: (g, 0, i, 0))
        out_shape = jax.ShapeDtypeStruct((b, 2, half, c), BF16)
        scratch = []
    return pl.pallas_call(
        functools.partial(_hinv_kernel, natural_out=natural_out),
        grid=(b // nb, half // tm),
        in_specs=_mat_specs(tm, half, 4) + [
            pl.BlockSpec((nb, 4, half, c), lambda g, i: (g, 0, 0, 0)),
            pl.BlockSpec((nb, 8, 2 * c), lambda g, i: (g, 0, 0)),
            pl.BlockSpec((nb, 2, tm, c), lambda g, i: (g, 0, i, zcol)),
            pl.BlockSpec((nb, 2, tm, c), lambda g, i: (g, 0, i, gcol)),
            pl.BlockSpec((1, 1, c), lambda g, i: (order, 0, 0)),
        ],
        out_specs=out_specs,
        out_shape=out_shape,
        scratch_shapes=scratch,
        compiler_params=_params(("arbitrary", "arbitrary")),
        name="hyena_inv",
    )(*mats, spec, ym, zf, uc, skip)


def _fseq_kernel(ce_ref, co_ref, se_ref, so_ref, u_ref, o_ref):
    nb = u_ref.shape[0]
    c = o_ref.shape[3]
    ce, co, se, so = ce_ref[...], co_ref[...], se_ref[...], so_ref[...]
    for bb in range(nb):
        ae = _dot(ce, u_ref[bb, 0, :, 0:c]) + _dot(se, u_ref[bb, 0, :, c:2 * c])
        ao = _dot(co, u_ref[bb, 1, :, 0:c]) + _dot(so, u_ref[bb, 1, :, c:2 * c])
        o_ref[bb, 0] = (ae + ao).astype(BF16)
        o_ref[bb, 1] = (ae - ao).astype(BF16)


def _fseq_call(mats, ucs, nb, tm):
    b, _, half, c2 = ucs.shape
    c = c2 // 2
    out = pl.pallas_call(
        _fseq_kernel,
        grid=(b // nb, half // tm),
        in_specs=_mat_specs(tm, half, 4) + [pl.BlockSpec((nb, 2, half, c2), lambda g, i: (g, 0, 0, 0))],
        out_specs=pl.BlockSpec((nb, 2, tm, c), lambda g, i: (g, 0, i, 0)),
        out_shape=jax.ShapeDtypeStruct((b, 2, half, c), BF16),
        compiler_params=_params(("arbitrary", "arbitrary")),
        name="fnet_seq",
    )(*mats, ucs)
    return out.reshape(b, 2 * half, c)


def _stack_heads(q, heads):
    lane = lax.broadcasted_iota(jnp.int32, q.shape, 1)
    zero = jnp.zeros_like(q)
    return jnp.concatenate([jnp.where(lax.shift_right_logical(lane, int(math.log2(NA_HEAD_DIM))) == h, q, zero) for h in range(heads)], axis=0)


def _merge_heads(o, heads, n):
    lane = lax.broadcasted_iota(jnp.int32, (n, o.shape[1]), 1)
    out = jnp.zeros((n, o.shape[1]), F32)
    for h in range(heads):
        out = out + jnp.where(lax.shift_right_logical(lane, int(math.log2(NA_HEAD_DIM))) == h, o[h * n:(h + 1) * n], 0.0)
    return out


def _natt_kernel(q_ref, k_ref, v_ref, kc_ref, vc_ref, bias_ref, o_ref, *, rows, kh, group):
    w = GRID_W
    heads = q_ref.shape[2] // NA_HEAD_DIM
    base = pl.program_id(1) * group
    lo = kh // 2
    kc, vc = kc_ref[0], vc_ref[0]
    scale = jnp.asarray(NA_HEAD_DIM ** -0.5, BF16)

    def body(j, carry):
        r = base + j
        r0 = jnp.clip(r - lo, 0, rows - kh)
        start = pl.multiple_of(r0 * w, w)
        cls = jnp.minimum(r, lo) + jnp.maximum(r - (rows - kh + lo), 0)
        q4 = _stack_heads(q_ref[0, pl.ds(pl.multiple_of(j * w, w), w), :] * scale, heads)
        ks = k_ref[0, pl.ds(start, kh * w), :]
        vs = v_ref[0, pl.ds(start, kh * w), :]
        s_nb = _dot_t(q4, ks) + bias_ref[cls]
        s_cx = _dot_t(q4, kc)
        m = jnp.maximum(jnp.max(s_nb, axis=-1, keepdims=True), jnp.max(s_cx, axis=-1, keepdims=True))
        p_nb = jnp.exp(s_nb - m)
        p_cx = jnp.exp(s_cx - m)
        den = jnp.sum(p_nb, axis=-1, keepdims=True) + jnp.sum(p_cx, axis=-1, keepdims=True)
        o4 = (_dot(p_nb.astype(BF16), vs) + _dot(p_cx.astype(BF16), vc)) / den
        o_ref[0, pl.ds(pl.multiple_of(j * w, w), w), :] = _merge_heads(o4, heads, w).astype(BF16)
        return carry

    lax.fori_loop(0, group, body, 0, unroll=4)


def _natt_call(ob, obc, bias, c, group):
    b, l, _ = ob.shape
    lc = obc.shape[1]
    w = GRID_W
    rows = l // w
    kh = min(NA_WIN_H, rows)
    return pl.pallas_call(
        functools.partial(_natt_kernel, rows=rows, kh=kh, group=group),
        grid=(b, rows // group),
        in_specs=[
            pl.BlockSpec((1, group * w, c), lambda bi, r: (bi, r, 0)),
            pl.BlockSpec((1, l, c), lambda bi, r: (bi, 0, 1)),
            pl.BlockSpec((1, l, c), lambda bi, r: (bi, 0, 2)),
            pl.BlockSpec((1, lc, c), lambda bi, r: (bi, 0, 1)),
            pl.BlockSpec((1, lc, c), lambda bi, r: (bi, 0, 2)),
            pl.BlockSpec(bias.shape, lambda bi, r: (0, 0, 0)),
        ],
        out_specs=pl.BlockSpec((1, group * w, c), lambda bi, r: (bi, r, 0)),
        out_shape=jax.ShapeDtypeStruct((b, l, c), BF16),
        compiler_params=_params(("parallel", "arbitrary")),
        name="nb_attn",
    )(ob, ob, ob, obc, obc, bias)


def _bias_kernel(r_ref, o_ref):
    w = GRID_W
    shift = int(math.log2(w))
    n = o_ref.shape[1]
    r = r_ref[...]
    hi = r.astype(BF16)
    mid = (r - hi.astype(F32)).astype(BF16)
    lo = (r - hi.astype(F32) - mid.astype(F32)).astype(BF16)
    dc = lax.broadcasted_iota(jnp.int32, (r.shape[1], n), 0)
    p = lax.broadcasted_iota(jnp.int32, (r.shape[1], n), 1)
    idx = jnp.clip((p & (w - 1)) - lax.shift_right_logical(p, shift), 1 - NA_WIN_W, NA_WIN_W - 1) + (NA_WIN_W - 1)
    onehot = jnp.where(dc == idx, 1.0, 0.0).astype(BF16)
    val = _dot(hi, onehot) + _dot(mid, onehot) + _dot(lo, onehot)
    po = lax.broadcasted_iota(jnp.int32, val.shape, 1)
    cq = lax.shift_right_logical(po, shift)
    ck = po & (w - 1)
    c0 = jnp.clip(cq - NA_WIN_W // 2, 0, w - NA_WIN_W)
    inside = jnp.logical_and(ck >= c0, ck < c0 + NA_WIN_W)
    o_ref[...] = jnp.where(inside, val, NEG_INF)


def _bias_call(rpb2d):
    rows, cols = rpb2d.shape
    n = GRID_W * GRID_W
    return pl.pallas_call(
        _bias_kernel,
        grid=(1,),
        in_specs=[pl.BlockSpec((rows, cols), lambda i: (0, 0))],
        out_specs=pl.BlockSpec((rows, n), lambda i: (0, 0)),
        out_shape=jax.ShapeDtypeStruct((rows, n), F32),
        compiler_params=_params(("arbitrary",)),
        name="rpb_table",
    )(rpb2d)


def _catt_kernel(q_ref, k_ref, v_ref, o_ref):
    n = q_ref.shape[1]
    heads = q_ref.shape[2] // NA_HEAD_DIM
    q = q_ref[0] * jnp.asarray(NA_HEAD_DIM ** -0.5, BF16)
    q4 = _stack_heads(q, heads)
    s = _dot_t(q4, k_ref[0])
    p = jnp.exp(s - jnp.max(s, axis=-1, keepdims=True))
    den = jnp.sum(p, axis=-1, keepdims=True)
    o4 = _dot(p.astype(BF16), v_ref[0]) / den
    o_ref[0] = _merge_heads(o4, heads, n).astype(BF16)


def _catt_call(obc, c):
    b, lc, _ = obc.shape
    return pl.pallas_call(
        _catt_kernel,
        grid=(b,),
        in_specs=[pl.BlockSpec((1, lc, c), lambda bi, j=j: (bi, 0, j)) for j in range(3)],
        out_specs=pl.BlockSpec((1, lc, c), lambda bi: (bi, 0, 0)),
        out_shape=jax.ShapeDtypeStruct((b, lc, c), BF16),
        compiler_params=_params(("parallel",)),
        name="ctx_attn",
    )(obc, obc, obc)


def _merge_kernel(x_ref, sh_ref, sc_ref, gt_ref, g_ref, y0_ref, y1_ref, y2_ref, y3_ref,
                  wg_ref, wb_ref, wo_ref, o_ref):
    d = x_ref.shape[2]
    x = x_ref[0]
    h = _ada_norm(x, g_ref[...], sh_ref[0, 0], sc_ref[0, 0]).astype(BF16)
    merged = None
    for bi, y_ref in enumerate((y0_ref, y1_ref, y2_ref, y3_ref)):
        gate = jax.nn.sigmoid(_dot(h, wg_ref[:, bi * d:(bi + 1) * d]))
        term = gate * _dot(y_ref[0].astype(BF16), wb_ref[bi])
        merged = term if merged is None else merged + term
    o_ref[0] = x + gt_ref[0, 0] * _dot(merged.astype(BF16), wo_ref[...])


def _merge_call(x, mod4, g, ys, wg, wb, wo, tm):
    b, l, d = x.shape
    c = ys[0].shape[2]
    const2 = lambda bi, i: (0, 0)
    mod_spec = lambda k: pl.BlockSpec((1, 1, 1, d), lambda bi, i: (bi, k, 0, 0))
    return pl.pallas_call(
        _merge_kernel,
        grid=(b, l // tm),
        in_specs=[
            pl.BlockSpec((1, tm, d), lambda bi, i: (bi, i, 0)),
            mod_spec(0), mod_spec(1), mod_spec(2),
            pl.BlockSpec((1, d), const2),
        ] + [pl.BlockSpec((1, tm, c), lambda bi, i: (bi, i, 0)) for _ in ys] + [
            pl.BlockSpec(wg.shape, const2, pipeline_mode=pl.Buffered(1)),
            pl.BlockSpec(wb.shape, lambda bi, i: (0, 0, 0), pipeline_mode=pl.Buffered(1)),
            pl.BlockSpec(wo.shape, const2, pipeline_mode=pl.Buffered(1)),
        ],
        out_specs=pl.BlockSpec((1, tm, d), lambda bi, i: (bi, i, 0)),
        out_shape=jax.ShapeDtypeStruct((b, l, d), F32),
        compiler_params=_params(("parallel", "parallel")),
        name="merge",
    )(x, mod4, mod4, mod4, g, *ys, wg, wb, wo)


def _ffn_kernel(x_ref, xp_ref, xn_ref, sh_ref, sc_ref, gt_ref, g_ref, wu_ref, wgt_ref, cw_ref, cb_ref,
                wd_ref, fg_ref, o_ref, a_ref, *, chunk, final_norm):
    i = pl.program_id(1)
    last = pl.num_programs(1) - 1
    tm = x_ref.shape[1]
    f = wu_ref.shape[1]
    n = tm + 2 * HALO
    g, sh, sc = g_ref[...], sh_ref[0, 0], sc_ref[0, 0]
    x = x_ref[0]
    h = _ada_norm(x, g, sh, sc).astype(BF16)
    hp = _ada_norm(xp_ref[0], g, sh, sc).astype(BF16)
    hn = _ada_norm(xn_ref[0], g, sh, sc).astype(BF16)
    hext = jnp.concatenate([hp, h, hn], axis=0)
    row = lax.broadcasted_iota(jnp.int32, (n, chunk), 0)
    valid = jnp.logical_and(jnp.logical_or(row >= HALO, i > 0), jnp.logical_or(row < HALO + tm, i < last))
    for j in range(f // chunk):
        cols = slice(j * chunk, (j + 1) * chunk)
        gp = jnp.where(valid, _dot(hext, wgt_ref[:, cols]), 0.0)
        cw = cw_ref[:, cols]
        gc = (pltpu.roll(gp, 1, 0) * cw[0:1] + gp * cw[1:2] + pltpu.roll(gp, n - 1, 0) * cw[2:3]
              + cb_ref[:, cols])[HALO:HALO + tm]
        u = _dot(h, wu_ref[:, cols])
        a_ref[:, cols] = (gc * jax.nn.sigmoid(gc) * u).astype(BF16)
    y = x + gt_ref[0, 0] * _dot(a_ref[...], wd_ref[...])
    if final_norm:
        y = y * lax.rsqrt(jnp.mean(y * y, axis=-1, keepdims=True) + EPS) * fg_ref[...]
    o_ref[0] = y


def _ffn_call(x, mod4, g, wu, wgt, cw, cb, wd, fg, tm, chunk, final_norm):
    b, l, d = x.shape
    f = wu.shape[1]
    hb = tm // HALO
    nblk = l // HALO
    const2 = lambda bi, i: (0, 0)
    mod_spec = lambda k: pl.BlockSpec((1, 1, 1, d), lambda bi, i: (bi, k, 0, 0))
    single = dict(pipeline_mode=pl.Buffered(1))
    return pl.pallas_call(
        functools.partial(_ffn_kernel, chunk=chunk, final_norm=final_norm),
        grid=(b, l // tm),
        in_specs=[
            pl.BlockSpec((1, tm, d), lambda bi, i: (bi, i, 0)),
            pl.BlockSpec((1, HALO, d), lambda bi, i: (bi, jnp.maximum(i * hb - 1, 0), 0)),
            pl.BlockSpec((1, HALO, d), lambda bi, i: (bi, jnp.minimum((i + 1) * hb, nblk - 1), 0)),
            mod_spec(3), mod_spec(4), mod_spec(5),
            pl.BlockSpec((1, d), const2),
            pl.BlockSpec((d, f), const2, **single),
            pl.BlockSpec((d, f), const2, **single),
            pl.BlockSpec((cw.shape[0], f), const2),
            pl.BlockSpec((1, f), const2),
            pl.BlockSpec((f, d), const2, **single),
            pl.BlockSpec((1, d), const2),
        ],
        out_specs=pl.BlockSpec((1, tm, d), lambda bi, i: (bi, i, 0)),
        out_shape=jax.ShapeDtypeStruct((b, l, d), F32),
        scratch_shapes=[pltpu.VMEM((tm, f), BF16)],
        compiler_params=_params(("parallel", "arbitrary")),
        name="ffn",
    )(x, x, x, mod4, mod4, mod4, g, wu, wgt, cw, cb, wd, fg)


def _trig_tables(n, row_mul, row_add, col_mul, col_add, period):
    a_sz = 64 if n % 64 == 0 else 1
    q = (jnp.arange(n, dtype=jnp.int32) * col_mul + col_add)[None, :]
    ra = (jnp.arange(n // a_sz, dtype=jnp.int32) * (a_sz * row_mul))[:, None]
    rb = (jnp.arange(a_sz, dtype=jnp.int32) * row_mul + row_add)[:, None]
    w = 2.0 * math.pi / period
    ang_a = ((ra * q) % period).astype(F32) * w
    ang_b = ((rb * q) % period).astype(F32) * w
    ca, sa = jnp.cos(ang_a)[:, None, :], jnp.sin(ang_a)[:, None, :]
    cb, sb = jnp.cos(ang_b)[None, :, :], jnp.sin(ang_b)[None, :, :]
    cos = (ca * cb - sa * sb).reshape(n, n)
    sin = (sa * cb + ca * sb).reshape(n, n)
    return cos, sin


def _pad2(a, rows, cols):
    return jnp.pad(a, ((0, rows - a.shape[0]), (0, cols - a.shape[1])))


def _hyena_features(l, pad_to):
    t = jnp.linspace(0.0, 1.0, l, dtype=F32)[:, None]
    bands = jnp.linspace(1e-4, HYENA_BANDS - 1, HYENA_BANDS, dtype=F32)[None, :]
    ang = (2.0 * math.pi / l) * jnp.arange(l, dtype=F32)[:, None] * bands
    feats = jnp.concatenate([t, jnp.cos(ang), -jnp.sin(ang)], axis=-1)
    return jnp.pad(feats, ((0, 0), (0, pad_to - feats.shape[1])))


def _bias_classes(rows, kh):
    r = np.arange(rows)
    r0 = np.clip(r - kh // 2, 0, rows - kh)
    off = r0 - r
    lo = kh // 2
    reps = list(range(lo)) + [lo] + list(range(rows - kh + lo + 1, rows))
    return [int(off[i]) for i in reps]


def _attention_bias(rpb, rows, kh):
    heads, nr, nc = rpb.shape
    w = GRID_W
    flat = rpb.astype(F32).reshape(heads * nr, nc)
    table = _bias_call(_pad2(flat, -(-heads * nr // 16) * 16, 128))
    table = table[:heads * nr].reshape(heads, nr, w, w)
    out = []
    for off in _bias_classes(rows, kh):
        first = off + NA_WIN_H - 1
        sl = table[:, first:first + kh]
        out.append(jnp.transpose(sl, (0, 2, 1, 3)).reshape(heads * w, kh * w))
    return jnp.stack(out)


def _mixer_inputs(x, mod4, lw, tm):
    return _proj_call(x, mod4, lw['norm1_g'], lw['wa'], lw['wb'], lw['cs'], tm)


def _hyena(oh, lw, tabs, c, nb_fwd, nb, tm, rows):
    fwd, inv = tabs['hy_fwd'], tabs['hy_inv']
    uc = _dwconv_call(oh, lw['hyena_conv_w'], lw['hyena_conv_b'], c, rows)
    h, row0, asum = _filt_call(tabs['feats'], lw['fw1'], lw['fb1'], lw['fw2'], lw['fb2'], lw['fw3'],
                               lw['fb3'], lw['freq'], tabs['deltas'], min(rows * 2, oh.shape[1]))
    kk, km = _kspec_call(fwd, h, row0, asum, c, tm)
    spec, ym = _hfwd_call(fwd, uc, kk, km, 0, c, nb_fwd, tm)
    z2 = _hinv_call(inv, spec, ym, uc, 0, uc, 1, lw['skip'], 0, False, c, nb, tm)
    spec, ym = _hfwd_call(fwd, z2, kk, km, 1, c, nb_fwd, tm)
    return _hinv_call(inv, spec, ym, z2, 0, uc, 2, lw['skip'], 1, True, c, nb, tm)


def _stream_layer(x, mod4, lw, tabs, proj, y_att, cfg, final_norm):
    c = cfg['c']
    oa, oh, ob, of = proj
    y_pool = _pool_call(oa, lw['pool_blk'], lw['pool_scale'], cfg['rows'])
    y_fnet = _fseq_call(tabs['fn'], of, cfg['nb'], cfg['tseq'])
    y_hyena = _hyena(oh, lw, tabs, c, cfg['nb_fwd'], cfg['nb'], cfg['tseq'], cfg['rows'])
    x = _merge_call(x, mod4, lw['norm1_g'], (y_pool, y_fnet, y_hyena, y_att), lw['wg'], lw['w_branch'],
                    lw['w_out'], cfg['tm'])
    return _ffn_call(x, mod4, lw['norm2_g'], lw['wu'], lw['wgt'], lw['ffn_conv_w'], lw['ffn_conv_b'],
                     lw['wd'], lw['final_g'], cfg['tm'], cfg['chunk'], final_norm)


def _seq_tables(l, c):
    half = l // 2
    b16 = lambda t: tuple(a.astype(BF16) for a in t)
    ce, se = _trig_tables(half, 1, 0, 2, 0, 2 * l)
    co, so = _trig_tables(half, 1, 0, 2, 1, 2 * l)
    cot, sot = _trig_tables(half, 2, 1, 1, 0, 2 * l)
    fce, fse = _trig_tables(half, 1, 0, 2, 0, l)
    fco, fso = _trig_tables(half, 1, 0, 2, 1, l)
    scale = 1.0 / math.sqrt(l * (c // FNET_GROUPS))
    deltas = jnp.linspace(math.log(HYENA_DECAY_TARGET) / HYENA_SLOW_DECAY,
                          math.log(HYENA_DECAY_TARGET) / HYENA_FAST_DECAY, c, dtype=F32)[None, :]
    feats = _hyena_features(l, 128)
    return {
        'hy_fwd': b16((ce, co, se, so)), 'hy_inv': b16((ce, se, cot, sot)),
        'fn': b16((fce * scale, fco * scale, -fse * scale, -fso * scale)),
        'feats': jnp.concatenate([feats[0::2], feats[1::2]], axis=0), 'deltas': deltas,
    }


def kernel(x, c, ctx, c_ctx, w_mod, b_mod, norm1_g, norm2_g, w_in, pool_w, pool_scale, hyena_conv_w,
           hyena_conv_b, hyena_filt_w1, hyena_filt_b1, hyena_filt_w2, hyena_filt_b2, hyena_filt_w3,
           hyena_filt_b3, hyena_freq, hyena_skip, na_rpb, w_branch, w_out, ffn_w_up, ffn_conv_w, ffn_conv_b,
           ffn_w_down, final_norm_g):
    batch, seq, d = x.shape
    lc = ctx.shape[1]
    depth = w_mod.shape[0]
    m = d // N_BRANCH
    f = ffn_w_down.shape[1]
    rows = seq // GRID_W
    kh = min(NA_WIN_H, rows)

    cvec = jnp.concatenate([c, c_ctx[None], jnp.zeros((16 - batch - 1, d), F32)], axis=0)
    mod = _mod_call(cvec, w_mod, b_mod)
    mod_x = mod[:, :batch].reshape(depth, batch, 6, 1, d)
    mod_c = jnp.broadcast_to(mod[:, batch:batch + 1], (depth, batch, 6 * d)).reshape(depth, batch, 6, 1, d)

    gsz = m // FNET_GROUPS
    cc, ss = _trig_tables(gsz, 1, 0, 1, 0, gsz)
    eye = jnp.eye(FNET_GROUPS, dtype=F32)
    cs = jnp.concatenate([jnp.kron(eye, cc), jnp.kron(eye, ss)], axis=1).astype(BF16)

    tabs_x = _seq_tables(seq, m)
    tabs_c = _seq_tables(lc, m)
    cfg_x = dict(c=m, tm=min(1024, seq), rows=256, nb=2, nb_fwd=min(4, batch), tseq=min(512, seq // 2),
                 chunk=256, group=8)
    cfg_c = dict(c=m, tm=lc, rows=lc, nb=2, nb_fwd=min(4, batch), tseq=lc // 2, chunk=256)

    pool_off, fnet_off, hy_off, na_off = 0, m, 2 * m, 5 * m
    gate_off = 8 * m

    for l in range(depth):
        last = l == depth - 1
        wi = w_in[l]
        lw = {
            'norm1_g': norm1_g[l][None], 'norm2_g': norm2_g[l][None], 'final_g': final_norm_g[None],
            'wa': jnp.concatenate([wi[:, pool_off:pool_off + m], wi[:, hy_off:hy_off + 3 * m]], axis=1).astype(BF16),
            'wb': jnp.concatenate([wi[:, fnet_off:fnet_off + m], wi[:, na_off:na_off + 3 * m]], axis=1).astype(BF16),
            'cs': cs,
            'wg': wi[:, gate_off:].astype(BF16),
            'pool_blk': jax.scipy.linalg.block_diag(*[pool_w[l, gi] for gi in range(pool_w.shape[1])]).astype(BF16),
            'pool_scale': pool_scale[l][None],
            'hyena_conv_w': hyena_conv_w[l], 'hyena_conv_b': hyena_conv_b[l][None],
            'fw1': _pad2(hyena_filt_w1[l], 128, 128), 'fb1': _pad2(hyena_filt_b1[l][None], 1, 128),
            'fw2': _pad2(hyena_filt_w2[l], 128, 128), 'fb2': _pad2(hyena_filt_b2[l][None], 1, 128),
            'fw3': _pad2(hyena_filt_w3[l], 128, hyena_filt_w3.shape[2]), 'fb3': hyena_filt_b3[l][None],
            'freq': _pad2(hyena_freq[l][None], 1, 128),
            'skip': hyena_skip[l][:, None, :],
            'w_branch': w_branch[l].astype(BF16), 'w_out': w_out[l].astype(BF16),
            'wu': ffn_w_up[l][:, :f].astype(BF16), 'wgt': ffn_w_up[l][:, f:].astype(BF16),
            'ffn_conv_w': ffn_conv_w[l], 'ffn_conv_b': ffn_conv_b[l][None],
            'wd': ffn_w_down[l].astype(BF16),
        }
        bias = _attention_bias(na_rpb[l], rows, kh)

        proj_c = _mixer_inputs(ctx, mod_c[l], lw, cfg_c['tm'])
        proj_x = _mixer_inputs(x, mod_x[l], lw, cfg_x['tm'])
        y_att = _natt_call(proj_x[2], proj_c[2], bias, m, cfg_x['group'])
        x = _stream_layer(x, mod_x[l], lw, tabs_x, proj_x, y_att, cfg_x, last)
        if not last:
            ctx = _stream_layer(ctx, mod_c[l], lw, tabs_c, proj_c, _catt_call(proj_c[2], m), cfg_c, False)
    return x
```

```python
import functools
import math

import numpy as np
import jax
import jax.numpy as jnp
from jax import lax
from jax.experimental import pallas as pl
from jax.experimental.pallas import tpu as pltpu

F32 = jnp.float32
BF16 = jnp.bfloat16

GRID_W = 64
N_BRANCH = 4
POOL_WINDOWS = (2, 4, 8, 16)
FNET_GROUPS = 4
HYENA_ORDER = 2
HYENA_BANDS = 16
HYENA_DECAY_TARGET = 1e-2
HYENA_FAST_DECAY = 0.3
HYENA_SLOW_DECAY = 1.5
HYENA_DECAY_SHIFT = 0.05
NA_HEAD_DIM = 64
NA_WIN_H = 8
NA_WIN_W = 16
EPS = 1e-6
NEG_INF = -1e30

RESIDUES = 4
SQRT_HALF = math.sqrt(0.5)
LANES = 128
HALO = 16
PAD = 8
VMEM_LIMIT = 56 * 1024 * 1024


def _params(sem):
    return pltpu.CompilerParams(dimension_semantics=sem, vmem_limit_bytes=VMEM_LIMIT)


def _dot(a, b):
    return jnp.dot(a, b, preferred_element_type=F32)


def _dot_t(a, b):
    return lax.dot_general(a, b, (((1,), (1,)), ((), ())), preferred_element_type=F32)


def _split_bf16(a):
    hi = a.astype(BF16)
    lo = (a - hi.astype(F32)).astype(BF16)
    return hi, lo


def _mod_kernel(c_ref, w_ref, b_ref, o_ref):
    a = c_ref[...]
    a = a * jax.nn.sigmoid(a)
    a_hi, a_lo = _split_bf16(a)
    w_hi, w_lo = _split_bf16(w_ref[0])
    o_ref[0] = _dot(a_hi, w_hi) + _dot(a_lo, w_hi) + _dot(a_hi, w_lo) + b_ref[0]


def _mod_call(cvec, w_mod, b_mod):
    depth, d, n = w_mod.shape
    rows = cvec.shape[0]
    tn = 1024
    return pl.pallas_call(
        _mod_kernel,
        grid=(depth, n // tn),
        in_specs=[
            pl.BlockSpec((rows, d), lambda l, j: (0, 0)),
            pl.BlockSpec((1, d, tn), lambda l, j: (l, 0, j)),
            pl.BlockSpec((1, 1, tn), lambda l, j: (l, 0, j)),
        ],
        out_specs=pl.BlockSpec((1, rows, tn), lambda l, j: (l, 0, j)),
        out_shape=jax.ShapeDtypeStruct((depth, rows, n), F32),
        compiler_params=_params(("arbitrary", "arbitrary")),
        name="mod",
    )(cvec, w_mod, b_mod.reshape(depth, 1, n))


def _ada_norm(x, g, shift, scale):
    y = x * lax.rsqrt(jnp.mean(x * x, axis=-1, keepdims=True) + EPS)
    return (y * g) * (1.0 + scale) + shift


def _deinterleave(y, scr_ref, ways):
    n = y.shape[0]
    parts = [[] for _ in range(ways)]
    for cb in range(y.shape[1] // LANES):
        scr_ref[cb] = y[:, cb * LANES:(cb + 1) * LANES]
        for k in range(ways):
            parts[k].append(scr_ref[cb, pl.ds(k, n // ways, stride=ways), :])
    return [jnp.concatenate(p, axis=1) for p in parts]


def _interleave(parts, scr_ref):
    ways = len(parts)
    n = parts[0].shape[0]
    cols = []
    for cb in range(parts[0].shape[1] // LANES):
        for k in range(ways):
            scr_ref[cb, pl.ds(k, n, stride=ways), :] = parts[k][:, cb * LANES:(cb + 1) * LANES]
        cols.append(scr_ref[cb])
    return jnp.concatenate(cols, axis=1)


def _proj_kernel(x_ref, sh_ref, sc_ref, g_ref, wa_ref, wb_ref, cs_ref, oa_ref, oh_ref, ob_ref, of_ref, scr_ref):
    m = cs_ref.shape[0]
    h = _ada_norm(x_ref[0], g_ref[...], sh_ref[0, 0], sc_ref[0, 0]).astype(BF16)
    pa = _dot(h, wa_ref[...])
    oa_ref[0] = pa[:, :m]
    oh_ref[0] = pa[:, m:].astype(BF16)
    pb = _dot(h, wb_ref[...])
    ob_ref[0] = pb[:, m:].astype(BF16)
    ev, od = _deinterleave(_dot(pb[:, :m].astype(BF16), cs_ref[...]), scr_ref, 2)
    of_ref[0, 0] = ev.astype(BF16)
    of_ref[0, 1] = od.astype(BF16)


def _proj_call(x, mod4, g, wa, wb, cs, tm):
    b, l, d = x.shape
    m = cs.shape[0]
    na, nb = wa.shape[1], wb.shape[1]
    const = lambda bi, i: (0, 0)
    return pl.pallas_call(
        _proj_kernel,
        grid=(b, l // tm),
        in_specs=[
            pl.BlockSpec((1, tm, d), lambda bi, i: (bi, i, 0)),
            pl.BlockSpec((1, 1, 1, d), lambda bi, i: (bi, 0, 0, 0)),
            pl.BlockSpec((1, 1, 1, d), lambda bi, i: (bi, 1, 0, 0)),
            pl.BlockSpec((1, d), const),
            pl.BlockSpec((d, na), const),
            pl.BlockSpec((d, nb), const),
            pl.BlockSpec((m, 2 * m), const),
        ],
        out_specs=[
            pl.BlockSpec((1, tm, m), lambda bi, i: (bi, i, 0)),
            pl.BlockSpec((1, tm, na - m), lambda bi, i: (bi, i, 0)),
            pl.BlockSpec((1, tm, nb - m), lambda bi, i: (bi, i, 0)),
            pl.BlockSpec((1, 2, tm // 2, 2 * m), lambda bi, i: (bi, 0, i, 0)),
        ],
        out_shape=[
            jax.ShapeDtypeStruct((b, l, m), F32),
            jax.ShapeDtypeStruct((b, l, na - m), BF16),
            jax.ShapeDtypeStruct((b, l, nb - m), BF16),
            jax.ShapeDtypeStruct((b, 2, l // 2, 2 * m), BF16),
        ],
        scratch_shapes=[pltpu.VMEM((2 * m // LANES, tm, LANES), F32)],
        compiler_params=_params(("parallel", "parallel")),
        name="proj",
    )(x, mod4, mod4, g, wa, wb, cs)


def _fill_padded(src_ref, pad_ref, l, rows):
    c = pad_ref.shape[1]
    pad_ref[0:PAD, :] = jnp.zeros((PAD, c), F32)
    pad_ref[l + PAD:l + 2 * PAD, :] = jnp.zeros((PAD, c), F32)

    def copy(i, carry):
        r = pl.multiple_of(i * rows, rows)
        pad_ref[pl.ds(r + PAD, rows), :] = src_ref[0, pl.ds(r, rows), :].astype(F32)
        return carry

    lax.fori_loop(0, l // rows, copy, 0)


def _pool_kernel(u_ref, w_ref, s_ref, o_ref, pad_ref, *, rows):
    l, c = u_ref.shape[1], u_ref.shape[2]
    gw = c // len(POOL_WINDOWS)
    _fill_padded(u_ref, pad_ref, l, rows)
    n = rows + 2 * PAD
    lane = lax.broadcasted_iota(jnp.int32, (n, c), 1)
    row = lax.broadcasted_iota(jnp.int32, (n, c), 0)
    grp = lax.shift_right_logical(lane, int(math.log2(gw)))
    half = jnp.where(grp == 0, 1, jnp.where(grp == 1, 2, jnp.where(grp == 2, 4, 8)))
    w = w_ref[...]
    scale = s_ref[...]

    def body(i, carry):
        r = pl.multiple_of(i * rows, rows)
        win = pad_ref[pl.ds(r, n), :]
        a2 = win + pltpu.roll(win, 1, 0)
        w4 = pltpu.roll(a2, 1, 0) + pltpu.roll(a2, n - 1, 0)
        w8 = pltpu.roll(w4, 2, 0) + pltpu.roll(w4, n - 2, 0)
        w16 = pltpu.roll(w8, 4, 0) + pltpu.roll(w8, n - 4, 0)
        s = jnp.where(grp == 0, a2, jnp.where(grp == 1, w4, jnp.where(grp == 2, w8, w16)))
        t = row + (r - PAD)
        cnt = (jnp.minimum(t + half, l) - jnp.maximum(t - half, 0)).astype(F32)
        y = (s / jnp.maximum(cnt, 1.0) - win)[PAD:PAD + rows]
        o_ref[0, pl.ds(r, rows), :] = (_dot(y.astype(BF16), w) * scale).astype(BF16)
        return carry

    lax.fori_loop(0, l // rows, body, 0)


def _pool_call(oa, w_blk, scale, rows):
    b, l, _ = oa.shape
    c = w_blk.shape[0]
    return pl.pallas_call(
        functools.partial(_pool_kernel, rows=rows),
        grid=(b,),
        in_specs=[
            pl.BlockSpec((1, l, c), lambda bi: (bi, 0, 0)),
            pl.BlockSpec((c, c), lambda bi: (0, 0)),
            pl.BlockSpec((1, c), lambda bi: (0, 0)),
        ],
        out_specs=pl.BlockSpec((1, l, c), lambda bi: (bi, 0, 0)),
        out_shape=jax.ShapeDtypeStruct((b, l, c), BF16),
        scratch_shapes=[pltpu.VMEM((l + 2 * PAD, c), F32)],
        compiler_params=_params(("parallel",)),
        name="pool",
    )(oa, w_blk, scale)


def _dwconv_kernel(u_ref, w_ref, b_ref, o_ref, pad_ref, scr_ref, *, rows):
    l = u_ref.shape[1]
    _fill_padded(u_ref, pad_ref, l, rows)
    n = rows + 2 * PAD
    sub = rows // RESIDUES
    w = w_ref[...]
    bias = b_ref[...]

    def body(i, carry):
        r = pl.multiple_of(i * rows, rows)
        rs = pl.multiple_of(i * sub, sub)
        win = pad_ref[pl.ds(r, n), :]
        y = pltpu.roll(win, 1, 0) * w[0:1] + win * w[1:2] + pltpu.roll(win, n - 1, 0) * w[2:3] + bias
        for k, part in enumerate(_deinterleave(y[PAD:PAD + rows], scr_ref, RESIDUES)):
            o_ref[0, k, pl.ds(rs, sub), :] = part.astype(BF16)
        return carry

    lax.fori_loop(0, l // rows, body, 0)


def _dwconv_call(u, w, bias, c, rows):
    b, l, _ = u.shape
    nblk = w.shape[1] // c
    return pl.pallas_call(
        functools.partial(_dwconv_kernel, rows=rows),
        grid=(b, nblk),
        in_specs=[
            pl.BlockSpec((1, l, c), lambda bi, j: (bi, 0, j)),
            pl.BlockSpec((w.shape[0], c), lambda bi, j: (0, j)),
            pl.BlockSpec((1, c), lambda bi, j: (0, j)),
        ],
        out_specs=pl.BlockSpec((1, RESIDUES, l // RESIDUES, c), lambda bi, j: (bi, 0, 0, j)),
        out_shape=jax.ShapeDtypeStruct((b, RESIDUES, l // RESIDUES, nblk * c), BF16),
        scratch_shapes=[pltpu.VMEM((l + 2 * PAD, c), F32), pltpu.VMEM((c // LANES, rows, LANES), F32)],
        compiler_params=_params(("parallel", "parallel")),
        name="dwconv",
    )(u, w, bias)


def _filt_kernel(f_ref, w1_ref, b1_ref, w2_ref, b2_ref, w3_ref, b3_ref, fr_ref, dl_ref,
                 hb_ref, row0_ref, asum_ref):
    hp = lax.Precision.HIGHEST
    feats = f_ref[...]
    freq = fr_ref[...]
    h = jnp.sin(freq * (jnp.dot(feats, w1_ref[...], precision=hp, preferred_element_type=F32) + b1_ref[...]))
    h = jnp.sin(freq * (jnp.dot(h, w2_ref[...], precision=hp, preferred_element_type=F32) + b2_ref[...]))
    h = jnp.dot(h, w3_ref[...], precision=hp, preferred_element_type=F32) + b3_ref[...]
    t = feats[:, 0:1]
    win = jnp.exp(-t * jnp.abs(dl_ref[...])) + HYENA_DECAY_SHIFT
    h = h * jnp.concatenate([win] * (h.shape[1] // win.shape[1]), axis=1)
    hb = h.astype(BF16)
    hb_ref[...] = hb
    part = jnp.sum(jnp.abs(h), axis=0, keepdims=True)

    @pl.when(pl.program_id(0) == 0)
    def _():
        asum_ref[...] = jnp.zeros_like(asum_ref)
        row0_ref[...] = hb[0:16].astype(F32)[0:8]

    asum_ref[...] += part


def _filt_call(feats, w1, b1, w2, b2, w3, b3, freq, deltas, rows):
    l, fd = feats.shape
    hd = w2.shape[0]
    n = w3.shape[1]
    c = deltas.shape[1]
    const = lambda i: (0, 0)
    return pl.pallas_call(
        _filt_kernel,
        grid=(l // rows,),
        in_specs=[
            pl.BlockSpec((rows, fd), lambda i: (i, 0)),
            pl.BlockSpec((fd, hd), const), pl.BlockSpec((1, hd), const),
            pl.BlockSpec((hd, hd), const), pl.BlockSpec((1, hd), const),
            pl.BlockSpec((hd, n), const), pl.BlockSpec((1, n), const),
            pl.BlockSpec((1, hd), const), pl.BlockSpec((1, c), const),
        ],
        out_specs=[
            pl.BlockSpec((rows, n), lambda i: (i, 0)),
            pl.BlockSpec((8, n), const),
            pl.BlockSpec((1, n), const),
        ],
        out_shape=[
            jax.ShapeDtypeStruct((l, n), BF16),
            jax.ShapeDtypeStruct((8, n), F32),
            jax.ShapeDtypeStruct((1, n), F32),
        ],
        compiler_params=_params(("arbitrary",)),
        name="hyena_filt",
    )(feats, w1, b1, w2, b2, w3, b3, freq, deltas)


def _alt_signs(n):
    t = lax.broadcasted_iota(jnp.int32, (16, n), 1)
    return jnp.where((t & 1) == 0, 1.0, -1.0).astype(BF16)


def _mat_specs(tm, cols, n):
    return [pl.BlockSpec((tm, cols), lambda g, i: (i, 0)) for _ in range(n)]


def _quad_transform(mat_refs, parts):
    a = [_dot(mat_refs[2 * r][...], parts[r]) for r in range(RESIDUES)]
    b = [_dot(mat_refs[2 * r + 1][...], parts[r]) for r in range(RESIDUES)]
    ea, fa, ga, ha = a[0] + a[2], a[0] - a[2], a[1] + a[3], a[1] - a[3]
    eb, fb, gb, hb = b[0] + b[2], b[0] - b[2], b[1] + b[3], b[1] - b[3]
    zr = (ea + ga, fa + hb, fa - hb, ea - ga)
    zs = (eb + gb, ha - fb, ha + fb, gb - eb)
    return zr, zs


def _quad_special(alt):
    c1, c3 = SQRT_HALF * alt[1], SQRT_HALF * alt[3]
    return (alt[0] + c1 - c3, c1 + alt[2] + c3), (alt[0] - c1 + c3, c1 - alt[2] + c3)


def _kspec_kernel(*refs):
    mat_refs, (h_ref, row0_ref, asum_ref, kk_ref, km_ref) = refs[:8], refs[8:]
    i = pl.program_id(1)
    tm, q = mat_refs[0].shape
    c = kk_ref.shape[3]
    l = RESIDUES * q
    parts = [h_ref[r * q:(r + 1) * q, :] for r in range(RESIDUES)]
    zr, zs = _quad_transform(mat_refs, parts)
    asum = asum_ref[...]
    inv = 1.0 / (asum[:, :c] + asum[:, c:] + EPS)
    hb0 = row0_ref[0:1, c:]
    row = lax.broadcasted_iota(jnp.int32, (tm, c), 0) + i * tm
    wj = jnp.where(row == 0, 0.5 / l, 1.0 / l) * inv
    for f in range(4):
        kk_ref[0, 2 * f] = (zr[f][:, :c] + zr[f][:, c:] - hb0) * wj
        kk_ref[0, 2 * f + 1] = (zs[f][:, :c] - zs[f][:, c:]) * wj

    @pl.when(i == 0)
    def _():
        sg = _alt_signs(q)
        alt = [_dot(sg, p)[0:8] for p in parts]
        for f, (xr, xs) in enumerate(_quad_special(alt)):
            km_ref[0, :, 2 * f * c:(2 * f + 1) * c] = (xr[:, :c] + xr[:, c:] - hb0) * inv * (1.0 / l)
            km_ref[0, :, (2 * f + 1) * c:(2 * f + 2) * c] = (xs[:, :c] - xs[:, c:]) * inv * (1.0 / l)


def _kspec_call(mats, h, row0, asum, c, tm):
    q = mats[0].shape[0]
    l = RESIDUES * q
    orders = h.shape[1] // (2 * c)
    return pl.pallas_call(
        _kspec_kernel,
        grid=(orders, q // tm),
        in_specs=_mat_specs(tm, q, 8) + [
            pl.BlockSpec((l, 2 * c), lambda o, i: (0, o)),
            pl.BlockSpec((8, 2 * c), lambda o, i: (0, o)),
            pl.BlockSpec((1, 2 * c), lambda o, i: (0, o)),
        ],
        out_specs=[
            pl.BlockSpec((1, 8, tm, c), lambda o, i: (o, 0, i, 0)),
            pl.BlockSpec((1, 8, 4 * c), lambda o, i: (o, 0, 0)),
        ],
        out_shape=[
            jax.ShapeDtypeStruct((orders, 8, q, c), F32),
            jax.ShapeDtypeStruct((orders, 8, 4 * c), F32),
        ],
        compiler_params=_params(("arbitrary", "arbitrary")),
        name="hyena_kspec",
    )(*mats, h, row0, asum)


def _cmul(zr, zs, kr, ks):
    return zr * kr - zs * ks, zr * ks + zs * kr


def _hfwd_kernel(*refs):
    mat_refs, (z_ref, kk_ref, km_ref, spec_ref, ym_ref) = refs[:8], refs[8:]
    nb = z_ref.shape[0]
    c = z_ref.shape[3]
    for bb in range(nb):
        zr, zs = _quad_transform(mat_refs, [z_ref[bb, r] for r in range(RESIDUES)])
        yr, ys = zip(*[_cmul(zr[f], zs[f], kk_ref[0, 2 * f], kk_ref[0, 2 * f + 1]) for f in range(4)])
        a, b, u1, u2 = yr[0] + yr[3], yr[0] - yr[3], yr[1] + yr[2], ys[1] + ys[2]
        cc, dd, d1, d2 = ys[0] - ys[3], ys[0] + ys[3], yr[1] - yr[2], ys[2] - ys[1]
        for k, v in enumerate((a + u1, cc + d2, b + u2, dd + d1, a - u1, cc - d2, b - u2, dd - d1)):
            spec_ref[bb, k] = v.astype(BF16)

    @pl.when(pl.program_id(1) == 0)
    def _():
        sg = _alt_signs(z_ref.shape[2])
        for bb in range(nb):
            alt = [_dot(sg, z_ref[bb, r])[0:8] for r in range(RESIDUES)]
            (zq_r, zq_s), (z3_r, z3_s) = _quad_special(alt)
            yq_r, yq_s = _cmul(zq_r, zq_s, km_ref[0, :, 0:c], km_ref[0, :, c:2 * c])
            y3_r, y3_s = _cmul(z3_r, z3_s, km_ref[0, :, 2 * c:3 * c], km_ref[0, :, 3 * c:4 * c])
            ym_ref[bb, :, 0:c] = yq_r + y3_r
            ym_ref[bb, :, c:2 * c] = SQRT_HALF * (yq_r + yq_s - y3_r + y3_s)
            ym_ref[bb, :, 2 * c:3 * c] = yq_s - y3_s
            ym_ref[bb, :, 3 * c:4 * c] = SQRT_HALF * (yq_s - yq_r + y3_r + y3_s)


def _hfwd_call(mats, zb, kk, km, order, c, nb, tm):
    b, _, q, _ = zb.shape
    return pl.pallas_call(
        _hfwd_kernel,
        grid=(b // nb, q // tm),
        in_specs=_mat_specs(tm, q, 8) + [
            pl.BlockSpec((nb, RESIDUES, q, c), lambda g, i: (g, 0, 0, 0)),
            pl.BlockSpec((1, 8, tm, c), lambda g, i: (order, 0, i, 0)),
            pl.BlockSpec((1, 8, 4 * c), lambda g, i: (order, 0, 0)),
        ],
        out_specs=[
            pl.BlockSpec((nb, 8, tm, c), lambda g, i: (g, 0, i, 0)),
            pl.BlockSpec((nb, 8, 4 * c), lambda g, i: (g, 0, 0)),
        ],
        out_shape=[
            jax.ShapeDtypeStruct((b, 8, q, c), BF16),
            jax.ShapeDtypeStruct((b, 8, 4 * c), F32),
        ],
        compiler_params=_params(("arbitrary", "arbitrary")),
        name="hyena_fwd",
    )(*mats, zb, kk, km)


def _hinv_kernel(*refs, natural_out):
    mat_refs, (spec_ref, ym_ref, z_ref, gate_ref, skip_ref), out_refs = refs[:8], refs[8:13], refs[13:]
    nb, _, tm, c = z_ref.shape
    row = lax.broadcasted_iota(jnp.int32, (tm, c), 0)
    even_row = (row & 1) == 0
    skip = skip_ref[0]
    for bb in range(nb):
        outs = []
        for r in range(RESIDUES):
            m = ym_ref[bb, 0:1, r * c:(r + 1) * c]
            y = (_dot(mat_refs[2 * r][...], spec_ref[bb, 2 * r]) + _dot(mat_refs[2 * r + 1][...], spec_ref[bb, 2 * r + 1])
                 + jnp.where(even_row, m, -m))
            outs.append(gate_ref[bb, r].astype(F32) * (y + skip * z_ref[bb, r].astype(F32)))
        if natural_out:
            o_ref, scr_ref = out_refs
            o_ref[bb] = _interleave(outs, scr_ref).astype(BF16)
        else:
            (o_ref,) = out_refs
            for r in range(RESIDUES):
                o_ref[bb, r] = outs[r].astype(BF16)


def _hinv_call(mats, spec, ym, zf, zcol, uc, gcol, skip, order, natural_out, c, nb, tm):
    b, _, q, _ = spec.shape
    if natural_out:
        out_specs = pl.BlockSpec((nb, RESIDUES * tm, c), lambda g, i: (g, i, 0))
        out_shape = jax.ShapeDtypeStruct((b, RESIDUES * q, c), BF16)
        scratch = [pltpu.VMEM((c // LANES, RESIDUES * tm, LANES), F32)]
    else:
        out_specs = pl.BlockSpec((nb, RESIDUES, tm, c), lambda g, i: (g, 0, i, 0))
        out_shape = jax.ShapeDtypeStruct((b, RESIDUES, q, c), BF16)
        scratch = []
    return pl.pallas_call(
        functools.partial(_hinv_kernel, natural_out=natural_out),
        grid=(b // nb, q // tm),
        in_specs=_mat_specs(tm, q, 8) + [
            pl.BlockSpec((nb, 8, q, c), lambda g, i: (g, 0, 0, 0)),
            pl.BlockSpec((nb, 8, 4 * c), lambda g, i: (g, 0, 0)),
            pl.BlockSpec((nb, RESIDUES, tm, c), lambda g, i: (g, 0, i, zcol)),
            pl.BlockSpec((nb, RESIDUES, tm, c), lambda g, i: (g, 0, i, gcol)),
            pl.BlockSpec((1, 1, c), lambda g, i: (order, 0, 0)),
        ],
        out_specs=out_specs,
        out_shape=out_shape,
        scratch_shapes=scratch,
        compiler_params=_params(("arbitrary", "arbitrary")),
        name="hyena_inv",
    )(*mats, spec, ym, zf, uc, skip)


def _fseq_kernel(ce_ref, co_ref, se_ref, so_ref, u_ref, o_ref):
    nb = u_ref.shape[0]
    c = o_ref.shape[3]
    ce, co, se, so = ce_ref[...], co_ref[...], se_ref[...], so_ref[...]
    for bb in range(nb):
        ae = _dot(ce, u_ref[bb, 0, :, 0:c]) + _dot(se, u_ref[bb, 0, :, c:2 * c])
        ao = _dot(co, u_ref[bb, 1, :, 0:c]) + _dot(so, u_ref[bb, 1, :, c:2 * c])
        o_ref[bb, 0] = (ae + ao).astype(BF16)
        o_ref[bb, 1] = (ae - ao).astype(BF16)


def _fseq_call(mats, ucs, nb, tm):
    b, _, half, c2 = ucs.shape
    c = c2 // 2
    out = pl.pallas_call(
        _fseq_kernel,
        grid=(b // nb, half // tm),
        in_specs=_mat_specs(tm, half, 4) + [pl.BlockSpec((nb, 2, half, c2), lambda g, i: (g, 0, 0, 0))],
        out_specs=pl.BlockSpec((nb, 2, tm, c), lambda g, i: (g, 0, i, 0)),
        out_shape=jax.ShapeDtypeStruct((b, 2, half, c), BF16),
        compiler_params=_params(("arbitrary", "arbitrary")),
        name="fnet_seq",
    )(*mats, ucs)
    return out.reshape(b, 2 * half, c)


def _stack_heads(q, heads):
    lane = lax.broadcasted_iota(jnp.int32, q.shape, 1)
    zero = jnp.zeros_like(q)
    return jnp.concatenate([jnp.where(lax.shift_right_logical(lane, int(math.log2(NA_HEAD_DIM))) == h, q, zero) for h in range(heads)], axis=0)


def _merge_heads(o, heads, n):
    lane = lax.broadcasted_iota(jnp.int32, (n, o.shape[1]), 1)
    out = jnp.zeros((n, o.shape[1]), F32)
    for h in range(heads):
        out = out + jnp.where(lax.shift_right_logical(lane, int(math.log2(NA_HEAD_DIM))) == h, o[h * n:(h + 1) * n], 0.0)
    return out


def _natt_kernel(q_ref, k_ref, v_ref, kc_ref, vc_ref, bias_ref, o_ref, *, rows, kh, group):
    w = GRID_W
    heads = q_ref.shape[2] // NA_HEAD_DIM
    base = pl.program_id(1) * group
    lo = kh // 2
    kc, vc = kc_ref[0], vc_ref[0]
    scale = jnp.asarray(NA_HEAD_DIM ** -0.5, BF16)

    def body(j, carry):
        r = base + j
        r0 = jnp.clip(r - lo, 0, rows - kh)
        start = pl.multiple_of(r0 * w, w)
        cls = jnp.minimum(r, lo) + jnp.maximum(r - (rows - kh + lo), 0)
        q4 = _stack_heads(q_ref[0, pl.ds(pl.multiple_of(j * w, w), w), :] * scale, heads)
        ks = k_ref[0, pl.ds(start, kh * w), :]
        vs = v_ref[0, pl.ds(start, kh * w), :]
        s_nb = _dot_t(q4, ks) + bias_ref[cls]
        s_cx = _dot_t(q4, kc)
        m = jnp.maximum(jnp.max(s_nb, axis=-1, keepdims=True), jnp.max(s_cx, axis=-1, keepdims=True))
        p_nb = jnp.exp(s_nb - m)
        p_cx = jnp.exp(s_cx - m)
        den = jnp.sum(p_nb, axis=-1, keepdims=True) + jnp.sum(p_cx, axis=-1, keepdims=True)
        o4 = (_dot(p_nb.astype(BF16), vs) + _dot(p_cx.astype(BF16), vc)) / den
        o_ref[0, pl.ds(pl.multiple_of(j * w, w), w), :] = _merge_heads(o4, heads, w).astype(BF16)
        return carry

    lax.fori_loop(0, group, body, 0, unroll=4)


def _natt_call(ob, obc, bias, c, group):
    b, l, _ = ob.shape
    lc = obc.shape[1]
    w = GRID_W
    rows = l // w
    kh = min(NA_WIN_H, rows)
    return pl.pallas_call(
        functools.partial(_natt_kernel, rows=rows, kh=kh, group=group),
        grid=(b, rows // group),
        in_specs=[
            pl.BlockSpec((1, group * w, c), lambda bi, r: (bi, r, 0)),
            pl.BlockSpec((1, l, c), lambda bi, r: (bi, 0, 1)),
            pl.BlockSpec((1, l, c), lambda bi, r: (bi, 0, 2)),
            pl.BlockSpec((1, lc, c), lambda bi, r: (bi, 0, 1)),
            pl.BlockSpec((1, lc, c), lambda bi, r: (bi, 0, 2)),
            pl.BlockSpec(bias.shape, lambda bi, r: (0, 0, 0)),
        ],
        out_specs=pl.BlockSpec((1, group * w, c), lambda bi, r: (bi, r, 0)),
        out_shape=jax.ShapeDtypeStruct((b, l, c), BF16),
        compiler_params=_params(("parallel", "arbitrary")),
        name="nb_attn",
    )(ob, ob, ob, obc, obc, bias)


def _bias_kernel(r_ref, o_ref):
    w = GRID_W
    shift = int(math.log2(w))
    n = o_ref.shape[1]
    r = r_ref[...]
    hi = r.astype(BF16)
    mid = (r - hi.astype(F32)).astype(BF16)
    lo = (r - hi.astype(F32) - mid.astype(F32)).astype(BF16)
    dc = lax.broadcasted_iota(jnp.int32, (r.shape[1], n), 0)
    p = lax.broadcasted_iota(jnp.int32, (r.shape[1], n), 1)
    idx = jnp.clip((p & (w - 1)) - lax.shift_right_logical(p, shift), 1 - NA_WIN_W, NA_WIN_W - 1) + (NA_WIN_W - 1)
    onehot = jnp.where(dc == idx, 1.0, 0.0).astype(BF16)
    val = _dot(hi, onehot) + _dot(mid, onehot) + _dot(lo, onehot)
    po = lax.broadcasted_iota(jnp.int32, val.shape, 1)
    cq = lax.shift_right_logical(po, shift)
    ck = po & (w - 1)
    c0 = jnp.clip(cq - NA_WIN_W // 2, 0, w - NA_WIN_W)
    inside = jnp.logical_and(ck >= c0, ck < c0 + NA_WIN_W)
    o_ref[...] = jnp.where(inside, val, NEG_INF)


def _bias_call(rpb2d):
    rows, cols = rpb2d.shape
    n = GRID_W * GRID_W
    return pl.pallas_call(
        _bias_kernel,
        grid=(1,),
        in_specs=[pl.BlockSpec((rows, cols), lambda i: (0, 0))],
        out_specs=pl.BlockSpec((rows, n), lambda i: (0, 0)),
        out_shape=jax.ShapeDtypeStruct((rows, n), F32),
        compiler_params=_params(("arbitrary",)),
        name="rpb_table",
    )(rpb2d)


def _catt_kernel(q_ref, k_ref, v_ref, o_ref):
    n = q_ref.shape[1]
    heads = q_ref.shape[2] // NA_HEAD_DIM
    q = q_ref[0] * jnp.asarray(NA_HEAD_DIM ** -0.5, BF16)
    q4 = _stack_heads(q, heads)
    s = _dot_t(q4, k_ref[0])
    p = jnp.exp(s - jnp.max(s, axis=-1, keepdims=True))
    den = jnp.sum(p, axis=-1, keepdims=True)
    o4 = _dot(p.astype(BF16), v_ref[0]) / den
    o_ref[0] = _merge_heads(o4, heads, n).astype(BF16)


def _catt_call(obc, c):
    b, lc, _ = obc.shape
    return pl.pallas_call(
        _catt_kernel,
        grid=(b,),
        in_specs=[pl.BlockSpec((1, lc, c), lambda bi, j=j: (bi, 0, j)) for j in range(3)],
        out_specs=pl.BlockSpec((1, lc, c), lambda bi: (bi, 0, 0)),
        out_shape=jax.ShapeDtypeStruct((b, lc, c), BF16),
        compiler_params=_params(("parallel",)),
        name="ctx_attn",
    )(obc, obc, obc)


def _merge_kernel(x_ref, sh_ref, sc_ref, gt_ref, g_ref, y0_ref, y1_ref, y2_ref, y3_ref,
                  wg_ref, wb_ref, wo_ref, o_ref):
    d = x_ref.shape[2]
    x = x_ref[0]
    h = _ada_norm(x, g_ref[...], sh_ref[0, 0], sc_ref[0, 0]).astype(BF16)
    merged = None
    for bi, y_ref in enumerate((y0_ref, y1_ref, y2_ref, y3_ref)):
        gate = jax.nn.sigmoid(_dot(h, wg_ref[:, bi * d:(bi + 1) * d]))
        term = gate * _dot(y_ref[0].astype(BF16), wb_ref[bi])
        merged = term if merged is None else merged + term
    o_ref[0] = x + gt_ref[0, 0] * _dot(merged.astype(BF16), wo_ref[...])


def _merge_call(x, mod4, g, ys, wg, wb, wo, tm):
    b, l, d = x.shape
    c = ys[0].shape[2]
    const2 = lambda bi, i: (0, 0)
    mod_spec = lambda k: pl.BlockSpec((1, 1, 1, d), lambda bi, i: (bi, k, 0, 0))
    return pl.pallas_call(
        _merge_kernel,
        grid=(b, l // tm),
        in_specs=[
            pl.BlockSpec((1, tm, d), lambda bi, i: (bi, i, 0)),
            mod_spec(0), mod_spec(1), mod_spec(2),
            pl.BlockSpec((1, d), const2),
        ] + [pl.BlockSpec((1, tm, c), lambda bi, i: (bi, i, 0)) for _ in ys] + [
            pl.BlockSpec(wg.shape, const2, pipeline_mode=pl.Buffered(1)),
            pl.BlockSpec(wb.shape, lambda bi, i: (0, 0, 0), pipeline_mode=pl.Buffered(1)),
            pl.BlockSpec(wo.shape, const2, pipeline_mode=pl.Buffered(1)),
        ],
        out_specs=pl.BlockSpec((1, tm, d), lambda bi, i: (bi, i, 0)),
        out_shape=jax.ShapeDtypeStruct((b, l, d), F32),
        compiler_params=_params(("parallel", "parallel")),
        name="merge",
    )(x, mod4, mod4, mod4, g, *ys, wg, wb, wo)


def _ffn_kernel(x_ref, xp_ref, xn_ref, sh_ref, sc_ref, gt_ref, g_ref, wu_ref, wgt_ref, cw_ref, cb_ref,
                wd_ref, fg_ref, o_ref, a_ref, *, chunk, final_norm):
    i = pl.program_id(1)
    last = pl.num_programs(1) - 1
    tm = x_ref.shape[1]
    f = wu_ref.shape[1]
    n = tm + 2 * HALO
    g, sh, sc = g_ref[...], sh_ref[0, 0], sc_ref[0, 0]
    x = x_ref[0]
    h = _ada_norm(x, g, sh, sc).astype(BF16)
    hp = _ada_norm(xp_ref[0], g, sh, sc).astype(BF16)
    hn = _ada_norm(xn_ref[0], g, sh, sc).astype(BF16)
    hext = jnp.concatenate([hp, h, hn], axis=0)
    row = lax.broadcasted_iota(jnp.int32, (n, chunk), 0)
    valid = jnp.logical_and(jnp.logical_or(row >= HALO, i > 0), jnp.logical_or(row < HALO + tm, i < last))
    for j in range(f // chunk):
        cols = slice(j * chunk, (j + 1) * chunk)
        gp = jnp.where(valid, _dot(hext, wgt_ref[:, cols]), 0.0)
        cw = cw_ref[:, cols]
        gc = (pltpu.roll(gp, 1, 0) * cw[0:1] + gp * cw[1:2] + pltpu.roll(gp, n - 1, 0) * cw[2:3]
              + cb_ref[:, cols])[HALO:HALO + tm]
        u = _dot(h, wu_ref[:, cols])
        a_ref[:, cols] = (gc * jax.nn.sigmoid(gc) * u).astype(BF16)
    y = x + gt_ref[0, 0] * _dot(a_ref[...], wd_ref[...])
    if final_norm:
        y = y * lax.rsqrt(jnp.mean(y * y, axis=-1, keepdims=True) + EPS) * fg_ref[...]
    o_ref[0] = y


def _ffn_call(x, mod4, g, wu, wgt, cw, cb, wd, fg, tm, chunk, final_norm):
    b, l, d = x.shape
    f = wu.shape[1]
    hb = tm // HALO
    nblk = l // HALO
    const2 = lambda bi, i: (0, 0)
    mod_spec = lambda k: pl.BlockSpec((1, 1, 1, d), lambda bi, i: (bi, k, 0, 0))
    single = dict(pipeline_mode=pl.Buffered(1))
    return pl.pallas_call(
        functools.partial(_ffn_kernel, chunk=chunk, final_norm=final_norm),
        grid=(b, l // tm),
        in_specs=[
            pl.BlockSpec((1, tm, d), lambda bi, i: (bi, i, 0)),
            pl.BlockSpec((1, HALO, d), lambda bi, i: (bi, jnp.maximum(i * hb - 1, 0), 0)),
            pl.BlockSpec((1, HALO, d), lambda bi, i: (bi, jnp.minimum((i + 1) * hb, nblk - 1), 0)),
            mod_spec(3), mod_spec(4), mod_spec(5),
            pl.BlockSpec((1, d), const2),
            pl.BlockSpec((d, f), const2, **single),
            pl.BlockSpec((d, f), const2, **single),
            pl.BlockSpec((cw.shape[0], f), const2),
            pl.BlockSpec((1, f), const2),
            pl.BlockSpec((f, d), const2, **single),
            pl.BlockSpec((1, d), const2),
        ],
        out_specs=pl.BlockSpec((1, tm, d), lambda bi, i: (bi, i, 0)),
        out_shape=jax.ShapeDtypeStruct((b, l, d), F32),
        scratch_shapes=[pltpu.VMEM((tm, f), BF16)],
        compiler_params=_params(("parallel", "arbitrary")),
        name="ffn",
    )(x, x, x, mod4, mod4, mod4, g, wu, wgt, cw, cb, wd, fg)


def _trig_tables(n, row_mul, row_add, col_mul, col_add, period):
    a_sz = 64 if n % 64 == 0 else 1
    q = (jnp.arange(n, dtype=jnp.int32) * col_mul + col_add)[None, :]
    ra = (jnp.arange(n // a_sz, dtype=jnp.int32) * (a_sz * row_mul))[:, None]
    rb = (jnp.arange(a_sz, dtype=jnp.int32) * row_mul + row_add)[:, None]
    w = 2.0 * math.pi / period
    ang_a = ((ra * q) % period).astype(F32) * w
    ang_b = ((rb * q) % period).astype(F32) * w
    ca, sa = jnp.cos(ang_a)[:, None, :], jnp.sin(ang_a)[:, None, :]
    cb, sb = jnp.cos(ang_b)[None, :, :], jnp.sin(ang_b)[None, :, :]
    cos = (ca * cb - sa * sb).reshape(n, n)
    sin = (sa * cb + ca * sb).reshape(n, n)
    return cos, sin


def _pad2(a, rows, cols):
    return jnp.pad(a, ((0, rows - a.shape[0]), (0, cols - a.shape[1])))


def _hyena_features(l, pad_to):
    t = jnp.linspace(0.0, 1.0, l, dtype=F32)[:, None]
    bands = jnp.linspace(1e-4, HYENA_BANDS - 1, HYENA_BANDS, dtype=F32)[None, :]
    ang = (2.0 * math.pi / l) * jnp.arange(l, dtype=F32)[:, None] * bands
    feats = jnp.concatenate([t, jnp.cos(ang), -jnp.sin(ang)], axis=-1)
    return jnp.pad(feats, ((0, 0), (0, pad_to - feats.shape[1])))


def _bias_classes(rows, kh):
    r = np.arange(rows)
    r0 = np.clip(r - kh // 2, 0, rows - kh)
    off = r0 - r
    lo = kh // 2
    reps = list(range(lo)) + [lo] + list(range(rows - kh + lo + 1, rows))
    return [int(off[i]) for i in reps]


def _attention_bias(rpb, rows, kh):
    heads, nr, nc = rpb.shape
    w = GRID_W
    flat = rpb.astype(F32).reshape(heads * nr, nc)
    table = _bias_call(_pad2(flat, -(-heads * nr // 16) * 16, 128))
    table = table[:heads * nr].reshape(heads, nr, w, w)
    out = []
    for off in _bias_classes(rows, kh):
        first = off + NA_WIN_H - 1
        sl = table[:, first:first + kh]
        out.append(jnp.transpose(sl, (0, 2, 1, 3)).reshape(heads * w, kh * w))
    return jnp.stack(out)


def _mixer_inputs(x, mod4, lw, tm):
    return _proj_call(x, mod4, lw['norm1_g'], lw['wa'], lw['wb'], lw['cs'], tm)


def _hyena(oh, lw, tabs, c, nb_fwd, nb, tm, rows):
    fwd, inv = tabs['hy_fwd'], tabs['hy_inv']
    uc = _dwconv_call(oh, lw['hyena_conv_w'], lw['hyena_conv_b'], c, rows)
    h, row0, asum = _filt_call(tabs['feats'], lw['fw1'], lw['fb1'], lw['fw2'], lw['fb2'], lw['fw3'],
                               lw['fb3'], lw['freq'], tabs['deltas'], min(rows * 2, oh.shape[1]))
    kk, km = _kspec_call(fwd, h, row0, asum, c, tm)
    spec, ym = _hfwd_call(fwd, uc, kk, km, 0, c, nb_fwd, tm)
    z2 = _hinv_call(inv, spec, ym, uc, 0, uc, 1, lw['skip'], 0, False, c, nb, tm)
    spec, ym = _hfwd_call(fwd, z2, kk, km, 1, c, nb_fwd, tm)
    return _hinv_call(inv, spec, ym, z2, 0, uc, 2, lw['skip'], 1, True, c, nb, tm)


def _stream_layer(x, mod4, lw, tabs, proj, y_att, cfg, final_norm):
    c = cfg['c']
    oa, oh, ob, of = proj
    y_pool = _pool_call(oa, lw['pool_blk'], lw['pool_scale'], cfg['rows'])
    y_fnet = _fseq_call(tabs['fn'], of, cfg['nb'], cfg['tseq'])
    y_hyena = _hyena(oh, lw, tabs, c, cfg['nb'], cfg['nb'], cfg['thy'], cfg['rows'])
    x = _merge_call(x, mod4, lw['norm1_g'], (y_pool, y_fnet, y_hyena, y_att), lw['wg'], lw['w_branch'],
                    lw['w_out'], cfg['tm'])
    return _ffn_call(x, mod4, lw['norm2_g'], lw['wu'], lw['wgt'], lw['ffn_conv_w'], lw['ffn_conv_b'],
                     lw['wd'], lw['final_g'], cfg['tm'], cfg['chunk'], final_norm)


def _seq_tables(l, c):
    half, q = l // 2, l // RESIDUES
    b16 = lambda t: tuple(a.astype(BF16) for a in t)
    hy_fwd = sum((_trig_tables(q, 1, 0, RESIDUES, r, 2 * l) for r in range(RESIDUES)), ())
    hy_inv = sum((_trig_tables(q, RESIDUES, r, 1, 0, 2 * l) for r in range(RESIDUES)), ())
    fce, fse = _trig_tables(half, 1, 0, 2, 0, l)
    fco, fso = _trig_tables(half, 1, 0, 2, 1, l)
    scale = 1.0 / math.sqrt(l * (c // FNET_GROUPS))
    deltas = jnp.linspace(math.log(HYENA_DECAY_TARGET) / HYENA_SLOW_DECAY,
                          math.log(HYENA_DECAY_TARGET) / HYENA_FAST_DECAY, c, dtype=F32)[None, :]
    feats = _hyena_features(l, 128)
    return {
        'hy_fwd': b16(hy_fwd), 'hy_inv': b16(hy_inv),
        'fn': b16((fce * scale, fco * scale, -fse * scale, -fso * scale)),
        'feats': jnp.concatenate([feats[r::RESIDUES] for r in range(RESIDUES)], axis=0), 'deltas': deltas,
    }


def kernel(x, c, ctx, c_ctx, w_mod, b_mod, norm1_g, norm2_g, w_in, pool_w, pool_scale, hyena_conv_w,
           hyena_conv_b, hyena_filt_w1, hyena_filt_b1, hyena_filt_w2, hyena_filt_b2, hyena_filt_w3,
           hyena_filt_b3, hyena_freq, hyena_skip, na_rpb, w_branch, w_out, ffn_w_up, ffn_conv_w, ffn_conv_b,
           ffn_w_down, final_norm_g):
    batch, seq, d = x.shape
    lc = ctx.shape[1]
    depth = w_mod.shape[0]
    m = d // N_BRANCH
    f = ffn_w_down.shape[1]
    rows = seq // GRID_W
    kh = min(NA_WIN_H, rows)

    cvec = jnp.concatenate([c, c_ctx[None], jnp.zeros((16 - batch - 1, d), F32)], axis=0)
    mod = _mod_call(cvec, w_mod, b_mod)
    mod_x = mod[:, :batch].reshape(depth, batch, 6, 1, d)
    mod_c = jnp.broadcast_to(mod[:, batch:batch + 1], (depth, batch, 6 * d)).reshape(depth, batch, 6, 1, d)

    gsz = m // FNET_GROUPS
    cc, ss = _trig_tables(gsz, 1, 0, 1, 0, gsz)
    eye = jnp.eye(FNET_GROUPS, dtype=F32)
    cs = jnp.concatenate([jnp.kron(eye, cc), jnp.kron(eye, ss)], axis=1).astype(BF16)

    tabs_x = _seq_tables(seq, m)
    tabs_c = _seq_tables(lc, m)
    cfg_x = dict(c=m, tm=min(1024, seq), rows=256, nb=2, tseq=min(512, seq // 2),
                 thy=min(512, seq // RESIDUES), chunk=256, group=8)
    cfg_c = dict(c=m, tm=lc, rows=lc, nb=2, tseq=lc // 2, thy=lc // RESIDUES, chunk=256)

    pool_off, fnet_off, hy_off, na_off = 0, m, 2 * m, 5 * m
    gate_off = 8 * m

    for l in range(depth):
        last = l == depth - 1
        wi = w_in[l]
        lw = {
            'norm1_g': norm1_g[l][None], 'norm2_g': norm2_g[l][None], 'final_g': final_norm_g[None],
            'wa': jnp.concatenate([wi[:, pool_off:pool_off + m], wi[:, hy_off:hy_off + 3 * m]], axis=1).astype(BF16),
            'wb': jnp.concatenate([wi[:, fnet_off:fnet_off + m], wi[:, na_off:na_off + 3 * m]], axis=1).astype(BF16),
            'cs': cs,
            'wg': wi[:, gate_off:].astype(BF16),
            'pool_blk': jax.scipy.linalg.block_diag(*[pool_w[l, gi] for gi in range(pool_w.shape[1])]).astype(BF16),
            'pool_scale': pool_scale[l][None],
            'hyena_conv_w': hyena_conv_w[l], 'hyena_conv_b': hyena_conv_b[l][None],
            'fw1': _pad2(hyena_filt_w1[l], 128, 128), 'fb1': _pad2(hyena_filt_b1[l][None], 1, 128),
            'fw2': _pad2(hyena_filt_w2[l], 128, 128), 'fb2': _pad2(hyena_filt_b2[l][None], 1, 128),
            'fw3': _pad2(hyena_filt_w3[l], 128, hyena_filt_w3.shape[2]), 'fb3': hyena_filt_b3[l][None],
            'freq': _pad2(hyena_freq[l][None], 1, 128),
            'skip': hyena_skip[l][:, None, :],
            'w_branch': w_branch[l].astype(BF16), 'w_out': w_out[l].astype(BF16),
            'wu': ffn_w_up[l][:, :f].astype(BF16), 'wgt': ffn_w_up[l][:, f:].astype(BF16),
            'ffn_conv_w': ffn_conv_w[l], 'ffn_conv_b': ffn_conv_b[l][None],
            'wd': ffn_w_down[l].astype(BF16),
        }
        bias = _attention_bias(na_rpb[l], rows, kh)

        proj_c = _mixer_inputs(ctx, mod_c[l], lw, cfg_c['tm'])
        proj_x = _mixer_inputs(x, mod_x[l], lw, cfg_x['tm'])
        y_att = _natt_call(proj_x[2], proj_c[2], bias, m, cfg_x['group'])
        x = _stream_layer(x, mod_x[l], lw, tabs_x, proj_x, y_att, cfg_x, last)
        if not last:
            ctx = _stream_layer(ctx, mod_c[l], lw, tabs_c, proj_c, _catt_call(proj_c[2], m), cfg_c, False)
    return x
```

```python
import functools
import math

import numpy as np
import jax
import jax.numpy as jnp
from jax import lax
from jax.experimental import pallas as pl
from jax.experimental.pallas import tpu as pltpu

F32 = jnp.float32
BF16 = jnp.bfloat16

GRID_W = 64
N_BRANCH = 4
POOL_WINDOWS = (2, 4, 8, 16)
FNET_GROUPS = 4
HYENA_ORDER = 2
HYENA_BANDS = 16
HYENA_DECAY_TARGET = 1e-2
HYENA_FAST_DECAY = 0.3
HYENA_SLOW_DECAY = 1.5
HYENA_DECAY_SHIFT = 0.05
NA_HEAD_DIM = 64
NA_WIN_H = 8
NA_WIN_W = 16
EPS = 1e-6
NEG_INF = -1e30

RESIDUES = 4
SQRT_HALF = math.sqrt(0.5)
LANES = 128
HALO = 16
PAD = 8
VMEM_LIMIT = 56 * 1024 * 1024


def _params(sem):
    return pltpu.CompilerParams(dimension_semantics=sem, vmem_limit_bytes=VMEM_LIMIT)


def _dot(a, b):
    return jnp.dot(a, b, preferred_element_type=F32)


def _dot_t(a, b):
    return lax.dot_general(a, b, (((1,), (1,)), ((), ())), preferred_element_type=F32)


def _split_bf16(a):
    hi = a.astype(BF16)
    lo = (a - hi.astype(F32)).astype(BF16)
    return hi, lo


def _mod_kernel(c_ref, w_ref, b_ref, o_ref):
    a = c_ref[...]
    a = a * jax.nn.sigmoid(a)
    a_hi, a_lo = _split_bf16(a)
    w_hi, w_lo = _split_bf16(w_ref[0])
    o_ref[0] = _dot(a_hi, w_hi) + _dot(a_lo, w_hi) + _dot(a_hi, w_lo) + b_ref[0]


def _mod_call(cvec, w_mod, b_mod):
    depth, d, n = w_mod.shape
    rows = cvec.shape[0]
    tn = 1024
    return pl.pallas_call(
        _mod_kernel,
        grid=(depth, n // tn),
        in_specs=[
            pl.BlockSpec((rows, d), lambda l, j: (0, 0)),
            pl.BlockSpec((1, d, tn), lambda l, j: (l, 0, j)),
            pl.BlockSpec((1, 1, tn), lambda l, j: (l, 0, j)),
        ],
        out_specs=pl.BlockSpec((1, rows, tn), lambda l, j: (l, 0, j)),
        out_shape=jax.ShapeDtypeStruct((depth, rows, n), F32),
        compiler_params=_params(("arbitrary", "arbitrary")),
        name="mod",
    )(cvec, w_mod, b_mod.reshape(depth, 1, n))


def _ada_norm(x, g, shift, scale):
    y = x * lax.rsqrt(jnp.mean(x * x, axis=-1, keepdims=True) + EPS)
    return (y * g) * (1.0 + scale) + shift


def _deinterleave(y, scr_ref, ways):
    n = y.shape[0]
    parts = [[] for _ in range(ways)]
    for cb in range(y.shape[1] // LANES):
        scr_ref[cb] = y[:, cb * LANES:(cb + 1) * LANES]
        for k in range(ways):
            parts[k].append(scr_ref[cb, pl.ds(k, n // ways, stride=ways), :])
    return [jnp.concatenate(p, axis=1) for p in parts]


def _interleave(parts, scr_ref):
    ways = len(parts)
    n = parts[0].shape[0]
    cols = []
    for cb in range(parts[0].shape[1] // LANES):
        for k in range(ways):
            scr_ref[cb, pl.ds(k, n, stride=ways), :] = parts[k][:, cb * LANES:(cb + 1) * LANES]
        cols.append(scr_ref[cb])
    return jnp.concatenate(cols, axis=1)


def _proj_kernel(x_ref, sh_ref, sc_ref, g_ref, wa_ref, wb_ref, cs_ref, oa_ref, oh_ref, ob_ref, of_ref, scr_ref):
    m = cs_ref.shape[0]
    h = _ada_norm(x_ref[0], g_ref[...], sh_ref[0, 0], sc_ref[0, 0]).astype(BF16)
    pa = _dot(h, wa_ref[...])
    oa_ref[0] = pa[:, :m]
    oh_ref[0] = pa[:, m:].astype(BF16)
    pb = _dot(h, wb_ref[...])
    ob_ref[0] = pb[:, m:].astype(BF16)
    ev, od = _deinterleave(_dot(pb[:, :m].astype(BF16), cs_ref[...]), scr_ref, 2)
    of_ref[0, 0] = ev.astype(BF16)
    of_ref[0, 1] = od.astype(BF16)


def _proj_call(x, mod4, g, wa, wb, cs, tm):
    b, l, d = x.shape
    m = cs.shape[0]
    na, nb = wa.shape[1], wb.shape[1]
    const = lambda bi, i: (0, 0)
    return pl.pallas_call(
        _proj_kernel,
        grid=(b, l // tm),
        in_specs=[
            pl.BlockSpec((1, tm, d), lambda bi, i: (bi, i, 0)),
            pl.BlockSpec((1, 1, 1, d), lambda bi, i: (bi, 0, 0, 0)),
            pl.BlockSpec((1, 1, 1, d), lambda bi, i: (bi, 1, 0, 0)),
            pl.BlockSpec((1, d), const),
            pl.BlockSpec((d, na), const),
            pl.BlockSpec((d, nb), const),
            pl.BlockSpec((m, 2 * m), const),
        ],
        out_specs=[
            pl.BlockSpec((1, tm, m), lambda bi, i: (bi, i, 0)),
            pl.BlockSpec((1, tm, na - m), lambda bi, i: (bi, i, 0)),
            pl.BlockSpec((1, tm, nb - m), lambda bi, i: (bi, i, 0)),
            pl.BlockSpec((1, 2, tm // 2, 2 * m), lambda bi, i: (bi, 0, i, 0)),
        ],
        out_shape=[
            jax.ShapeDtypeStruct((b, l, m), F32),
            jax.ShapeDtypeStruct((b, l, na - m), BF16),
            jax.ShapeDtypeStruct((b, l, nb - m), BF16),
            jax.ShapeDtypeStruct((b, 2, l // 2, 2 * m), BF16),
        ],
        scratch_shapes=[pltpu.VMEM((2 * m // LANES, tm, LANES), F32)],
        compiler_params=_params(("parallel", "parallel")),
        name="proj",
    )(x, mod4, mod4, g, wa, wb, cs)


def _fill_padded(src_ref, pad_ref, l, rows):
    c = pad_ref.shape[1]
    pad_ref[0:PAD, :] = jnp.zeros((PAD, c), F32)
    pad_ref[l + PAD:l + 2 * PAD, :] = jnp.zeros((PAD, c), F32)

    def copy(i, carry):
        r = pl.multiple_of(i * rows, rows)
        pad_ref[pl.ds(r + PAD, rows), :] = src_ref[0, pl.ds(r, rows), :].astype(F32)
        return carry

    lax.fori_loop(0, l // rows, copy, 0)


def _centred_window_sums(u, windows):
    n = u.shape[0]
    fwd = {1: u}
    w = 1
    while w < max(windows):
        fwd[2 * w] = fwd[w] + pltpu.roll(fwd[w], n - w, 0)
        w *= 2
    return {w: pltpu.roll(fwd[w], w // 2, 0) for w in windows}


def _pool_kernel(u_ref, inv_ref, w_ref, s_ref, o_ref, pad_ref, *, rows):
    l, c = u_ref.shape[1], u_ref.shape[2]
    gw = c // len(POOL_WINDOWS)
    per_block = LANES // gw
    _fill_padded(u_ref, pad_ref, l, rows)
    n = rows + 2 * PAD
    lane = lax.broadcasted_iota(jnp.int32, (n, LANES), 1)
    w = w_ref[...]
    scale = s_ref[...]

    def body(i, carry):
        r = pl.multiple_of(i * rows, rows)
        cols = []
        for cb in range(c // LANES):
            lanes = slice(cb * LANES, (cb + 1) * LANES)
            win = pad_ref[pl.ds(r, n), lanes]
            wins = POOL_WINDOWS[cb * per_block:(cb + 1) * per_block]
            sums = _centred_window_sums(win, wins)
            s = sums[wins[-1]]
            for k in range(per_block - 2, -1, -1):
                s = jnp.where(lane < (k + 1) * gw, sums[wins[k]], s)
            cols.append((s * inv_ref[pl.ds(r, n), lanes] - win)[PAD:PAD + rows])
        y = jnp.concatenate(cols, axis=1)
        o_ref[0, pl.ds(r, rows), :] = (_dot(y.astype(BF16), w) * scale).astype(BF16)
        return carry

    lax.fori_loop(0, l // rows, body, 0)


def _pool_inv_counts(l, c):
    gw = c // len(POOL_WINDOWS)
    t = jnp.arange(-PAD, l + PAD, dtype=jnp.int32)[:, None]
    half = jnp.asarray(np.repeat(np.array(POOL_WINDOWS) // 2, gw), jnp.int32)[None, :]
    cnt = jnp.minimum(t + half, l) - jnp.maximum(t - half, 0)
    return 1.0 / jnp.maximum(cnt, 1).astype(F32)


def _pool_call(oa, inv_cnt, w_blk, scale, rows):
    b, l, _ = oa.shape
    c = w_blk.shape[0]
    return pl.pallas_call(
        functools.partial(_pool_kernel, rows=rows),
        grid=(b,),
        in_specs=[
            pl.BlockSpec((1, l, c), lambda bi: (bi, 0, 0)),
            pl.BlockSpec((l + 2 * PAD, c), lambda bi: (0, 0)),
            pl.BlockSpec((c, c), lambda bi: (0, 0)),
            pl.BlockSpec((1, c), lambda bi: (0, 0)),
        ],
        out_specs=pl.BlockSpec((1, l, c), lambda bi: (bi, 0, 0)),
        out_shape=jax.ShapeDtypeStruct((b, l, c), BF16),
        scratch_shapes=[pltpu.VMEM((l + 2 * PAD, c), F32)],
        compiler_params=_params(("parallel",)),
        name="pool",
    )(oa, inv_cnt, w_blk, scale)


def _dwconv_kernel(u_ref, w_ref, b_ref, o_ref, pad_ref, scr_ref, *, rows):
    l = u_ref.shape[1]
    _fill_padded(u_ref, pad_ref, l, rows)
    n = rows + 2 * PAD
    sub = rows // RESIDUES
    w = w_ref[...]
    bias = b_ref[...]

    def body(i, carry):
        r = pl.multiple_of(i * rows, rows)
        rs = pl.multiple_of(i * sub, sub)
        win = pad_ref[pl.ds(r, n), :]
        y = pltpu.roll(win, 1, 0) * w[0:1] + win * w[1:2] + pltpu.roll(win, n - 1, 0) * w[2:3] + bias
        for k, part in enumerate(_deinterleave(y[PAD:PAD + rows], scr_ref, RESIDUES)):
            o_ref[0, k, pl.ds(rs, sub), :] = part.astype(BF16)
        return carry

    lax.fori_loop(0, l // rows, body, 0)


def _dwconv_call(u, w, bias, c, rows):
    b, l, _ = u.shape
    nblk = w.shape[1] // c
    return pl.pallas_call(
        functools.partial(_dwconv_kernel, rows=rows),
        grid=(b, nblk),
        in_specs=[
            pl.BlockSpec((1, l, c), lambda bi, j: (bi, 0, j)),
            pl.BlockSpec((w.shape[0], c), lambda bi, j: (0, j)),
            pl.BlockSpec((1, c), lambda bi, j: (0, j)),
        ],
        out_specs=pl.BlockSpec((1, RESIDUES, l // RESIDUES, c), lambda bi, j: (bi, 0, 0, j)),
        out_shape=jax.ShapeDtypeStruct((b, RESIDUES, l // RESIDUES, nblk * c), BF16),
        scratch_shapes=[pltpu.VMEM((l + 2 * PAD, c), F32), pltpu.VMEM((c // LANES, rows, LANES), F32)],
        compiler_params=_params(("parallel", "parallel")),
        name="dwconv",
    )(u, w, bias)


def _filt_kernel(f_ref, w1_ref, b1_ref, w2_ref, b2_ref, w3_ref, b3_ref, fr_ref, dl_ref,
                 hb_ref, row0_ref, asum_ref):
    hp = lax.Precision.HIGHEST
    feats = f_ref[...]
    freq = fr_ref[...]
    h = jnp.sin(freq * (jnp.dot(feats, w1_ref[...], precision=hp, preferred_element_type=F32) + b1_ref[...]))
    h = jnp.sin(freq * (jnp.dot(h, w2_ref[...], precision=hp, preferred_element_type=F32) + b2_ref[...]))
    h = jnp.dot(h, w3_ref[...], precision=hp, preferred_element_type=F32) + b3_ref[...]
    t = feats[:, 0:1]
    win = jnp.exp(-t * jnp.abs(dl_ref[...])) + HYENA_DECAY_SHIFT
    h = h * jnp.concatenate([win] * (h.shape[1] // win.shape[1]), axis=1)
    hb = h.astype(BF16)
    hb_ref[...] = hb
    part = jnp.sum(jnp.abs(h), axis=0, keepdims=True)

    @pl.when(pl.program_id(0) == 0)
    def _():
        asum_ref[...] = jnp.zeros_like(asum_ref)
        row0_ref[...] = hb[0:16].astype(F32)[0:8]

    asum_ref[...] += part


def _filt_call(feats, w1, b1, w2, b2, w3, b3, freq, deltas, rows):
    l, fd = feats.shape
    hd = w2.shape[0]
    n = w3.shape[1]
    c = deltas.shape[1]
    const = lambda i: (0, 0)
    return pl.pallas_call(
        _filt_kernel,
        grid=(l // rows,),
        in_specs=[
            pl.BlockSpec((rows, fd), lambda i: (i, 0)),
            pl.BlockSpec((fd, hd), const), pl.BlockSpec((1, hd), const),
            pl.BlockSpec((hd, hd), const), pl.BlockSpec((1, hd), const),
            pl.BlockSpec((hd, n), const), pl.BlockSpec((1, n), const),
            pl.BlockSpec((1, hd), const), pl.BlockSpec((1, c), const),
        ],
        out_specs=[
            pl.BlockSpec((rows, n), lambda i: (i, 0)),
            pl.BlockSpec((8, n), const),
            pl.BlockSpec((1, n), const),
        ],
        out_shape=[
            jax.ShapeDtypeStruct((l, n), BF16),
            jax.ShapeDtypeStruct((8, n), F32),
            jax.ShapeDtypeStruct((1, n), F32),
        ],
        compiler_params=_params(("arbitrary",)),
        name="hyena_filt",
    )(feats, w1, b1, w2, b2, w3, b3, freq, deltas)


def _alt_signs(n):
    t = lax.broadcasted_iota(jnp.int32, (16, n), 1)
    return jnp.where((t & 1) == 0, 1.0, -1.0).astype(BF16)


def _mat_specs(tm, cols, n):
    return [pl.BlockSpec((tm, cols), lambda g, i: (i, 0)) for _ in range(n)]


def _quad_transform(mat_refs, parts):
    a = [_dot(mat_refs[2 * r][...], parts[r]) for r in range(RESIDUES)]
    b = [_dot(mat_refs[2 * r + 1][...], parts[r]) for r in range(RESIDUES)]
    ea, fa, ga, ha = a[0] + a[2], a[0] - a[2], a[1] + a[3], a[1] - a[3]
    eb, fb, gb, hb = b[0] + b[2], b[0] - b[2], b[1] + b[3], b[1] - b[3]
    zr = (ea + ga, fa + hb, fa - hb, ea - ga)
    zs = (eb + gb, ha - fb, ha + fb, gb - eb)
    return zr, zs


def _quad_special(alt):
    c1, c3 = SQRT_HALF * alt[1], SQRT_HALF * alt[3]
    return (alt[0] + c1 - c3, c1 + alt[2] + c3), (alt[0] - c1 + c3, c1 - alt[2] + c3)


def _kspec_kernel(*refs):
    mat_refs, (h_ref, row0_ref, asum_ref, kk_ref, km_ref) = refs[:8], refs[8:]
    i = pl.program_id(1)
    tm, q = mat_refs[0].shape
    c = kk_ref.shape[3]
    l = RESIDUES * q
    parts = [h_ref[r * q:(r + 1) * q, :] for r in range(RESIDUES)]
    zr, zs = _quad_transform(mat_refs, parts)
    asum = asum_ref[...]
    inv = 1.0 / (asum[:, :c] + asum[:, c:] + EPS)
    hb0 = row0_ref[0:1, c:]
    row = lax.broadcasted_iota(jnp.int32, (tm, c), 0) + i * tm
    wj = jnp.where(row == 0, 0.5 / l, 1.0 / l) * inv
    for f in range(4):
        kk_ref[0, 2 * f] = (zr[f][:, :c] + zr[f][:, c:] - hb0) * wj
        kk_ref[0, 2 * f + 1] = (zs[f][:, :c] - zs[f][:, c:]) * wj

    @pl.when(i == 0)
    def _():
        sg = _alt_signs(q)
        alt = [_dot(sg, p)[0:8] for p in parts]
        for f, (xr, xs) in enumerate(_quad_special(alt)):
            km_ref[0, :, 2 * f * c:(2 * f + 1) * c] = (xr[:, :c] + xr[:, c:] - hb0) * inv * (1.0 / l)
            km_ref[0, :, (2 * f + 1) * c:(2 * f + 2) * c] = (xs[:, :c] - xs[:, c:]) * inv * (1.0 / l)


def _kspec_call(mats, h, row0, asum, c, tm):
    q = mats[0].shape[0]
    l = RESIDUES * q
    orders = h.shape[1] // (2 * c)
    return pl.pallas_call(
        _kspec_kernel,
        grid=(orders, q // tm),
        in_specs=_mat_specs(tm, q, 8) + [
            pl.BlockSpec((l, 2 * c), lambda o, i: (0, o)),
            pl.BlockSpec((8, 2 * c), lambda o, i: (0, o)),
            pl.BlockSpec((1, 2 * c), lambda o, i: (0, o)),
        ],
        out_specs=[
            pl.BlockSpec((1, 8, tm, c), lambda o, i: (o, 0, i, 0)),
            pl.BlockSpec((1, 8, 4 * c), lambda o, i: (o, 0, 0)),
        ],
        out_shape=[
            jax.ShapeDtypeStruct((orders, 8, q, c), F32),
            jax.ShapeDtypeStruct((orders, 8, 4 * c), F32),
        ],
        compiler_params=_params(("arbitrary", "arbitrary")),
        name="hyena_kspec",
    )(*mats, h, row0, asum)


def _cmul(zr, zs, kr, ks):
    return zr * kr - zs * ks, zr * ks + zs * kr


def _hfwd_kernel(*refs):
    mat_refs, (z_ref, kk_ref, km_ref, spec_ref, ym_ref) = refs[:8], refs[8:]
    nb = z_ref.shape[0]
    c = z_ref.shape[3]
    for bb in range(nb):
        zr, zs = _quad_transform(mat_refs, [z_ref[bb, r] for r in range(RESIDUES)])
        yr, ys = zip(*[_cmul(zr[f], zs[f], kk_ref[0, 2 * f], kk_ref[0, 2 * f + 1]) for f in range(4)])
        a, b, u1, u2 = yr[0] + yr[3], yr[0] - yr[3], yr[1] + yr[2], ys[1] + ys[2]
        cc, dd, d1, d2 = ys[0] - ys[3], ys[0] + ys[3], yr[1] - yr[2], ys[2] - ys[1]
        for k, v in enumerate((a + u1, cc + d2, b + u2, dd + d1, a - u1, cc - d2, b - u2, dd - d1)):
            spec_ref[bb, k] = v.astype(BF16)

    @pl.when(pl.program_id(1) == 0)
    def _():
        sg = _alt_signs(z_ref.shape[2])
        for bb in range(nb):
            alt = [_dot(sg, z_ref[bb, r])[0:8] for r in range(RESIDUES)]
            (zq_r, zq_s), (z3_r, z3_s) = _quad_special(alt)
            yq_r, yq_s = _cmul(zq_r, zq_s, km_ref[0, :, 0:c], km_ref[0, :, c:2 * c])
            y3_r, y3_s = _cmul(z3_r, z3_s, km_ref[0, :, 2 * c:3 * c], km_ref[0, :, 3 * c:4 * c])
            ym_ref[bb, :, 0:c] = yq_r + y3_r
            ym_ref[bb, :, c:2 * c] = SQRT_HALF * (yq_r + yq_s - y3_r + y3_s)
            ym_ref[bb, :, 2 * c:3 * c] = yq_s - y3_s
            ym_ref[bb, :, 3 * c:4 * c] = SQRT_HALF * (yq_s - yq_r + y3_r + y3_s)


def _hfwd_call(mats, zb, kk, km, order, c, nb, tm):
    b, _, q, _ = zb.shape
    return pl.pallas_call(
        _hfwd_kernel,
        grid=(b // nb, q // tm),
        in_specs=_mat_specs(tm, q, 8) + [
            pl.BlockSpec((nb, RESIDUES, q, c), lambda g, i: (g, 0, 0, 0)),
            pl.BlockSpec((1, 8, tm, c), lambda g, i: (order, 0, i, 0)),
            pl.BlockSpec((1, 8, 4 * c), lambda g, i: (order, 0, 0)),
        ],
        out_specs=[
            pl.BlockSpec((nb, 8, tm, c), lambda g, i: (g, 0, i, 0)),
            pl.BlockSpec((nb, 8, 4 * c), lambda g, i: (g, 0, 0)),
        ],
        out_shape=[
            jax.ShapeDtypeStruct((b, 8, q, c), BF16),
            jax.ShapeDtypeStruct((b, 8, 4 * c), F32),
        ],
        compiler_params=_params(("arbitrary", "arbitrary")),
        name="hyena_fwd",
    )(*mats, zb, kk, km)


def _hinv_kernel(*refs, natural_out):
    mat_refs, (spec_ref, ym_ref, z_ref, gate_ref, skip_ref), out_refs = refs[:8], refs[8:13], refs[13:]
    nb, _, tm, c = z_ref.shape
    row = lax.broadcasted_iota(jnp.int32, (tm, c), 0)
    even_row = (row & 1) == 0
    skip = skip_ref[0]
    for bb in range(nb):
        outs = []
        for r in range(RESIDUES):
            m = ym_ref[bb, 0:1, r * c:(r + 1) * c]
            y = (_dot(mat_refs[2 * r][...], spec_ref[bb, 2 * r]) + _dot(mat_refs[2 * r + 1][...], spec_ref[bb, 2 * r + 1])
                 + jnp.where(even_row, m, -m))
            outs.append(gate_ref[bb, r].astype(F32) * (y + skip * z_ref[bb, r].astype(F32)))
        if natural_out:
            o_ref, scr_ref = out_refs
            o_ref[bb] = _interleave(outs, scr_ref).astype(BF16)
        else:
            (o_ref,) = out_refs
            for r in range(RESIDUES):
                o_ref[bb, r] = outs[r].astype(BF16)


def _hinv_call(mats, spec, ym, zf, zcol, uc, gcol, skip, order, natural_out, c, nb, tm):
    b, _, q, _ = spec.shape
    if natural_out:
        out_specs = pl.BlockSpec((nb, RESIDUES * tm, c), lambda g, i: (g, i, 0))
        out_shape = jax.ShapeDtypeStruct((b, RESIDUES * q, c), BF16)
        scratch = [pltpu.VMEM((c // LANES, RESIDUES * tm, LANES), F32)]
    else:
        out_specs = pl.BlockSpec((nb, RESIDUES, tm, c), lambda g, i: (g, 0, i, 0))
        out_shape = jax.ShapeDtypeStruct((b, RESIDUES, q, c), BF16)
        scratch = []
    return pl.pallas_call(
        functools.partial(_hinv_kernel, natural_out=natural_out),
        grid=(b // nb, q // tm),
        in_specs=_mat_specs(tm, q, 8) + [
            pl.BlockSpec((nb, 8, q, c), lambda g, i: (g, 0, 0, 0)),
            pl.BlockSpec((nb, 8, 4 * c), lambda g, i: (g, 0, 0)),
            pl.BlockSpec((nb, RESIDUES, tm, c), lambda g, i: (g, 0, i, zcol)),
            pl.BlockSpec((nb, RESIDUES, tm, c), lambda g, i: (g, 0, i, gcol)),
            pl.BlockSpec((1, 1, c), lambda g, i: (order, 0, 0)),
        ],
        out_specs=out_specs,
        out_shape=out_shape,
        scratch_shapes=scratch,
        compiler_params=_params(("arbitrary", "arbitrary")),
        name="hyena_inv",
    )(*mats, spec, ym, zf, uc, skip)


def _fseq_kernel(ce_ref, co_ref, se_ref, so_ref, u_ref, o_ref):
    nb = u_ref.shape[0]
    c = o_ref.shape[3]
    ce, co, se, so = ce_ref[...], co_ref[...], se_ref[...], so_ref[...]
    for bb in range(nb):
        ae = _dot(ce, u_ref[bb, 0, :, 0:c]) + _dot(se, u_ref[bb, 0, :, c:2 * c])
        ao = _dot(co, u_ref[bb, 1, :, 0:c]) + _dot(so, u_ref[bb, 1, :, c:2 * c])
        o_ref[bb, 0] = (ae + ao).astype(BF16)
        o_ref[bb, 1] = (ae - ao).astype(BF16)


def _fseq_call(mats, ucs, nb, tm):
    b, _, half, c2 = ucs.shape
    c = c2 // 2
    out = pl.pallas_call(
        _fseq_kernel,
        grid=(b // nb, half // tm),
        in_specs=_mat_specs(tm, half, 4) + [
            pl.BlockSpec((nb, 2, half, c2), lambda g, i: (g, 0, 0, 0), pipeline_mode=pl.Buffered(1))],
        out_specs=pl.BlockSpec((nb, 2, tm, c), lambda g, i: (g, 0, i, 0)),
        out_shape=jax.ShapeDtypeStruct((b, 2, half, c), BF16),
        compiler_params=_params(("arbitrary", "arbitrary")),
        name="fnet_seq",
    )(*mats, ucs)
    return out.reshape(b, 2 * half, c)


def _stack_heads(q, heads):
    lane = lax.broadcasted_iota(jnp.int32, q.shape, 1)
    zero = jnp.zeros_like(q)
    return jnp.concatenate([jnp.where(lax.shift_right_logical(lane, int(math.log2(NA_HEAD_DIM))) == h, q, zero) for h in range(heads)], axis=0)


def _merge_heads(o, heads, n):
    lane = lax.broadcasted_iota(jnp.int32, (n, o.shape[1]), 1)
    out = jnp.zeros((n, o.shape[1]), F32)
    for h in range(heads):
        out = out + jnp.where(lax.shift_right_logical(lane, int(math.log2(NA_HEAD_DIM))) == h, o[h * n:(h + 1) * n], 0.0)
    return out


def _natt_kernel(q_ref, k_ref, v_ref, kc_ref, vc_ref, bias_ref, o_ref, *, rows, kh, group):
    w = GRID_W
    heads = q_ref.shape[2] // NA_HEAD_DIM
    base = pl.program_id(1) * group
    lo = kh // 2
    kc, vc = kc_ref[0], vc_ref[0]
    scale = jnp.asarray(NA_HEAD_DIM ** -0.5, BF16)

    def body(j, carry):
        r = base + j
        r0 = jnp.clip(r - lo, 0, rows - kh)
        start = pl.multiple_of(r0 * w, w)
        cls = jnp.minimum(r, lo) + jnp.maximum(r - (rows - kh + lo), 0)
        q4 = _stack_heads(q_ref[0, pl.ds(pl.multiple_of(j * w, w), w), :] * scale, heads)
        ks = k_ref[0, pl.ds(start, kh * w), :]
        vs = v_ref[0, pl.ds(start, kh * w), :]
        s_nb = _dot_t(q4, ks) + bias_ref[cls]
        s_cx = _dot_t(q4, kc)
        m = jnp.maximum(jnp.max(s_nb, axis=-1, keepdims=True), jnp.max(s_cx, axis=-1, keepdims=True))
        p_nb = jnp.exp(s_nb - m)
        p_cx = jnp.exp(s_cx - m)
        den = jnp.sum(p_nb, axis=-1, keepdims=True) + jnp.sum(p_cx, axis=-1, keepdims=True)
        o4 = (_dot(p_nb.astype(BF16), vs) + _dot(p_cx.astype(BF16), vc)) / den
        o_ref[0, pl.ds(pl.multiple_of(j * w, w), w), :] = _merge_heads(o4, heads, w).astype(BF16)
        return carry

    lax.fori_loop(0, group, body, 0, unroll=4)


def _natt_call(ob, obc, bias, c, group):
    b, l, _ = ob.shape
    lc = obc.shape[1]
    w = GRID_W
    rows = l // w
    kh = min(NA_WIN_H, rows)
    return pl.pallas_call(
        functools.partial(_natt_kernel, rows=rows, kh=kh, group=group),
        grid=(b, rows // group),
        in_specs=[
            pl.BlockSpec((1, group * w, c), lambda bi, r: (bi, r, 0)),
            pl.BlockSpec((1, l, c), lambda bi, r: (bi, 0, 1)),
            pl.BlockSpec((1, l, c), lambda bi, r: (bi, 0, 2)),
            pl.BlockSpec((1, lc, c), lambda bi, r: (bi, 0, 1)),
            pl.BlockSpec((1, lc, c), lambda bi, r: (bi, 0, 2)),
            pl.BlockSpec(bias.shape, lambda bi, r: (0, 0, 0)),
        ],
        out_specs=pl.BlockSpec((1, group * w, c), lambda bi, r: (bi, r, 0)),
        out_shape=jax.ShapeDtypeStruct((b, l, c), BF16),
        compiler_params=_params(("parallel", "arbitrary")),
        name="nb_attn",
    )(ob, ob, ob, obc, obc, bias)


def _bias_kernel(r_ref, o_ref):
    w = GRID_W
    shift = int(math.log2(w))
    n = o_ref.shape[1]
    r = r_ref[...]
    hi = r.astype(BF16)
    mid = (r - hi.astype(F32)).astype(BF16)
    lo = (r - hi.astype(F32) - mid.astype(F32)).astype(BF16)
    dc = lax.broadcasted_iota(jnp.int32, (r.shape[1], n), 0)
    p = lax.broadcasted_iota(jnp.int32, (r.shape[1], n), 1)
    idx = jnp.clip((p & (w - 1)) - lax.shift_right_logical(p, shift), 1 - NA_WIN_W, NA_WIN_W - 1) + (NA_WIN_W - 1)
    onehot = jnp.where(dc == idx, 1.0, 0.0).astype(BF16)
    val = _dot(hi, onehot) + _dot(mid, onehot) + _dot(lo, onehot)
    po = lax.broadcasted_iota(jnp.int32, val.shape, 1)
    cq = lax.shift_right_logical(po, shift)
    ck = po & (w - 1)
    c0 = jnp.clip(cq - NA_WIN_W // 2, 0, w - NA_WIN_W)
    inside = jnp.logical_and(ck >= c0, ck < c0 + NA_WIN_W)
    o_ref[...] = jnp.where(inside, val, NEG_INF)


def _bias_call(rpb2d):
    rows, cols = rpb2d.shape
    n = GRID_W * GRID_W
    return pl.pallas_call(
        _bias_kernel,
        grid=(1,),
        in_specs=[pl.BlockSpec((rows, cols), lambda i: (0, 0))],
        out_specs=pl.BlockSpec((rows, n), lambda i: (0, 0)),
        out_shape=jax.ShapeDtypeStruct((rows, n), F32),
        compiler_params=_params(("arbitrary",)),
        name="rpb_table",
    )(rpb2d)


def _catt_kernel(q_ref, k_ref, v_ref, o_ref):
    n = q_ref.shape[1]
    heads = q_ref.shape[2] // NA_HEAD_DIM
    q = q_ref[0] * jnp.asarray(NA_HEAD_DIM ** -0.5, BF16)
    q4 = _stack_heads(q, heads)
    s = _dot_t(q4, k_ref[0])
    p = jnp.exp(s - jnp.max(s, axis=-1, keepdims=True))
    den = jnp.sum(p, axis=-1, keepdims=True)
    o4 = _dot(p.astype(BF16), v_ref[0]) / den
    o_ref[0] = _merge_heads(o4, heads, n).astype(BF16)


def _catt_call(obc, c):
    b, lc, _ = obc.shape
    return pl.pallas_call(
        _catt_kernel,
        grid=(b,),
        in_specs=[pl.BlockSpec((1, lc, c), lambda bi, j=j: (bi, 0, j)) for j in range(3)],
        out_specs=pl.BlockSpec((1, lc, c), lambda bi: (bi, 0, 0)),
        out_shape=jax.ShapeDtypeStruct((b, lc, c), BF16),
        compiler_params=_params(("parallel",)),
        name="ctx_attn",
    )(obc, obc, obc)


def _merge_kernel(x_ref, sh_ref, sc_ref, gt_ref, g_ref, y0_ref, y1_ref, y2_ref, y3_ref,
                  wg_ref, wb_ref, wo_ref, o_ref):
    d = x_ref.shape[2]
    x = x_ref[0]
    h = _ada_norm(x, g_ref[...], sh_ref[0, 0], sc_ref[0, 0]).astype(BF16)
    merged = None
    for bi, y_ref in enumerate((y0_ref, y1_ref, y2_ref, y3_ref)):
        gate = jax.nn.sigmoid(_dot(h, wg_ref[:, bi * d:(bi + 1) * d]))
        term = gate * _dot(y_ref[0].astype(BF16), wb_ref[bi])
        merged = term if merged is None else merged + term
    o_ref[0] = x + gt_ref[0, 0] * _dot(merged.astype(BF16), wo_ref[...])


def _merge_call(x, mod4, g, ys, wg, wb, wo, tm):
    b, l, d = x.shape
    c = ys[0].shape[2]
    const2 = lambda bi, i: (0, 0)
    mod_spec = lambda k: pl.BlockSpec((1, 1, 1, d), lambda bi, i: (bi, k, 0, 0))
    return pl.pallas_call(
        _merge_kernel,
        grid=(b, l // tm),
        in_specs=[
            pl.BlockSpec((1, tm, d), lambda bi, i: (bi, i, 0)),
            mod_spec(0), mod_spec(1), mod_spec(2),
            pl.BlockSpec((1, d), const2),
        ] + [pl.BlockSpec((1, tm, c), lambda bi, i: (bi, i, 0)) for _ in ys] + [
            pl.BlockSpec(wg.shape, const2, pipeline_mode=pl.Buffered(1)),
            pl.BlockSpec(wb.shape, lambda bi, i: (0, 0, 0), pipeline_mode=pl.Buffered(1)),
            pl.BlockSpec(wo.shape, const2, pipeline_mode=pl.Buffered(1)),
        ],
        out_specs=pl.BlockSpec((1, tm, d), lambda bi, i: (bi, i, 0)),
        out_shape=jax.ShapeDtypeStruct((b, l, d), F32),
        compiler_params=_params(("parallel", "parallel")),
        name="merge",
    )(x, mod4, mod4, mod4, g, *ys, wg, wb, wo)


def _ffn_kernel(x_ref, xp_ref, xn_ref, sh_ref, sc_ref, gt_ref, g_ref, wu_ref, wgt_ref, cw_ref, cb_ref,
                wd_ref, fg_ref, o_ref, a_ref, *, chunk, final_norm):
    i = pl.program_id(1)
    last = pl.num_programs(1) - 1
    tm = x_ref.shape[1]
    f = wu_ref.shape[1]
    n = tm + 2 * HALO
    g, sh, sc = g_ref[...], sh_ref[0, 0], sc_ref[0, 0]
    x = x_ref[0]
    h = _ada_norm(x, g, sh, sc).astype(BF16)
    hp = _ada_norm(xp_ref[0], g, sh, sc).astype(BF16)
    hn = _ada_norm(xn_ref[0], g, sh, sc).astype(BF16)
    hext = jnp.concatenate([hp, h, hn], axis=0)
    row = lax.broadcasted_iota(jnp.int32, (n, chunk), 0)
    valid = jnp.logical_and(jnp.logical_or(row >= HALO, i > 0), jnp.logical_or(row < HALO + tm, i < last))
    for j in range(f // chunk):
        cols = slice(j * chunk, (j + 1) * chunk)
        gp = jnp.where(valid, _dot(hext, wgt_ref[:, cols]), 0.0)
        cw = cw_ref[:, cols]
        gc = (pltpu.roll(gp, 1, 0) * cw[0:1] + gp * cw[1:2] + pltpu.roll(gp, n - 1, 0) * cw[2:3]
              + cb_ref[:, cols])[HALO:HALO + tm]
        u = _dot(h, wu_ref[:, cols])
        a_ref[:, cols] = (gc * jax.nn.sigmoid(gc) * u).astype(BF16)
    y = x + gt_ref[0, 0] * _dot(a_ref[...], wd_ref[...])
    if final_norm:
        y = y * lax.rsqrt(jnp.mean(y * y, axis=-1, keepdims=True) + EPS) * fg_ref[...]
    o_ref[0] = y


def _ffn_call(x, mod4, g, wu, wgt, cw, cb, wd, fg, tm, chunk, final_norm):
    b, l, d = x.shape
    f = wu.shape[1]
    hb = tm // HALO
    nblk = l // HALO
    const2 = lambda bi, i: (0, 0)
    mod_spec = lambda k: pl.BlockSpec((1, 1, 1, d), lambda bi, i: (bi, k, 0, 0))
    single = dict(pipeline_mode=pl.Buffered(1))
    return pl.pallas_call(
        functools.partial(_ffn_kernel, chunk=chunk, final_norm=final_norm),
        grid=(b, l // tm),
        in_specs=[
            pl.BlockSpec((1, tm, d), lambda bi, i: (bi, i, 0)),
            pl.BlockSpec((1, HALO, d), lambda bi, i: (bi, jnp.maximum(i * hb - 1, 0), 0)),
            pl.BlockSpec((1, HALO, d), lambda bi, i: (bi, jnp.minimum((i + 1) * hb, nblk - 1), 0)),
            mod_spec(3), mod_spec(4), mod_spec(5),
            pl.BlockSpec((1, d), const2),
            pl.BlockSpec((d, f), const2, **single),
            pl.BlockSpec((d, f), const2, **single),
            pl.BlockSpec((cw.shape[0], f), const2),
            pl.BlockSpec((1, f), const2),
            pl.BlockSpec((f, d), const2, **single),
            pl.BlockSpec((1, d), const2),
        ],
        out_specs=pl.BlockSpec((1, tm, d), lambda bi, i: (bi, i, 0)),
        out_shape=jax.ShapeDtypeStruct((b, l, d), F32),
        scratch_shapes=[pltpu.VMEM((tm, f), BF16)],
        compiler_params=_params(("parallel", "arbitrary")),
        name="ffn",
    )(x, x, x, mod4, mod4, mod4, g, wu, wgt, cw, cb, wd, fg)


def _trig_tables(n, row_mul, row_add, col_mul, col_add, period):
    a_sz = 64 if n % 64 == 0 else 1
    q = (jnp.arange(n, dtype=jnp.int32) * col_mul + col_add)[None, :]
    ra = (jnp.arange(n // a_sz, dtype=jnp.int32) * (a_sz * row_mul))[:, None]
    rb = (jnp.arange(a_sz, dtype=jnp.int32) * row_mul + row_add)[:, None]
    w = 2.0 * math.pi / period
    ang_a = ((ra * q) % period).astype(F32) * w
    ang_b = ((rb * q) % period).astype(F32) * w
    ca, sa = jnp.cos(ang_a)[:, None, :], jnp.sin(ang_a)[:, None, :]
    cb, sb = jnp.cos(ang_b)[None, :, :], jnp.sin(ang_b)[None, :, :]
    cos = (ca * cb - sa * sb).reshape(n, n)
    sin = (sa * cb + ca * sb).reshape(n, n)
    return cos, sin


def _pad2(a, rows, cols):
    return jnp.pad(a, ((0, rows - a.shape[0]), (0, cols - a.shape[1])))


def _hyena_features(l, pad_to):
    t = jnp.linspace(0.0, 1.0, l, dtype=F32)[:, None]
    bands = jnp.linspace(1e-4, HYENA_BANDS - 1, HYENA_BANDS, dtype=F32)[None, :]
    ang = (2.0 * math.pi / l) * jnp.arange(l, dtype=F32)[:, None] * bands
    feats = jnp.concatenate([t, jnp.cos(ang), -jnp.sin(ang)], axis=-1)
    return jnp.pad(feats, ((0, 0), (0, pad_to - feats.shape[1])))


def _bias_classes(rows, kh):
    r = np.arange(rows)
    r0 = np.clip(r - kh // 2, 0, rows - kh)
    off = r0 - r
    lo = kh // 2
    reps = list(range(lo)) + [lo] + list(range(rows - kh + lo + 1, rows))
    return [int(off[i]) for i in reps]


def _attention_bias(rpb, rows, kh):
    heads, nr, nc = rpb.shape
    w = GRID_W
    flat = rpb.astype(F32).reshape(heads * nr, nc)
    table = _bias_call(_pad2(flat, -(-heads * nr // 16) * 16, 128))
    table = table[:heads * nr].reshape(heads, nr, w, w)
    out = []
    for off in _bias_classes(rows, kh):
        first = off + NA_WIN_H - 1
        sl = table[:, first:first + kh]
        out.append(jnp.transpose(sl, (0, 2, 1, 3)).reshape(heads * w, kh * w))
    return jnp.stack(out)


def _mixer_inputs(x, mod4, lw, tm):
    return _proj_call(x, mod4, lw['norm1_g'], lw['wa'], lw['wb'], lw['cs'], tm)


def _hyena(oh, lw, tabs, c, nb_fwd, nb, tm, rows):
    fwd, inv = tabs['hy_fwd'], tabs['hy_inv']
    uc = _dwconv_call(oh, lw['hyena_conv_w'], lw['hyena_conv_b'], c, rows)
    h, row0, asum = _filt_call(tabs['feats'], lw['fw1'], lw['fb1'], lw['fw2'], lw['fb2'], lw['fw3'],
                               lw['fb3'], lw['freq'], tabs['deltas'], min(rows * 2, oh.shape[1]))
    kk, km = _kspec_call(fwd, h, row0, asum, c, tm)
    spec, ym = _hfwd_call(fwd, uc, kk, km, 0, c, nb_fwd, tm)
    z2 = _hinv_call(inv, spec, ym, uc, 0, uc, 1, lw['skip'], 0, False, c, nb, tm)
    spec, ym = _hfwd_call(fwd, z2, kk, km, 1, c, nb_fwd, tm)
    return _hinv_call(inv, spec, ym, z2, 0, uc, 2, lw['skip'], 1, True, c, nb, tm)


def _stream_layer(x, mod4, lw, tabs, proj, y_att, cfg, final_norm):
    c = cfg['c']
    oa, oh, ob, of = proj
    y_pool = _pool_call(oa, tabs['pool_inv'], lw['pool_blk'], lw['pool_scale'], cfg['rows'])
    y_fnet = _fseq_call(tabs['fn'], of, cfg['nb_fnet'], cfg['tseq'])
    y_hyena = _hyena(oh, lw, tabs, c, cfg['nb'], cfg['nb'], cfg['thy'], cfg['rows'])
    x = _merge_call(x, mod4, lw['norm1_g'], (y_pool, y_fnet, y_hyena, y_att), lw['wg'], lw['w_branch'],
                    lw['w_out'], cfg['tm'])
    return _ffn_call(x, mod4, lw['norm2_g'], lw['wu'], lw['wgt'], lw['ffn_conv_w'], lw['ffn_conv_b'],
                     lw['wd'], lw['final_g'], cfg['tm'], cfg['chunk'], final_norm)


def _seq_tables(l, c):
    half, q = l // 2, l // RESIDUES
    b16 = lambda t: tuple(a.astype(BF16) for a in t)
    hy_fwd = sum((_trig_tables(q, 1, 0, RESIDUES, r, 2 * l) for r in range(RESIDUES)), ())
    hy_inv = sum((_trig_tables(q, RESIDUES, r, 1, 0, 2 * l) for r in range(RESIDUES)), ())
    fce, fse = _trig_tables(half, 1, 0, 2, 0, l)
    fco, fso = _trig_tables(half, 1, 0, 2, 1, l)
    scale = 1.0 / math.sqrt(l * (c // FNET_GROUPS))
    deltas = jnp.linspace(math.log(HYENA_DECAY_TARGET) / HYENA_SLOW_DECAY,
                          math.log(HYENA_DECAY_TARGET) / HYENA_FAST_DECAY, c, dtype=F32)[None, :]
    feats = _hyena_features(l, 128)
    return {
        'hy_fwd': b16(hy_fwd), 'hy_inv': b16(hy_inv),
        'fn': b16((fce * scale, fco * scale, -fse * scale, -fso * scale)),
        'feats': jnp.concatenate([feats[r::RESIDUES] for r in range(RESIDUES)], axis=0), 'deltas': deltas,
        'pool_inv': _pool_inv_counts(l, c),
    }


def kernel(x, c, ctx, c_ctx, w_mod, b_mod, norm1_g, norm2_g, w_in, pool_w, pool_scale, hyena_conv_w,
           hyena_conv_b, hyena_filt_w1, hyena_filt_b1, hyena_filt_w2, hyena_filt_b2, hyena_filt_w3,
           hyena_filt_b3, hyena_freq, hyena_skip, na_rpb, w_branch, w_out, ffn_w_up, ffn_conv_w, ffn_conv_b,
           ffn_w_down, final_norm_g):
    batch, seq, d = x.shape
    lc = ctx.shape[1]
    depth = w_mod.shape[0]
    m = d // N_BRANCH
    f = ffn_w_down.shape[1]
    rows = seq // GRID_W
    kh = min(NA_WIN_H, rows)

    cvec = jnp.concatenate([c, c_ctx[None], jnp.zeros((16 - batch - 1, d), F32)], axis=0)
    mod = _mod_call(cvec, w_mod, b_mod)
    mod_x = mod[:, :batch].reshape(depth, batch, 6, 1, d)
    mod_c = jnp.broadcast_to(mod[:, batch:batch + 1], (depth, batch, 6 * d)).reshape(depth, batch, 6, 1, d)

    gsz = m // FNET_GROUPS
    cc, ss = _trig_tables(gsz, 1, 0, 1, 0, gsz)
    eye = jnp.eye(FNET_GROUPS, dtype=F32)
    cs = jnp.concatenate([jnp.kron(eye, cc), jnp.kron(eye, ss)], axis=1).astype(BF16)

    tabs_x = _seq_tables(seq, m)
    tabs_c = _seq_tables(lc, m)
    assert max(POOL_WINDOWS) // 2 <= PAD and batch % 2 == 0
    nb_fnet = 4 if batch % 4 == 0 else 2
    cfg_x = dict(c=m, tm=min(1024, seq), rows=256, nb=2, nb_fnet=nb_fnet, tseq=min(512, seq // 2),
                 thy=min(512, seq // RESIDUES), chunk=256, group=min(16, rows))
    cfg_c = dict(c=m, tm=lc, rows=lc, nb=2, nb_fnet=nb_fnet, tseq=lc // 2, thy=lc // RESIDUES, chunk=256)

    pool_off, fnet_off, hy_off, na_off = 0, m, 2 * m, 5 * m
    gate_off = 8 * m

    for l in range(depth):
        last = l == depth - 1
        wi = w_in[l]
        lw = {
            'norm1_g': norm1_g[l][None], 'norm2_g': norm2_g[l][None], 'final_g': final_norm_g[None],
            'wa': jnp.concatenate([wi[:, pool_off:pool_off + m], wi[:, hy_off:hy_off + 3 * m]], axis=1).astype(BF16),
            'wb': jnp.concatenate([wi[:, fnet_off:fnet_off + m], wi[:, na_off:na_off + 3 * m]], axis=1).astype(BF16),
            'cs': cs,
            'wg': wi[:, gate_off:].astype(BF16),
            'pool_blk': jax.scipy.linalg.block_diag(*[pool_w[l, gi] for gi in range(pool_w.shape[1])]).astype(BF16),
            'pool_scale': pool_scale[l][None],
            'hyena_conv_w': hyena_conv_w[l], 'hyena_conv_b': hyena_conv_b[l][None],
            'fw1': _pad2(hyena_filt_w1[l], 128, 128), 'fb1': _pad2(hyena_filt_b1[l][None], 1, 128),
            'fw2': _pad2(hyena_filt_w2[l], 128, 128), 'fb2': _pad2(hyena_filt_b2[l][None], 1, 128),
            'fw3': _pad2(hyena_filt_w3[l], 128, hyena_filt_w3.shape[2]), 'fb3': hyena_filt_b3[l][None],
            'freq': _pad2(hyena_freq[l][None], 1, 128),
            'skip': hyena_skip[l][:, None, :],
            'w_branch': w_branch[l].astype(BF16), 'w_out': w_out[l].astype(BF16),
            'wu': ffn_w_up[l][:, :f].astype(BF16), 'wgt': ffn_w_up[l][:, f:].astype(BF16),
            'ffn_conv_w': ffn_conv_w[l], 'ffn_conv_b': ffn_conv_b[l][None],
            'wd': ffn_w_down[l].astype(BF16),
        }
        bias = _attention_bias(na_rpb[l], rows, kh)

        proj_c = _mixer_inputs(ctx, mod_c[l], lw, cfg_c['tm'])
        proj_x = _mixer_inputs(x, mod_x[l], lw, cfg_x['tm'])
        y_att = _natt_call(proj_x[2], proj_c[2], bias, m, cfg_x['group'])
        x = _stream_layer(x, mod_x[l], lw, tabs_x, proj_x, y_att, cfg_x, last)
        if not last:
            ctx = _stream_layer(ctx, mod_c[l], lw, tabs_c, proj_c, _catt_call(proj_c[2], m), cfg_c, False)
    return x
```

```python
import functools
import math

import numpy as np
import jax
import jax.numpy as jnp
from jax import lax
from jax.experimental import pallas as pl
from jax.experimental.pallas import tpu as pltpu

F32 = jnp.float32
BF16 = jnp.bfloat16

GRID_W = 64
N_BRANCH = 4
POOL_WINDOWS = (2, 4, 8, 16)
FNET_GROUPS = 4
HYENA_ORDER = 2
HYENA_BANDS = 16
HYENA_DECAY_TARGET = 1e-2
HYENA_FAST_DECAY = 0.3
HYENA_SLOW_DECAY = 1.5
HYENA_DECAY_SHIFT = 0.05
NA_HEAD_DIM = 64
NA_WIN_H = 8
NA_WIN_W = 16
EPS = 1e-6
NEG_INF = -1e30

RESIDUES = 4
SQRT_HALF = math.sqrt(0.5)
LANES = 128
HALO = 16
PAD = 8
VMEM_LIMIT = 56 * 1024 * 1024


def _params(sem):
    return pltpu.CompilerParams(dimension_semantics=sem, vmem_limit_bytes=VMEM_LIMIT)


def _dot(a, b):
    return jnp.dot(a, b, preferred_element_type=F32)


def _dot_t(a, b):
    return lax.dot_general(a, b, (((1,), (1,)), ((), ())), preferred_element_type=F32)


def _split_bf16(a):
    hi = a.astype(BF16)
    lo = (a - hi.astype(F32)).astype(BF16)
    return hi, lo


def _mod_kernel(c_ref, w_ref, b_ref, o_ref):
    a = c_ref[...]
    a = a * jax.nn.sigmoid(a)
    a_hi, a_lo = _split_bf16(a)
    w_hi, w_lo = _split_bf16(w_ref[0])
    o_ref[0] = _dot(a_hi, w_hi) + _dot(a_lo, w_hi) + _dot(a_hi, w_lo) + b_ref[0]


def _mod_call(cvec, w_mod, b_mod):
    depth, d, n = w_mod.shape
    rows = cvec.shape[0]
    tn = 1024
    return pl.pallas_call(
        _mod_kernel,
        grid=(depth, n // tn),
        in_specs=[
            pl.BlockSpec((rows, d), lambda l, j: (0, 0)),
            pl.BlockSpec((1, d, tn), lambda l, j: (l, 0, j)),
            pl.BlockSpec((1, 1, tn), lambda l, j: (l, 0, j)),
        ],
        out_specs=pl.BlockSpec((1, rows, tn), lambda l, j: (l, 0, j)),
        out_shape=jax.ShapeDtypeStruct((depth, rows, n), F32),
        compiler_params=_params(("arbitrary", "arbitrary")),
        name="mod",
    )(cvec, w_mod, b_mod.reshape(depth, 1, n))


def _ada_norm(x, g, shift, scale):
    y = x * lax.rsqrt(jnp.mean(x * x, axis=-1, keepdims=True) + EPS)
    return y * (g * (1.0 + scale)) + shift


def _deinterleave(y, scr_ref, ways):
    n = y.shape[0]
    parts = [[] for _ in range(ways)]
    for cb in range(y.shape[1] // LANES):
        scr_ref[cb] = y[:, cb * LANES:(cb + 1) * LANES]
        for k in range(ways):
            parts[k].append(scr_ref[cb, pl.ds(k, n // ways, stride=ways), :])
    return [jnp.concatenate(p, axis=1) for p in parts]


def _interleave(parts, scr_ref):
    ways = len(parts)
    n = parts[0].shape[0]
    cols = []
    for cb in range(parts[0].shape[1] // LANES):
        for k in range(ways):
            scr_ref[cb, pl.ds(k, n, stride=ways), :] = parts[k][:, cb * LANES:(cb + 1) * LANES]
        cols.append(scr_ref[cb])
    return jnp.concatenate(cols, axis=1)


def _proj_kernel(x_ref, sh_ref, sc_ref, g_ref, wa_ref, wb_ref, oa_ref, oh_ref, ob_ref, of_ref, scr_ref):
    m = oa_ref.shape[2]
    h = _ada_norm(x_ref[0], g_ref[...], sh_ref[0, 0], sc_ref[0, 0]).astype(BF16)
    pa = _dot(h, wa_ref[...])
    oa_ref[0] = pa[:, :m]
    oh_ref[0] = pa[:, m:].astype(BF16)
    pb = _dot(h, wb_ref[...])
    ob_ref[0] = pb[:, m:].astype(BF16)
    for k, part in enumerate(_deinterleave(pb[:, :m], scr_ref, RESIDUES)):
        of_ref[0, k] = part.astype(BF16)


def _proj_call(x, mod4, g, wa, wb, m, tm):
    b, l, d = x.shape
    na, nb = wa.shape[1], wb.shape[1]
    const = lambda bi, i: (0, 0)
    return pl.pallas_call(
        _proj_kernel,
        grid=(b, l // tm),
        in_specs=[
            pl.BlockSpec((1, tm, d), lambda bi, i: (bi, i, 0)),
            pl.BlockSpec((1, 1, 1, d), lambda bi, i: (bi, 0, 0, 0)),
            pl.BlockSpec((1, 1, 1, d), lambda bi, i: (bi, 1, 0, 0)),
            pl.BlockSpec((1, d), const),
            pl.BlockSpec((d, na), const),
            pl.BlockSpec((d, nb), const),
        ],
        out_specs=[
            pl.BlockSpec((1, tm, m), lambda bi, i: (bi, i, 0)),
            pl.BlockSpec((1, tm, na - m), lambda bi, i: (bi, i, 0)),
            pl.BlockSpec((1, tm, nb - m), lambda bi, i: (bi, i, 0)),
            pl.BlockSpec((1, RESIDUES, tm // RESIDUES, m), lambda bi, i: (bi, 0, i, 0)),
        ],
        out_shape=[
            jax.ShapeDtypeStruct((b, l, m), F32),
            jax.ShapeDtypeStruct((b, l, na - m), BF16),
            jax.ShapeDtypeStruct((b, l, nb - m), BF16),
            jax.ShapeDtypeStruct((b, RESIDUES, l // RESIDUES, m), BF16),
        ],
        scratch_shapes=[pltpu.VMEM((m // LANES, tm, LANES), F32)],
        compiler_params=_params(("parallel", "parallel")),
        name="proj",
    )(x, mod4, mod4, g, wa, wb)


def _fill_padded(src_ref, pad_ref, l, rows):
    c = pad_ref.shape[1]
    pad_ref[0:PAD, :] = jnp.zeros((PAD, c), F32)
    pad_ref[l + PAD:l + 2 * PAD, :] = jnp.zeros((PAD, c), F32)

    def copy(i, carry):
        r = pl.multiple_of(i * rows, rows)
        pad_ref[pl.ds(r + PAD, rows), :] = src_ref[0, pl.ds(r, rows), :].astype(F32)
        return carry

    lax.fori_loop(0, l // rows, copy, 0)


def _centred_window_sums(u, windows):
    n = u.shape[0]
    fwd = {1: u}
    w = 1
    while w < max(windows):
        fwd[2 * w] = fwd[w] + pltpu.roll(fwd[w], n - w, 0)
        w *= 2
    return {w: pltpu.roll(fwd[w], w // 2, 0) for w in windows}


def _pool_kernel(u_ref, inv_ref, w_ref, s_ref, o_ref, pad_ref, *, rows):
    l, c = u_ref.shape[1], u_ref.shape[2]
    gw = c // len(POOL_WINDOWS)
    per_block = LANES // gw
    _fill_padded(u_ref, pad_ref, l, rows)
    n = rows + 2 * PAD
    lane = lax.broadcasted_iota(jnp.int32, (n, LANES), 1)
    w = w_ref[...]
    scale = s_ref[...]

    def body(i, carry):
        r = pl.multiple_of(i * rows, rows)
        cols = []
        for cb in range(c // LANES):
            lanes = slice(cb * LANES, (cb + 1) * LANES)
            win = pad_ref[pl.ds(r, n), lanes]
            wins = POOL_WINDOWS[cb * per_block:(cb + 1) * per_block]
            sums = _centred_window_sums(win, wins)
            s = sums[wins[-1]]
            for k in range(per_block - 2, -1, -1):
                s = jnp.where(lane < (k + 1) * gw, sums[wins[k]], s)
            cols.append((s * inv_ref[pl.ds(r, n), lanes] - win)[PAD:PAD + rows])
        y = jnp.concatenate(cols, axis=1)
        o_ref[0, pl.ds(r, rows), :] = (_dot(y.astype(BF16), w) * scale).astype(BF16)
        return carry

    lax.fori_loop(0, l // rows, body, 0)


def _pool_inv_counts(l, c):
    gw = c // len(POOL_WINDOWS)
    t = jnp.arange(-PAD, l + PAD, dtype=jnp.int32)[:, None]
    half = jnp.asarray(np.repeat(np.array(POOL_WINDOWS) // 2, gw), jnp.int32)[None, :]
    cnt = jnp.minimum(t + half, l) - jnp.maximum(t - half, 0)
    return 1.0 / jnp.maximum(cnt, 1).astype(F32)


def _pool_call(oa, inv_cnt, w_blk, scale, rows):
    b, l, _ = oa.shape
    c = w_blk.shape[0]
    return pl.pallas_call(
        functools.partial(_pool_kernel, rows=rows),
        grid=(b,),
        in_specs=[
            pl.BlockSpec((1, l, c), lambda bi: (bi, 0, 0)),
            pl.BlockSpec((l + 2 * PAD, c), lambda bi: (0, 0)),
            pl.BlockSpec((c, c), lambda bi: (0, 0)),
            pl.BlockSpec((1, c), lambda bi: (0, 0)),
        ],
        out_specs=pl.BlockSpec((1, l, c), lambda bi: (bi, 0, 0)),
        out_shape=jax.ShapeDtypeStruct((b, l, c), BF16),
        scratch_shapes=[pltpu.VMEM((l + 2 * PAD, c), F32)],
        compiler_params=_params(("parallel",)),
        name="pool",
    )(oa, inv_cnt, w_blk, scale)


def _dwconv_kernel(u_ref, w_ref, b_ref, o_ref, pad_ref, scr_ref, *, rows):
    l = u_ref.shape[1]
    _fill_padded(u_ref, pad_ref, l, rows)
    n = rows + 2 * PAD
    sub = rows // RESIDUES
    w = w_ref[...]
    bias = b_ref[...]

    def body(i, carry):
        r = pl.multiple_of(i * rows, rows)
        rs = pl.multiple_of(i * sub, sub)
        win = pad_ref[pl.ds(r, n), :]
        y = pltpu.roll(win, 1, 0) * w[0:1] + win * w[1:2] + pltpu.roll(win, n - 1, 0) * w[2:3] + bias
        for k, part in enumerate(_deinterleave(y[PAD:PAD + rows], scr_ref, RESIDUES)):
            o_ref[0, k, pl.ds(rs, sub), :] = part.astype(BF16)
        return carry

    lax.fori_loop(0, l // rows, body, 0)


def _dwconv_call(u, w, bias, c, rows):
    b, l, _ = u.shape
    nblk = w.shape[1] // c
    return pl.pallas_call(
        functools.partial(_dwconv_kernel, rows=rows),
        grid=(b, nblk),
        in_specs=[
            pl.BlockSpec((1, l, c), lambda bi, j: (bi, 0, j)),
            pl.BlockSpec((w.shape[0], c), lambda bi, j: (0, j)),
            pl.BlockSpec((1, c), lambda bi, j: (0, j)),
        ],
        out_specs=pl.BlockSpec((1, RESIDUES, l // RESIDUES, c), lambda bi, j: (bi, 0, 0, j)),
        out_shape=jax.ShapeDtypeStruct((b, RESIDUES, l // RESIDUES, nblk * c), BF16),
        scratch_shapes=[pltpu.VMEM((l + 2 * PAD, c), F32), pltpu.VMEM((c // LANES, rows, LANES), F32)],
        compiler_params=_params(("parallel", "parallel")),
        name="dwconv",
    )(u, w, bias)


def _filt_kernel(f_ref, w1_ref, b1_ref, w2_ref, b2_ref, w3_ref, b3_ref, fr_ref, dl_ref,
                 hb_ref, row0_ref, asum_ref):
    hp = lax.Precision.HIGHEST
    feats = f_ref[...]
    freq = fr_ref[...]
    h = jnp.sin(freq * (jnp.dot(feats, w1_ref[...], precision=hp, preferred_element_type=F32) + b1_ref[...]))
    h = jnp.sin(freq * (jnp.dot(h, w2_ref[...], precision=hp, preferred_element_type=F32) + b2_ref[...]))
    h = jnp.dot(h, w3_ref[...], precision=hp, preferred_element_type=F32) + b3_ref[...]
    t = feats[:, 0:1]
    win = jnp.exp(-t * jnp.abs(dl_ref[...])) + HYENA_DECAY_SHIFT
    h = h * jnp.concatenate([win] * (h.shape[1] // win.shape[1]), axis=1)
    hb = h.astype(BF16)
    hb_ref[...] = hb
    part = jnp.sum(jnp.abs(h), axis=0, keepdims=True)

    @pl.when(pl.program_id(0) == 0)
    def _():
        asum_ref[...] = jnp.zeros_like(asum_ref)
        row0_ref[...] = hb[0:16].astype(F32)[0:8]

    asum_ref[...] += part


def _filt_call(feats, w1, b1, w2, b2, w3, b3, freq, deltas, rows):
    l, fd = feats.shape
    hd = w2.shape[0]
    n = w3.shape[1]
    c = deltas.shape[1]
    const = lambda i: (0, 0)
    return pl.pallas_call(
        _filt_kernel,
        grid=(l // rows,),
        in_specs=[
            pl.BlockSpec((rows, fd), lambda i: (i, 0)),
            pl.BlockSpec((fd, hd), const), pl.BlockSpec((1, hd), const),
            pl.BlockSpec((hd, hd), const), pl.BlockSpec((1, hd), const),
            pl.BlockSpec((hd, n), const), pl.BlockSpec((1, n), const),
            pl.BlockSpec((1, hd), const), pl.BlockSpec((1, c), const),
        ],
        out_specs=[
            pl.BlockSpec((rows, n), lambda i: (i, 0)),
            pl.BlockSpec((8, n), const),
            pl.BlockSpec((1, n), const),
        ],
        out_shape=[
            jax.ShapeDtypeStruct((l, n), BF16),
            jax.ShapeDtypeStruct((8, n), F32),
            jax.ShapeDtypeStruct((1, n), F32),
        ],
        compiler_params=_params(("arbitrary",)),
        name="hyena_filt",
    )(feats, w1, b1, w2, b2, w3, b3, freq, deltas)


def _alt_signs(n):
    t = lax.broadcasted_iota(jnp.int32, (16, n), 1)
    return jnp.where((t & 1) == 0, 1.0, -1.0).astype(BF16)


def _mat_specs(tm, tables):
    _, a, cols = tables[0].shape
    return [pl.BlockSpec((tm // a, a, cols), lambda g, i: (i, 0, 0)) for _ in tables]


def _mat(ref):
    return ref[...].reshape(ref.shape[0] * ref.shape[1], ref.shape[2])


def _quad_transform(mat_refs, parts):
    a = [_dot(_mat(mat_refs[2 * r]), parts[r]) for r in range(RESIDUES)]
    b = [_dot(_mat(mat_refs[2 * r + 1]), parts[r]) for r in range(RESIDUES)]
    ea, fa, ga, ha = a[0] + a[2], a[0] - a[2], a[1] + a[3], a[1] - a[3]
    eb, fb, gb, hb = b[0] + b[2], b[0] - b[2], b[1] + b[3], b[1] - b[3]
    zr = (ea + ga, fa + hb, fa - hb, ea - ga)
    zs = (eb + gb, ha - fb, ha + fb, gb - eb)
    return zr, zs


def _quad_special(alt):
    c1, c3 = SQRT_HALF * alt[1], SQRT_HALF * alt[3]
    return (alt[0] + c1 - c3, c1 + alt[2] + c3), (alt[0] - c1 + c3, c1 - alt[2] + c3)


def _kspec_kernel(*refs):
    mat_refs, (h_ref, row0_ref, asum_ref, kk_ref, km_ref) = refs[:8], refs[8:]
    i = pl.program_id(1)
    tm, q = kk_ref.shape[2], mat_refs[0].shape[2]
    c = kk_ref.shape[3]
    l = RESIDUES * q
    parts = [h_ref[r * q:(r + 1) * q, :] for r in range(RESIDUES)]
    zr, zs = _quad_transform(mat_refs, parts)
    asum = asum_ref[...]
    inv = 1.0 / (asum[:, :c] + asum[:, c:] + EPS)
    hb0 = row0_ref[0:1, c:]
    row = lax.broadcasted_iota(jnp.int32, (tm, c), 0) + i * tm
    wj = jnp.where(row == 0, 0.5 / l, 1.0 / l) * inv
    for f in range(4):
        kk_ref[0, 2 * f] = (zr[f][:, :c] + zr[f][:, c:] - hb0) * wj
        kk_ref[0, 2 * f + 1] = (zs[f][:, :c] - zs[f][:, c:]) * wj

    @pl.when(i == 0)
    def _():
        sg = _alt_signs(q)
        alt = [_dot(sg, p)[0:8] for p in parts]
        for f, (xr, xs) in enumerate(_quad_special(alt)):
            km_ref[0, :, 2 * f * c:(2 * f + 1) * c] = (xr[:, :c] + xr[:, c:] - hb0) * inv * (1.0 / l)
            km_ref[0, :, (2 * f + 1) * c:(2 * f + 2) * c] = (xs[:, :c] - xs[:, c:]) * inv * (1.0 / l)


def _kspec_call(mats, h, row0, asum, c, tm):
    q = mats[0].shape[2]
    l = RESIDUES * q
    orders = h.shape[1] // (2 * c)
    return pl.pallas_call(
        _kspec_kernel,
        grid=(orders, q // tm),
        in_specs=_mat_specs(tm, mats) + [
            pl.BlockSpec((l, 2 * c), lambda o, i: (0, o)),
            pl.BlockSpec((8, 2 * c), lambda o, i: (0, o)),
            pl.BlockSpec((1, 2 * c), lambda o, i: (0, o)),
        ],
        out_specs=[
            pl.BlockSpec((1, 8, tm, c), lambda o, i: (o, 0, i, 0)),
            pl.BlockSpec((1, 8, 4 * c), lambda o, i: (o, 0, 0)),
        ],
        out_shape=[
            jax.ShapeDtypeStruct((orders, 8, q, c), F32),
            jax.ShapeDtypeStruct((orders, 8, 4 * c), F32),
        ],
        compiler_params=_params(("arbitrary", "arbitrary")),
        name="hyena_kspec",
    )(*mats, h, row0, asum)


def _cmul(zr, zs, kr, ks):
    return zr * kr - zs * ks, zr * ks + zs * kr


def _hfwd_kernel(*refs):
    mat_refs, (z_ref, kk_ref, km_ref, spec_ref, ym_ref) = refs[:8], refs[8:]
    nb = z_ref.shape[0]
    c = z_ref.shape[3]
    for bb in range(nb):
        zr, zs = _quad_transform(mat_refs, [z_ref[bb, r] for r in range(RESIDUES)])
        yr, ys = zip(*[_cmul(zr[f], zs[f], kk_ref[0, 2 * f], kk_ref[0, 2 * f + 1]) for f in range(4)])
        a, b, u1, u2 = yr[0] + yr[3], yr[0] - yr[3], yr[1] + yr[2], ys[1] + ys[2]
        cc, dd, d1, d2 = ys[0] - ys[3], ys[0] + ys[3], yr[1] - yr[2], ys[2] - ys[1]
        for k, v in enumerate((a + u1, cc + d2, b + u2, dd + d1, a - u1, cc - d2, b - u2, dd - d1)):
            spec_ref[bb, k] = v.astype(BF16)

    @pl.when(pl.program_id(1) == 0)
    def _():
        sg = _alt_signs(z_ref.shape[2])
        for bb in range(nb):
            alt = [_dot(sg, z_ref[bb, r])[0:8] for r in range(RESIDUES)]
            (zq_r, zq_s), (z3_r, z3_s) = _quad_special(alt)
            yq_r, yq_s = _cmul(zq_r, zq_s, km_ref[0, :, 0:c], km_ref[0, :, c:2 * c])
            y3_r, y3_s = _cmul(z3_r, z3_s, km_ref[0, :, 2 * c:3 * c], km_ref[0, :, 3 * c:4 * c])
            ym_ref[bb, :, 0:c] = yq_r + y3_r
            ym_ref[bb, :, c:2 * c] = SQRT_HALF * (yq_r + yq_s - y3_r + y3_s)
            ym_ref[bb, :, 2 * c:3 * c] = yq_s - y3_s
            ym_ref[bb, :, 3 * c:4 * c] = SQRT_HALF * (yq_s - yq_r + y3_r + y3_s)


def _hfwd_call(mats, zb, kk, km, order, c, nb, tm):
    b, _, q, _ = zb.shape
    return pl.pallas_call(
        _hfwd_kernel,
        grid=(b // nb, q // tm),
        in_specs=_mat_specs(tm, mats) + [
            pl.BlockSpec((nb, RESIDUES, q, c), lambda g, i: (g, 0, 0, 0)),
            pl.BlockSpec((1, 8, tm, c), lambda g, i: (order, 0, i, 0)),
            pl.BlockSpec((1, 8, 4 * c), lambda g, i: (order, 0, 0)),
        ],
        out_specs=[
            pl.BlockSpec((nb, 8, tm, c), lambda g, i: (g, 0, i, 0)),
            pl.BlockSpec((nb, 8, 4 * c), lambda g, i: (g, 0, 0)),
        ],
        out_shape=[
            jax.ShapeDtypeStruct((b, 8, q, c), BF16),
            jax.ShapeDtypeStruct((b, 8, 4 * c), F32),
        ],
        compiler_params=_params(("arbitrary", "arbitrary")),
        name="hyena_fwd",
    )(*mats, zb, kk, km)


def _hinv_kernel(*refs, natural_out):
    mat_refs, (spec_ref, ym_ref, z_ref, gate_ref, skip_ref), out_refs = refs[:8], refs[8:13], refs[13:]
    nb, _, tm, c = z_ref.shape
    row = lax.broadcasted_iota(jnp.int32, (tm, c), 0)
    even_row = (row & 1) == 0
    skip = skip_ref[0]
    for bb in range(nb):
        outs = []
        for r in range(RESIDUES):
            m = ym_ref[bb, 0:1, r * c:(r + 1) * c]
            y = (_dot(_mat(mat_refs[2 * r]), spec_ref[bb, 2 * r]) + _dot(_mat(mat_refs[2 * r + 1]), spec_ref[bb, 2 * r + 1])
                 + jnp.where(even_row, m, -m))
            outs.append(gate_ref[bb, r].astype(F32) * (y + skip * z_ref[bb, r].astype(F32)))
        if natural_out:
            o_ref, scr_ref = out_refs
            o_ref[bb] = _interleave(outs, scr_ref).astype(BF16)
        else:
            (o_ref,) = out_refs
            for r in range(RESIDUES):
                o_ref[bb, r] = outs[r].astype(BF16)


def _hinv_call(mats, spec, ym, zf, zcol, uc, gcol, skip, order, natural_out, c, nb, tm):
    b, _, q, _ = spec.shape
    if natural_out:
        out_specs = pl.BlockSpec((nb, RESIDUES * tm, c), lambda g, i: (g, i, 0))
        out_shape = jax.ShapeDtypeStruct((b, RESIDUES * q, c), BF16)
        scratch = [pltpu.VMEM((c // LANES, RESIDUES * tm, LANES), F32)]
    else:
        out_specs = pl.BlockSpec((nb, RESIDUES, tm, c), lambda g, i: (g, 0, i, 0))
        out_shape = jax.ShapeDtypeStruct((b, RESIDUES, q, c), BF16)
        scratch = []
    return pl.pallas_call(
        functools.partial(_hinv_kernel, natural_out=natural_out),
        grid=(b // nb, q // tm),
        in_specs=_mat_specs(tm, mats) + [
            pl.BlockSpec((nb, 8, q, c), lambda g, i: (g, 0, 0, 0)),
            pl.BlockSpec((nb, 8, 4 * c), lambda g, i: (g, 0, 0)),
            pl.BlockSpec((nb, RESIDUES, tm, c), lambda g, i: (g, 0, i, zcol)),
            pl.BlockSpec((nb, RESIDUES, tm, c), lambda g, i: (g, 0, i, gcol)),
            pl.BlockSpec((1, 1, c), lambda g, i: (order, 0, 0)),
        ],
        out_specs=out_specs,
        out_shape=out_shape,
        scratch_shapes=scratch,
        compiler_params=_params(("arbitrary", "arbitrary")),
        name="hyena_inv",
    )(*mats, spec, ym, zf, uc, skip)


def _fnet_kernel(*refs):
    mat_refs, (u_ref, cs_ref, o_ref) = refs[:8], refs[8:]
    nb = u_ref.shape[0]
    cs = cs_ref[...]
    for bb in range(nb):
        parts = [u_ref[bb, r] for r in range(RESIDUES)]
        a = [_dot(_mat(mat_refs[2 * r]), parts[r]) for r in range(RESIDUES)]
        b = [_dot(_mat(mat_refs[2 * r + 1]), parts[r]) for r in range(RESIDUES)]
        ea, fa, ga, ha = a[0] + a[2], a[0] - a[2], a[1] + a[3], a[1] - a[3]
        eb, fb, gb, hb = b[0] + b[2], b[0] - b[2], b[1] + b[3], b[1] - b[3]
        uc = (ea + ga, fa - hb, ea - ga, fa + hb)
        us = (eb + gb, fb + ha, eb - gb, fb - ha)
        for m in range(RESIDUES):
            both = jnp.concatenate([uc[m].astype(BF16), us[m].astype(BF16)], axis=1)
            o_ref[bb, m] = _dot(both, cs).astype(BF16)


def _fnet_call(mats, u4, cs, nb, tm):
    b, _, q, c = u4.shape
    out = pl.pallas_call(
        _fnet_kernel,
        grid=(b // nb, q // tm),
        in_specs=_mat_specs(tm, mats) + [
            pl.BlockSpec((nb, RESIDUES, q, c), lambda g, i: (g, 0, 0, 0)),
            pl.BlockSpec(cs.shape, lambda g, i: (0, 0)),
        ],
        out_specs=pl.BlockSpec((nb, RESIDUES, tm, c), lambda g, i: (g, 0, i, 0)),
        out_shape=jax.ShapeDtypeStruct((b, RESIDUES, q, c), BF16),
        compiler_params=_params(("arbitrary", "arbitrary")),
        name="fnet",
    )(*mats, u4, cs)
    return out.reshape(b, RESIDUES * q, c)


def _stack_heads(q, heads):
    lane = lax.broadcasted_iota(jnp.int32, q.shape, 1)
    zero = jnp.zeros_like(q)
    return jnp.concatenate([jnp.where(lax.shift_right_logical(lane, int(math.log2(NA_HEAD_DIM))) == h, q, zero) for h in range(heads)], axis=0)


def _merge_heads(o, heads, n):
    lane = lax.broadcasted_iota(jnp.int32, (n, o.shape[1]), 1)
    out = jnp.zeros((n, o.shape[1]), F32)
    for h in range(heads):
        out = out + jnp.where(lax.shift_right_logical(lane, int(math.log2(NA_HEAD_DIM))) == h, o[h * n:(h + 1) * n], 0.0)
    return out


def _natt_kernel(q_ref, k_ref, v_ref, kc_ref, vc_ref, bias_ref, o_ref, *, rows, kh, group):
    w = GRID_W
    heads = q_ref.shape[2] // NA_HEAD_DIM
    base = pl.program_id(1) * group
    lo = kh // 2
    kc, vc = kc_ref[0], vc_ref[0]
    scale = jnp.asarray(NA_HEAD_DIM ** -0.5, BF16)

    def body(j, carry):
        r = base + j
        r0 = jnp.clip(r - lo, 0, rows - kh)
        start = pl.multiple_of(r0 * w, w)
        cls = jnp.minimum(r, lo) + jnp.maximum(r - (rows - kh + lo), 0)
        q4 = _stack_heads(q_ref[0, pl.ds(pl.multiple_of(j * w, w), w), :] * scale, heads)
        ks = k_ref[0, pl.ds(start, kh * w), :]
        vs = v_ref[0, pl.ds(start, kh * w), :]
        s_nb = _dot_t(q4, ks) + bias_ref[cls]
        s_cx = _dot_t(q4, kc)
        m = jnp.maximum(jnp.max(s_nb, axis=-1, keepdims=True), jnp.max(s_cx, axis=-1, keepdims=True))
        p_nb = jnp.exp(s_nb - m)
        p_cx = jnp.exp(s_cx - m)
        den = jnp.sum(p_nb, axis=-1, keepdims=True) + jnp.sum(p_cx, axis=-1, keepdims=True)
        o4 = (_dot(p_nb.astype(BF16), vs) + _dot(p_cx.astype(BF16), vc)) / den
        o_ref[0, pl.ds(pl.multiple_of(j * w, w), w), :] = _merge_heads(o4, heads, w).astype(BF16)
        return carry

    lax.fori_loop(0, group, body, 0, unroll=4)


def _natt_call(ob, obc, bias, c, group):
    b, l, _ = ob.shape
    lc = obc.shape[1]
    w = GRID_W
    rows = l // w
    kh = min(NA_WIN_H, rows)
    return pl.pallas_call(
        functools.partial(_natt_kernel, rows=rows, kh=kh, group=group),
        grid=(b, rows // group),
        in_specs=[
            pl.BlockSpec((1, group * w, c), lambda bi, r: (bi, r, 0)),
            pl.BlockSpec((1, l, c), lambda bi, r: (bi, 0, 1)),
            pl.BlockSpec((1, l, c), lambda bi, r: (bi, 0, 2)),
            pl.BlockSpec((1, lc, c), lambda bi, r: (bi, 0, 1)),
            pl.BlockSpec((1, lc, c), lambda bi, r: (bi, 0, 2)),
            pl.BlockSpec(bias.shape, lambda bi, r: (0, 0, 0)),
        ],
        out_specs=pl.BlockSpec((1, group * w, c), lambda bi, r: (bi, r, 0)),
        out_shape=jax.ShapeDtypeStruct((b, l, c), BF16),
        compiler_params=_params(("parallel", "arbitrary")),
        name="nb_attn",
    )(ob, ob, ob, obc, obc, bias)


def _bias_kernel(r_ref, o_ref):
    w = GRID_W
    shift = int(math.log2(w))
    n = o_ref.shape[1]
    r = r_ref[...]
    hi = r.astype(BF16)
    mid = (r - hi.astype(F32)).astype(BF16)
    lo = (r - hi.astype(F32) - mid.astype(F32)).astype(BF16)
    dc = lax.broadcasted_iota(jnp.int32, (r.shape[1], n), 0)
    p = lax.broadcasted_iota(jnp.int32, (r.shape[1], n), 1)
    idx = jnp.clip((p & (w - 1)) - lax.shift_right_logical(p, shift), 1 - NA_WIN_W, NA_WIN_W - 1) + (NA_WIN_W - 1)
    onehot = jnp.where(dc == idx, 1.0, 0.0).astype(BF16)
    val = _dot(hi, onehot) + _dot(mid, onehot) + _dot(lo, onehot)
    po = lax.broadcasted_iota(jnp.int32, val.shape, 1)
    cq = lax.shift_right_logical(po, shift)
    ck = po & (w - 1)
    c0 = jnp.clip(cq - NA_WIN_W // 2, 0, w - NA_WIN_W)
    inside = jnp.logical_and(ck >= c0, ck < c0 + NA_WIN_W)
    o_ref[...] = jnp.where(inside, val, NEG_INF)


def _bias_call(rpb2d):
    rows, cols = rpb2d.shape
    n = GRID_W * GRID_W
    return pl.pallas_call(
        _bias_kernel,
        grid=(1,),
        in_specs=[pl.BlockSpec((rows, cols), lambda i: (0, 0))],
        out_specs=pl.BlockSpec((rows, n), lambda i: (0, 0)),
        out_shape=jax.ShapeDtypeStruct((rows, n), F32),
        compiler_params=_params(("arbitrary",)),
        name="rpb_table",
    )(rpb2d)


def _catt_kernel(q_ref, k_ref, v_ref, o_ref):
    n = q_ref.shape[1]
    heads = q_ref.shape[2] // NA_HEAD_DIM
    q = q_ref[0] * jnp.asarray(NA_HEAD_DIM ** -0.5, BF16)
    q4 = _stack_heads(q, heads)
    s = _dot_t(q4, k_ref[0])
    p = jnp.exp(s - jnp.max(s, axis=-1, keepdims=True))
    den = jnp.sum(p, axis=-1, keepdims=True)
    o4 = _dot(p.astype(BF16), v_ref[0]) / den
    o_ref[0] = _merge_heads(o4, heads, n).astype(BF16)


def _catt_call(obc, c):
    b, lc, _ = obc.shape
    return pl.pallas_call(
        _catt_kernel,
        grid=(b,),
        in_specs=[pl.BlockSpec((1, lc, c), lambda bi, j=j: (bi, 0, j)) for j in range(3)],
        out_specs=pl.BlockSpec((1, lc, c), lambda bi: (bi, 0, 0)),
        out_shape=jax.ShapeDtypeStruct((b, lc, c), BF16),
        compiler_params=_params(("parallel",)),
        name="ctx_attn",
    )(obc, obc, obc)


def _merge_kernel(x_ref, sh_ref, sc_ref, gt_ref, g_ref, y0_ref, y1_ref, y2_ref, y3_ref,
                  wg_ref, wb_ref, wo_ref, o_ref):
    d = x_ref.shape[2]
    x = x_ref[0]
    h = _ada_norm(x, g_ref[...], sh_ref[0, 0], sc_ref[0, 0]).astype(BF16)
    merged = None
    for bi, y_ref in enumerate((y0_ref, y1_ref, y2_ref, y3_ref)):
        gate = jax.nn.sigmoid(_dot(h, wg_ref[:, bi * d:(bi + 1) * d]))
        term = gate * _dot(y_ref[0].astype(BF16), wb_ref[bi])
        merged = term if merged is None else merged + term
    o_ref[0] = x + gt_ref[0, 0] * _dot(merged.astype(BF16), wo_ref[...])


def _merge_call(x, mod4, g, ys, wg, wb, wo, tm):
    b, l, d = x.shape
    c = ys[0].shape[2]
    const2 = lambda bi, i: (0, 0)
    mod_spec = lambda k: pl.BlockSpec((1, 1, 1, d), lambda bi, i: (bi, k, 0, 0))
    return pl.pallas_call(
        _merge_kernel,
        grid=(b, l // tm),
        in_specs=[
            pl.BlockSpec((1, tm, d), lambda bi, i: (bi, i, 0)),
            mod_spec(0), mod_spec(1), mod_spec(2),
            pl.BlockSpec((1, d), const2),
        ] + [pl.BlockSpec((1, tm, c), lambda bi, i: (bi, i, 0)) for _ in ys] + [
            pl.BlockSpec(wg.shape, const2, pipeline_mode=pl.Buffered(1)),
            pl.BlockSpec(wb.shape, lambda bi, i: (0, 0, 0), pipeline_mode=pl.Buffered(1)),
            pl.BlockSpec(wo.shape, const2, pipeline_mode=pl.Buffered(1)),
        ],
        out_specs=pl.BlockSpec((1, tm, d), lambda bi, i: (bi, i, 0)),
        out_shape=jax.ShapeDtypeStruct((b, l, d), F32),
        compiler_params=_params(("parallel", "parallel")),
        name="merge",
    )(x, mod4, mod4, mod4, g, *ys, wg, wb, wo)


def _ffn_kernel(x_ref, xp_ref, xn_ref, sh_ref, sc_ref, gt_ref, g_ref, wu_ref, wgt_ref, cw_ref, cb_ref,
                wd_ref, fg_ref, o_ref, a_ref, *, chunk, final_norm):
    i = pl.program_id(1)
    last = pl.num_programs(1) - 1
    tm = x_ref.shape[1]
    f = wu_ref.shape[1]
    n = tm + 2 * HALO
    g, sh, sc = g_ref[...], sh_ref[0, 0], sc_ref[0, 0]
    x = x_ref[0]
    h = _ada_norm(x, g, sh, sc).astype(BF16)
    hp = _ada_norm(xp_ref[0], g, sh, sc).astype(BF16)
    hn = _ada_norm(xn_ref[0], g, sh, sc).astype(BF16)
    hext = jnp.concatenate([hp, h, hn], axis=0)
    row = lax.broadcasted_iota(jnp.int32, (n, chunk), 0)
    valid = jnp.logical_and(jnp.logical_or(row >= HALO, i > 0), jnp.logical_or(row < HALO + tm, i < last))
    for j in range(f // chunk):
        cols = slice(j * chunk, (j + 1) * chunk)
        gp = jnp.where(valid, _dot(hext, wgt_ref[:, cols]), 0.0)
        cw = cw_ref[:, cols]
        gc = (pltpu.roll(gp, 1, 0) * cw[0:1] + gp * cw[1:2] + pltpu.roll(gp, n - 1, 0) * cw[2:3]
              + cb_ref[:, cols])[HALO:HALO + tm]
        u = _dot(h, wu_ref[:, cols])
        a_ref[:, cols] = (gc * jax.nn.sigmoid(gc) * u).astype(BF16)
    y = x + gt_ref[0, 0] * _dot(a_ref[...], wd_ref[...])
    if final_norm:
        y = y * lax.rsqrt(jnp.mean(y * y, axis=-1, keepdims=True) + EPS) * fg_ref[...]
    o_ref[0] = y


def _ffn_call(x, mod4, g, wu, wgt, cw, cb, wd, fg, tm, chunk, final_norm):
    b, l, d = x.shape
    f = wu.shape[1]
    hb = tm // HALO
    nblk = l // HALO
    const2 = lambda bi, i: (0, 0)
    mod_spec = lambda k: pl.BlockSpec((1, 1, 1, d), lambda bi, i: (bi, k, 0, 0))
    single = dict(pipeline_mode=pl.Buffered(1))
    return pl.pallas_call(
        functools.partial(_ffn_kernel, chunk=chunk, final_norm=final_norm),
        grid=(b, l // tm),
        in_specs=[
            pl.BlockSpec((1, tm, d), lambda bi, i: (bi, i, 0)),
            pl.BlockSpec((1, HALO, d), lambda bi, i: (bi, jnp.maximum(i * hb - 1, 0), 0)),
            pl.BlockSpec((1, HALO, d), lambda bi, i: (bi, jnp.minimum((i + 1) * hb, nblk - 1), 0)),
            mod_spec(3), mod_spec(4), mod_spec(5),
            pl.BlockSpec((1, d), const2),
            pl.BlockSpec((d, f), const2, **single),
            pl.BlockSpec((d, f), const2, **single),
            pl.BlockSpec((cw.shape[0], f), const2),
            pl.BlockSpec((1, f), const2),
            pl.BlockSpec((f, d), const2, **single),
            pl.BlockSpec((1, d), const2),
        ],
        out_specs=pl.BlockSpec((1, tm, d), lambda bi, i: (bi, i, 0)),
        out_shape=jax.ShapeDtypeStruct((b, l, d), F32),
        scratch_shapes=[pltpu.VMEM((tm, f), BF16)],
        compiler_params=_params(("parallel", "arbitrary")),
        name="ffn",
    )(x, x, x, mod4, mod4, mod4, g, wu, wgt, cw, cb, wd, fg)


def _trig_tables(n, row_mul, row_add, col_mul, col_add, period):
    a_sz = 64 if n % 64 == 0 else 1
    q = (jnp.arange(n, dtype=jnp.int32) * col_mul + col_add)[None, :]
    ra = (jnp.arange(n // a_sz, dtype=jnp.int32) * (a_sz * row_mul))[:, None]
    rb = (jnp.arange(a_sz, dtype=jnp.int32) * row_mul + row_add)[:, None]
    w = 2.0 * math.pi / period
    ang_a = ((ra * q) % period).astype(F32) * w
    ang_b = ((rb * q) % period).astype(F32) * w
    ca, sa = jnp.cos(ang_a)[:, None, :], jnp.sin(ang_a)[:, None, :]
    cb, sb = jnp.cos(ang_b)[None, :, :], jnp.sin(ang_b)[None, :, :]
    return ca * cb - sa * sb, sa * cb + ca * sb


def _pad2(a, rows, cols):
    return jnp.pad(a, ((0, rows - a.shape[0]), (0, cols - a.shape[1])))


def _hyena_features(l, pad_to):
    t = jnp.linspace(0.0, 1.0, l, dtype=F32)[:, None]
    bands = jnp.linspace(1e-4, HYENA_BANDS - 1, HYENA_BANDS, dtype=F32)[None, :]
    ang = (2.0 * math.pi / l) * jnp.arange(l, dtype=F32)[:, None] * bands
    feats = jnp.concatenate([t, jnp.cos(ang), -jnp.sin(ang)], axis=-1)
    return jnp.pad(feats, ((0, 0), (0, pad_to - feats.shape[1])))


def _bias_classes(rows, kh):
    r = np.arange(rows)
    r0 = np.clip(r - kh // 2, 0, rows - kh)
    off = r0 - r
    lo = kh // 2
    reps = list(range(lo)) + [lo] + list(range(rows - kh + lo + 1, rows))
    return [int(off[i]) for i in reps]


def _attention_bias(rpb, rows, kh):
    heads, nr, nc = rpb.shape
    w = GRID_W
    flat = rpb.astype(F32).reshape(heads * nr, nc)
    table = _bias_call(_pad2(flat, -(-heads * nr // 16) * 16, 128))
    table = table[:heads * nr].reshape(heads, nr, w, w)
    out = []
    for off in _bias_classes(rows, kh):
        first = off + NA_WIN_H - 1
        sl = table[:, first:first + kh]
        out.append(jnp.transpose(sl, (0, 2, 1, 3)).reshape(heads * w, kh * w))
    return jnp.stack(out)


def _mixer_inputs(x, mod4, lw, tm):
    return _proj_call(x, mod4, lw['norm1_g'], lw['wa'], lw['wb'], lw['cs'].shape[1], tm)


def _hyena(oh, lw, tabs, c, nb_fwd, nb, tm, rows):
    fwd, inv = tabs['hy_fwd'], tabs['hy_inv']
    uc = _dwconv_call(oh, lw['hyena_conv_w'], lw['hyena_conv_b'], c, rows)
    h, row0, asum = _filt_call(tabs['feats'], lw['fw1'], lw['fb1'], lw['fw2'], lw['fb2'], lw['fw3'],
                               lw['fb3'], lw['freq'], tabs['deltas'], min(rows * 2, oh.shape[1]))
    kk, km = _kspec_call(fwd, h, row0, asum, c, tm)
    spec, ym = _hfwd_call(fwd, uc, kk, km, 0, c, nb_fwd, tm)
    z2 = _hinv_call(inv, spec, ym, uc, 0, uc, 1, lw['skip'], 0, False, c, nb, tm)
    spec, ym = _hfwd_call(fwd, z2, kk, km, 1, c, nb_fwd, tm)
    return _hinv_call(inv, spec, ym, z2, 0, uc, 2, lw['skip'], 1, True, c, nb, tm)


def _stream_layer(x, mod4, lw, tabs, proj, y_att, cfg, final_norm):
    c = cfg['c']
    oa, oh, ob, of = proj
    y_pool = _pool_call(oa, tabs['pool_inv'], lw['pool_blk'], lw['pool_scale'], cfg['rows'])
    y_fnet = _fnet_call(tabs['fn'], of, lw['cs'], cfg['nb'], cfg['thy'])
    y_hyena = _hyena(oh, lw, tabs, c, cfg['nb'], cfg['nb'], cfg['thy'], cfg['rows'])
    x = _merge_call(x, mod4, lw['norm1_g'], (y_pool, y_fnet, y_hyena, y_att), lw['wg'], lw['w_branch'],
                    lw['w_out'], cfg['tm'])
    return _ffn_call(x, mod4, lw['norm2_g'], lw['wu'], lw['wgt'], lw['ffn_conv_w'], lw['ffn_conv_b'],
                     lw['wd'], lw['final_g'], cfg['tm'], cfg['chunk'], final_norm)


def _seq_tables(l, c):
    q = l // RESIDUES
    b16 = lambda t: tuple(a.astype(BF16) for a in t)
    hy_fwd = sum((_trig_tables(q, 1, 0, RESIDUES, r, 2 * l) for r in range(RESIDUES)), ())
    hy_inv = sum((_trig_tables(q, RESIDUES, r, 1, 0, 2 * l) for r in range(RESIDUES)), ())
    scale = 1.0 / math.sqrt(l * (c // FNET_GROUPS))
    fn = sum((_trig_tables(q, 1, 0, RESIDUES, r, l) for r in range(RESIDUES)), ())
    deltas = jnp.linspace(math.log(HYENA_DECAY_TARGET) / HYENA_SLOW_DECAY,
                          math.log(HYENA_DECAY_TARGET) / HYENA_FAST_DECAY, c, dtype=F32)[None, :]
    feats = _hyena_features(l, 128)
    return {
        'hy_fwd': b16(hy_fwd), 'hy_inv': b16(hy_inv),
        'fn': b16(tuple(t * scale for t in fn)),
        'feats': jnp.concatenate([feats[r::RESIDUES] for r in range(RESIDUES)], axis=0), 'deltas': deltas,
        'pool_inv': _pool_inv_counts(l, c),
    }


def kernel(x, c, ctx, c_ctx, w_mod, b_mod, norm1_g, norm2_g, w_in, pool_w, pool_scale, hyena_conv_w,
           hyena_conv_b, hyena_filt_w1, hyena_filt_b1, hyena_filt_w2, hyena_filt_b2, hyena_filt_w3,
           hyena_filt_b3, hyena_freq, hyena_skip, na_rpb, w_branch, w_out, ffn_w_up, ffn_conv_w, ffn_conv_b,
           ffn_w_down, final_norm_g):
    batch, seq, d = x.shape
    lc = ctx.shape[1]
    depth = w_mod.shape[0]
    m = d // N_BRANCH
    f = ffn_w_down.shape[1]
    rows = seq // GRID_W
    kh = min(NA_WIN_H, rows)

    cvec = jnp.concatenate([c, c_ctx[None], jnp.zeros((16 - batch - 1, d), F32)], axis=0)
    mod = _mod_call(cvec, w_mod, b_mod)
    mod_x = mod[:, :batch].reshape(depth, batch, 6, 1, d)
    mod_c = jnp.broadcast_to(mod[:, batch:batch + 1], (depth, batch, 6 * d)).reshape(depth, batch, 6, 1, d)

    gsz = m // FNET_GROUPS
    cc, ss = (t.reshape(gsz, gsz) for t in _trig_tables(gsz, 1, 0, 1, 0, gsz))
    eye = jnp.eye(FNET_GROUPS, dtype=F32)
    cs = jnp.concatenate([jnp.kron(eye, cc), -jnp.kron(eye, ss)], axis=0).astype(BF16)

    tabs_x = _seq_tables(seq, m)
    tabs_c = _seq_tables(lc, m)
    assert max(POOL_WINDOWS) // 2 <= PAD and batch % 2 == 0
    cfg_x = dict(c=m, tm=min(1024, seq), rows=256, nb=2, thy=min(512, seq // RESIDUES), chunk=256,
                 group=min(16, rows))
    cfg_c = dict(c=m, tm=lc, rows=lc, nb=2, thy=lc // RESIDUES, chunk=256)

    pool_off, fnet_off, hy_off, na_off = 0, m, 2 * m, 5 * m
    gate_off = 8 * m

    for l in range(depth):
        last = l == depth - 1
        wi = w_in[l]
        lw = {
            'norm1_g': norm1_g[l][None], 'norm2_g': norm2_g[l][None], 'final_g': final_norm_g[None],
            'wa': jnp.concatenate([wi[:, pool_off:pool_off + m], wi[:, hy_off:hy_off + 3 * m]], axis=1).astype(BF16),
            'wb': jnp.concatenate([wi[:, fnet_off:fnet_off + m], wi[:, na_off:na_off + 3 * m]], axis=1).astype(BF16),
            'cs': cs,
            'wg': wi[:, gate_off:].astype(BF16),
            'pool_blk': jax.scipy.linalg.block_diag(*[pool_w[l, gi] for gi in range(pool_w.shape[1])]).astype(BF16),
            'pool_scale': pool_scale[l][None],
            'hyena_conv_w': hyena_conv_w[l], 'hyena_conv_b': hyena_conv_b[l][None],
            'fw1': _pad2(hyena_filt_w1[l], 128, 128), 'fb1': _pad2(hyena_filt_b1[l][None], 1, 128),
            'fw2': _pad2(hyena_filt_w2[l], 128, 128), 'fb2': _pad2(hyena_filt_b2[l][None], 1, 128),
            'fw3': _pad2(hyena_filt_w3[l], 128, hyena_filt_w3.shape[2]), 'fb3': hyena_filt_b3[l][None],
            'freq': _pad2(hyena_freq[l][None], 1, 128),
            'skip': hyena_skip[l][:, None, :],
            'w_branch': w_branch[l].astype(BF16), 'w_out': w_out[l].astype(BF16),
            'wu': ffn_w_up[l][:, :f].astype(BF16), 'wgt': ffn_w_up[l][:, f:].astype(BF16),
            'ffn_conv_w': ffn_conv_w[l], 'ffn_conv_b': ffn_conv_b[l][None],
            'wd': ffn_w_down[l].astype(BF16),
        }
        bias = _attention_bias(na_rpb[l], rows, kh)

        proj_c = _mixer_inputs(ctx, mod_c[l], lw, cfg_c['tm'])
        proj_x = _mixer_inputs(x, mod_x[l], lw, cfg_x['tm'])
        y_att = _natt_call(proj_x[2], proj_c[2], bias, m, cfg_x['group'])
        x = _stream_layer(x, mod_x[l], lw, tabs_x, proj_x, y_att, cfg_x, last)
        if not last:
            ctx = _stream_layer(ctx, mod_c[l], lw, tabs_c, proj_c, _catt_call(proj_c[2], m), cfg_c, False)
    return x
```

```python
import functools
import math

import numpy as np
import jax
import jax.numpy as jnp
from jax import lax
from jax.experimental import pallas as pl
from jax.experimental.pallas import tpu as pltpu

F32 = jnp.float32
BF16 = jnp.bfloat16

GRID_W = 64
N_BRANCH = 4
POOL_WINDOWS = (2, 4, 8, 16)
FNET_GROUPS = 4
HYENA_ORDER = 2
HYENA_BANDS = 16
HYENA_DECAY_TARGET = 1e-2
HYENA_FAST_DECAY = 0.3
HYENA_SLOW_DECAY = 1.5
HYENA_DECAY_SHIFT = 0.05
NA_HEAD_DIM = 64
NA_WIN_H = 8
NA_WIN_W = 16
EPS = 1e-6
NEG_INF = -1e30

RESIDUES = 4
SQRT_HALF = math.sqrt(0.5)
LANES = 128
HALO = 16
PAD = 8
VMEM_LIMIT = 56 * 1024 * 1024


def _params(sem):
    return pltpu.CompilerParams(dimension_semantics=sem, vmem_limit_bytes=VMEM_LIMIT)


def _dot(a, b):
    return jnp.dot(a, b, preferred_element_type=F32)


def _dot_t(a, b):
    return lax.dot_general(a, b, (((1,), (1,)), ((), ())), preferred_element_type=F32)


def _split_bf16(a):
    hi = a.astype(BF16)
    lo = (a - hi.astype(F32)).astype(BF16)
    return hi, lo


def _mod_kernel(c_ref, w_ref, b_ref, o_ref):
    a = c_ref[...]
    a = a * jax.nn.sigmoid(a)
    a_hi, a_lo = _split_bf16(a)
    w_hi, w_lo = _split_bf16(w_ref[0])
    o_ref[0] = _dot(a_hi, w_hi) + _dot(a_lo, w_hi) + _dot(a_hi, w_lo) + b_ref[0]


def _mod_call(cvec, w_mod, b_mod):
    depth, d, n = w_mod.shape
    rows = cvec.shape[0]
    tn = 1024
    return pl.pallas_call(
        _mod_kernel,
        grid=(depth, n // tn),
        in_specs=[
            pl.BlockSpec((rows, d), lambda l, j: (0, 0)),
            pl.BlockSpec((1, d, tn), lambda l, j: (l, 0, j)),
            pl.BlockSpec((1, 1, tn), lambda l, j: (l, 0, j)),
        ],
        out_specs=pl.BlockSpec((1, rows, tn), lambda l, j: (l, 0, j)),
        out_shape=jax.ShapeDtypeStruct((depth, rows, n), F32),
        compiler_params=_params(("arbitrary", "arbitrary")),
        name="mod",
    )(cvec, w_mod, b_mod.reshape(depth, 1, n))


def _ada_norm(x, g, shift, scale):
    y = x * lax.rsqrt(jnp.mean(x * x, axis=-1, keepdims=True) + EPS)
    return y * (g * (1.0 + scale)) + shift


def _deinterleave(y, scr_ref, ways):
    n = y.shape[0]
    parts = [[] for _ in range(ways)]
    for cb in range(y.shape[1] // LANES):
        scr_ref[cb] = y[:, cb * LANES:(cb + 1) * LANES]
        for k in range(ways):
            parts[k].append(scr_ref[cb, pl.ds(k, n // ways, stride=ways), :])
    return [jnp.concatenate(p, axis=1) for p in parts]


def _interleave(parts, scr_ref):
    ways = len(parts)
    n = parts[0].shape[0]
    cols = []
    for cb in range(parts[0].shape[1] // LANES):
        for k in range(ways):
            scr_ref[cb, pl.ds(k, n, stride=ways), :] = parts[k][:, cb * LANES:(cb + 1) * LANES]
        cols.append(scr_ref[cb])
    return jnp.concatenate(cols, axis=1)


def _proj_kernel(x_ref, sh_ref, sc_ref, g_ref, w_ref, oa_ref, oh_ref, ob_ref, of_ref, scr_ref):
    m = oa_ref.shape[2]
    h = _ada_norm(x_ref[0], g_ref[...], sh_ref[0, 0], sc_ref[0, 0]).astype(BF16)
    p = _dot(h, w_ref[...])
    oa_ref[0] = p[:, :m]
    oh_ref[0] = p[:, 2 * m:5 * m].astype(BF16)
    ob_ref[0] = p[:, 5 * m:8 * m].astype(BF16)
    for k, part in enumerate(_deinterleave(p[:, m:2 * m], scr_ref, RESIDUES)):
        of_ref[0, k] = part.astype(BF16)


def _proj_call(x, mod4, g, w_in, layer, m, tm):
    b, l, d = x.shape
    const = lambda bi, i: (0, 0)
    return pl.pallas_call(
        _proj_kernel,
        grid=(b, l // tm),
        in_specs=[
            pl.BlockSpec((1, tm, d), lambda bi, i: (bi, i, 0)),
            pl.BlockSpec((1, 1, 1, d), lambda bi, i: (bi, 0, 0, 0)),
            pl.BlockSpec((1, 1, 1, d), lambda bi, i: (bi, 1, 0, 0)),
            pl.BlockSpec((1, d), const),
            pl.BlockSpec((None, d, 8 * m), lambda bi, i: (layer, 0, 0)),
        ],
        out_specs=[
            pl.BlockSpec((1, tm, m), lambda bi, i: (bi, i, 0)),
            pl.BlockSpec((1, tm, 3 * m), lambda bi, i: (bi, i, 0)),
            pl.BlockSpec((1, tm, 3 * m), lambda bi, i: (bi, i, 0)),
            pl.BlockSpec((1, RESIDUES, tm // RESIDUES, m), lambda bi, i: (bi, 0, i, 0)),
        ],
        out_shape=[
            jax.ShapeDtypeStruct((b, l, m), F32),
            jax.ShapeDtypeStruct((b, l, 3 * m), BF16),
            jax.ShapeDtypeStruct((b, l, 3 * m), BF16),
            jax.ShapeDtypeStruct((b, RESIDUES, l // RESIDUES, m), BF16),
        ],
        scratch_shapes=[pltpu.VMEM((m // LANES, tm, LANES), F32)],
        compiler_params=_params(("parallel", "parallel")),
        name="proj",
    )(x, mod4, mod4, g, w_in)


def _fill_padded(src_ref, pad_ref, l, rows):
    c = pad_ref.shape[1]
    pad_ref[0:PAD, :] = jnp.zeros((PAD, c), F32)
    pad_ref[l + PAD:l + 2 * PAD, :] = jnp.zeros((PAD, c), F32)

    def copy(i, carry):
        r = pl.multiple_of(i * rows, rows)
        pad_ref[pl.ds(r + PAD, rows), :] = src_ref[0, pl.ds(r, rows), :].astype(F32)
        return carry

    lax.fori_loop(0, l // rows, copy, 0)


def _centred_window_sums(u, windows):
    n = u.shape[0]
    fwd = {1: u}
    w = 1
    while w < max(windows):
        fwd[2 * w] = fwd[w] + pltpu.roll(fwd[w], n - w, 0)
        w *= 2
    return {w: pltpu.roll(fwd[w], w // 2, 0) for w in windows}


def _pool_kernel(u_ref, inv_ref, w_ref, s_ref, o_ref, pad_ref, *, rows):
    l, c = u_ref.shape[1], u_ref.shape[2]
    gw = c // len(POOL_WINDOWS)
    per_block = LANES // gw
    _fill_padded(u_ref, pad_ref, l, rows)
    n = rows + 2 * PAD
    lane = lax.broadcasted_iota(jnp.int32, (n, LANES), 1)
    w = w_ref[...]
    scale = s_ref[...]

    def body(i, carry):
        r = pl.multiple_of(i * rows, rows)
        cols = []
        for cb in range(c // LANES):
            lanes = slice(cb * LANES, (cb + 1) * LANES)
            win = pad_ref[pl.ds(r, n), lanes]
            wins = POOL_WINDOWS[cb * per_block:(cb + 1) * per_block]
            sums = _centred_window_sums(win, wins)
            s = sums[wins[-1]]
            for k in range(per_block - 2, -1, -1):
                s = jnp.where(lane < (k + 1) * gw, sums[wins[k]], s)
            cols.append((s * inv_ref[pl.ds(r, n), lanes] - win)[PAD:PAD + rows])
        y = jnp.concatenate(cols, axis=1)
        o_ref[0, pl.ds(r, rows), :] = (_dot(y.astype(BF16), w) * scale).astype(BF16)
        return carry

    lax.fori_loop(0, l // rows, body, 0)


def _pool_inv_counts(l, c):
    gw = c // len(POOL_WINDOWS)
    t = jnp.arange(-PAD, l + PAD, dtype=jnp.int32)[:, None]
    half = jnp.asarray(np.repeat(np.array(POOL_WINDOWS) // 2, gw), jnp.int32)[None, :]
    cnt = jnp.minimum(t + half, l) - jnp.maximum(t - half, 0)
    return 1.0 / jnp.maximum(cnt, 1).astype(F32)


def _pool_call(oa, inv_cnt, w_blk, scale, rows):
    b, l, _ = oa.shape
    c = w_blk.shape[0]
    return pl.pallas_call(
        functools.partial(_pool_kernel, rows=rows),
        grid=(b,),
        in_specs=[
            pl.BlockSpec((1, l, c), lambda bi: (bi, 0, 0)),
            pl.BlockSpec((l + 2 * PAD, c), lambda bi: (0, 0)),
            pl.BlockSpec((c, c), lambda bi: (0, 0)),
            pl.BlockSpec((1, c), lambda bi: (0, 0)),
        ],
        out_specs=pl.BlockSpec((1, l, c), lambda bi: (bi, 0, 0)),
        out_shape=jax.ShapeDtypeStruct((b, l, c), BF16),
        scratch_shapes=[pltpu.VMEM((l + 2 * PAD, c), F32)],
        compiler_params=_params(("parallel",)),
        name="pool",
    )(oa, inv_cnt, w_blk, scale)


def _dwconv_kernel(u_ref, w_ref, b_ref, o_ref, pad_ref, scr_ref, *, rows):
    l = u_ref.shape[1]
    _fill_padded(u_ref, pad_ref, l, rows)
    n = rows + 2 * PAD
    sub = rows // RESIDUES
    w = w_ref[...]
    bias = b_ref[...]

    def body(i, carry):
        r = pl.multiple_of(i * rows, rows)
        rs = pl.multiple_of(i * sub, sub)
        win = pad_ref[pl.ds(r, n), :]
        y = pltpu.roll(win, 1, 0) * w[0:1] + win * w[1:2] + pltpu.roll(win, n - 1, 0) * w[2:3] + bias
        for k, part in enumerate(_deinterleave(y[PAD:PAD + rows], scr_ref, RESIDUES)):
            o_ref[0, k, pl.ds(rs, sub), :] = part.astype(BF16)
        return carry

    lax.fori_loop(0, l // rows, body, 0)


def _dwconv_call(u, w, bias, c, rows):
    b, l, _ = u.shape
    nblk = w.shape[1] // c
    return pl.pallas_call(
        functools.partial(_dwconv_kernel, rows=rows),
        grid=(b, nblk),
        in_specs=[
            pl.BlockSpec((1, l, c), lambda bi, j: (bi, 0, j)),
            pl.BlockSpec((w.shape[0], c), lambda bi, j: (0, j)),
            pl.BlockSpec((1, c), lambda bi, j: (0, j)),
        ],
        out_specs=pl.BlockSpec((1, RESIDUES, l // RESIDUES, c), lambda bi, j: (bi, 0, 0, j)),
        out_shape=jax.ShapeDtypeStruct((b, RESIDUES, l // RESIDUES, nblk * c), BF16),
        scratch_shapes=[pltpu.VMEM((l + 2 * PAD, c), F32), pltpu.VMEM((c // LANES, rows, LANES), F32)],
        compiler_params=_params(("parallel", "parallel")),
        name="dwconv",
    )(u, w, bias)


def _filt_kernel(f_ref, w1_ref, b1_ref, w2_ref, b2_ref, w3_ref, b3_ref, fr_ref, dl_ref,
                 hb_ref, row0_ref, asum_ref):
    hp = lax.Precision.HIGHEST
    feats = f_ref[...]
    freq = fr_ref[...]
    h = jnp.sin(freq * (jnp.dot(feats, w1_ref[...], precision=hp, preferred_element_type=F32) + b1_ref[...]))
    h = jnp.sin(freq * (jnp.dot(h, w2_ref[...], precision=hp, preferred_element_type=F32) + b2_ref[...]))
    h = jnp.dot(h, w3_ref[...], precision=hp, preferred_element_type=F32) + b3_ref[...]
    t = feats[:, 0:1]
    win = jnp.exp(-t * jnp.abs(dl_ref[...])) + HYENA_DECAY_SHIFT
    h = h * jnp.concatenate([win] * (h.shape[1] // win.shape[1]), axis=1)
    hb = h.astype(BF16)
    hb_ref[...] = hb
    part = jnp.sum(jnp.abs(h), axis=0, keepdims=True)

    @pl.when(pl.program_id(0) == 0)
    def _():
        asum_ref[...] = jnp.zeros_like(asum_ref)
        row0_ref[...] = hb[0:16].astype(F32)[0:8]

    asum_ref[...] += part


def _filt_call(feats, w1, b1, w2, b2, w3, b3, freq, deltas, rows):
    l, fd = feats.shape
    hd = w2.shape[0]
    n = w3.shape[1]
    c = deltas.shape[1]
    const = lambda i: (0, 0)
    return pl.pallas_call(
        _filt_kernel,
        grid=(l // rows,),
        in_specs=[
            pl.BlockSpec((rows, fd), lambda i: (i, 0)),
            pl.BlockSpec((fd, hd), const), pl.BlockSpec((1, hd), const),
            pl.BlockSpec((hd, hd), const), pl.BlockSpec((1, hd), const),
            pl.BlockSpec((hd, n), const), pl.BlockSpec((1, n), const),
            pl.BlockSpec((1, hd), const), pl.BlockSpec((1, c), const),
        ],
        out_specs=[
            pl.BlockSpec((rows, n), lambda i: (i, 0)),
            pl.BlockSpec((8, n), const),
            pl.BlockSpec((1, n), const),
        ],
        out_shape=[
            jax.ShapeDtypeStruct((l, n), BF16),
            jax.ShapeDtypeStruct((8, n), F32),
            jax.ShapeDtypeStruct((1, n), F32),
        ],
        compiler_params=_params(("arbitrary",)),
        name="hyena_filt",
    )(feats, w1, b1, w2, b2, w3, b3, freq, deltas)


def _alt_signs(n):
    t = lax.broadcasted_iota(jnp.int32, (16, n), 1)
    return jnp.where((t & 1) == 0, 1.0, -1.0).astype(BF16)


def _mat_specs(tm, tables):
    _, a, cols = tables[0].shape
    return [pl.BlockSpec((tm // a, a, cols), lambda g, i: (i, 0, 0)) for _ in tables]


def _mat(ref):
    return ref[...].reshape(ref.shape[0] * ref.shape[1], ref.shape[2])


def _quad_transform(mat_refs, parts):
    a = [_dot(_mat(mat_refs[2 * r]), parts[r]) for r in range(RESIDUES)]
    b = [_dot(_mat(mat_refs[2 * r + 1]), parts[r]) for r in range(RESIDUES)]
    ea, fa, ga, ha = a[0] + a[2], a[0] - a[2], a[1] + a[3], a[1] - a[3]
    eb, fb, gb, hb = b[0] + b[2], b[0] - b[2], b[1] + b[3], b[1] - b[3]
    zr = (ea + ga, fa + hb, fa - hb, ea - ga)
    zs = (eb + gb, ha - fb, ha + fb, gb - eb)
    return zr, zs


def _quad_special(alt):
    c1, c3 = SQRT_HALF * alt[1], SQRT_HALF * alt[3]
    return (alt[0] + c1 - c3, c1 + alt[2] + c3), (alt[0] - c1 + c3, c1 - alt[2] + c3)


def _kspec_kernel(*refs):
    mat_refs, (h_ref, row0_ref, asum_ref, kk_ref, km_ref) = refs[:8], refs[8:]
    i = pl.program_id(1)
    tm, q = kk_ref.shape[2], mat_refs[0].shape[2]
    c = kk_ref.shape[3]
    l = RESIDUES * q
    parts = [h_ref[r * q:(r + 1) * q, :] for r in range(RESIDUES)]
    zr, zs = _quad_transform(mat_refs, parts)
    asum = asum_ref[...]
    inv = 1.0 / (asum[:, :c] + asum[:, c:] + EPS)
    hb0 = row0_ref[0:1, c:]
    row = lax.broadcasted_iota(jnp.int32, (tm, c), 0) + i * tm
    wj = jnp.where(row == 0, 0.5 / l, 1.0 / l) * inv
    for f in range(4):
        kk_ref[0, 2 * f] = (zr[f][:, :c] + zr[f][:, c:] - hb0) * wj
        kk_ref[0, 2 * f + 1] = (zs[f][:, :c] - zs[f][:, c:]) * wj

    @pl.when(i == 0)
    def _():
        sg = _alt_signs(q)
        alt = [_dot(sg, p)[0:8] for p in parts]
        for f, (xr, xs) in enumerate(_quad_special(alt)):
            km_ref[0, :, 2 * f * c:(2 * f + 1) * c] = (xr[:, :c] + xr[:, c:] - hb0) * inv * (1.0 / l)
            km_ref[0, :, (2 * f + 1) * c:(2 * f + 2) * c] = (xs[:, :c] - xs[:, c:]) * inv * (1.0 / l)


def _kspec_call(mats, h, row0, asum, c, tm):
    q = mats[0].shape[2]
    l = RESIDUES * q
    orders = h.shape[1] // (2 * c)
    return pl.pallas_call(
        _kspec_kernel,
        grid=(orders, q // tm),
        in_specs=_mat_specs(tm, mats) + [
            pl.BlockSpec((l, 2 * c), lambda o, i: (0, o)),
            pl.BlockSpec((8, 2 * c), lambda o, i: (0, o)),
            pl.BlockSpec((1, 2 * c), lambda o, i: (0, o)),
        ],
        out_specs=[
            pl.BlockSpec((1, 8, tm, c), lambda o, i: (o, 0, i, 0)),
            pl.BlockSpec((1, 8, 4 * c), lambda o, i: (o, 0, 0)),
        ],
        out_shape=[
            jax.ShapeDtypeStruct((orders, 8, q, c), F32),
            jax.ShapeDtypeStruct((orders, 8, 4 * c), F32),
        ],
        compiler_params=_params(("arbitrary", "arbitrary")),
        name="hyena_kspec",
    )(*mats, h, row0, asum)


def _cmul(zr, zs, kr, ks):
    return zr * kr - zs * ks, zr * ks + zs * kr


def _hfwd_kernel(*refs):
    mat_refs, (z_ref, kk_ref, km_ref, spec_ref, ym_ref) = refs[:8], refs[8:]
    nb = z_ref.shape[0]
    c = z_ref.shape[3]
    for bb in range(nb):
        zr, zs = _quad_transform(mat_refs, [z_ref[bb, r] for r in range(RESIDUES)])
        yr, ys = zip(*[_cmul(zr[f], zs[f], kk_ref[0, 2 * f], kk_ref[0, 2 * f + 1]) for f in range(4)])
        a, b, u1, u2 = yr[0] + yr[3], yr[0] - yr[3], yr[1] + yr[2], ys[1] + ys[2]
        cc, dd, d1, d2 = ys[0] - ys[3], ys[0] + ys[3], yr[1] - yr[2], ys[2] - ys[1]
        for k, v in enumerate((a + u1, cc + d2, b + u2, dd + d1, a - u1, cc - d2, b - u2, dd - d1)):
            spec_ref[bb, k] = v.astype(BF16)

    @pl.when(pl.program_id(1) == 0)
    def _():
        sg = _alt_signs(z_ref.shape[2])
        for bb in range(nb):
            alt = [_dot(sg, z_ref[bb, r])[0:8] for r in range(RESIDUES)]
            (zq_r, zq_s), (z3_r, z3_s) = _quad_special(alt)
            yq_r, yq_s = _cmul(zq_r, zq_s, km_ref[0, :, 0:c], km_ref[0, :, c:2 * c])
            y3_r, y3_s = _cmul(z3_r, z3_s, km_ref[0, :, 2 * c:3 * c], km_ref[0, :, 3 * c:4 * c])
            ym_ref[bb, :, 0:c] = yq_r + y3_r
            ym_ref[bb, :, c:2 * c] = SQRT_HALF * (yq_r + yq_s - y3_r + y3_s)
            ym_ref[bb, :, 2 * c:3 * c] = yq_s - y3_s
            ym_ref[bb, :, 3 * c:4 * c] = SQRT_HALF * (yq_s - yq_r + y3_r + y3_s)


def _hfwd_call(mats, zb, kk, km, order, c, nb, tm):
    b, _, q, _ = zb.shape
    return pl.pallas_call(
        _hfwd_kernel,
        grid=(b // nb, q // tm),
        in_specs=_mat_specs(tm, mats) + [
            pl.BlockSpec((nb, RESIDUES, q, c), lambda g, i: (g, 0, 0, 0)),
            pl.BlockSpec((1, 8, tm, c), lambda g, i: (order, 0, i, 0)),
            pl.BlockSpec((1, 8, 4 * c), lambda g, i: (order, 0, 0)),
        ],
        out_specs=[
            pl.BlockSpec((nb, 8, tm, c), lambda g, i: (g, 0, i, 0)),
            pl.BlockSpec((nb, 8, 4 * c), lambda g, i: (g, 0, 0)),
        ],
        out_shape=[
            jax.ShapeDtypeStruct((b, 8, q, c), BF16),
            jax.ShapeDtypeStruct((b, 8, 4 * c), F32),
        ],
        compiler_params=_params(("arbitrary", "arbitrary")),
        name="hyena_fwd",
    )(*mats, zb, kk, km)


def _hinv_kernel(*refs, natural_out):
    mat_refs, (spec_ref, ym_ref, z_ref, gate_ref, skip_ref), out_refs = refs[:8], refs[8:13], refs[13:]
    nb, _, tm, c = z_ref.shape
    row = lax.broadcasted_iota(jnp.int32, (tm, c), 0)
    even_row = (row & 1) == 0
    skip = skip_ref[0]
    for bb in range(nb):
        outs = []
        for r in range(RESIDUES):
            m = ym_ref[bb, 0:1, r * c:(r + 1) * c]
            y = (_dot(_mat(mat_refs[2 * r]), spec_ref[bb, 2 * r]) + _dot(_mat(mat_refs[2 * r + 1]), spec_ref[bb, 2 * r + 1])
                 + jnp.where(even_row, m, -m))
            outs.append(gate_ref[bb, r].astype(F32) * (y + skip * z_ref[bb, r].astype(F32)))
        if natural_out:
            o_ref, scr_ref = out_refs
            o_ref[bb] = _interleave(outs, scr_ref).astype(BF16)
        else:
            (o_ref,) = out_refs
            for r in range(RESIDUES):
                o_ref[bb, r] = outs[r].astype(BF16)


def _hinv_call(mats, spec, ym, zf, zcol, uc, gcol, skip, order, natural_out, c, nb, tm):
    b, _, q, _ = spec.shape
    if natural_out:
        out_specs = pl.BlockSpec((nb, RESIDUES * tm, c), lambda g, i: (g, i, 0))
        out_shape = jax.ShapeDtypeStruct((b, RESIDUES * q, c), BF16)
        scratch = [pltpu.VMEM((c // LANES, RESIDUES * tm, LANES), F32)]
    else:
        out_specs = pl.BlockSpec((nb, RESIDUES, tm, c), lambda g, i: (g, 0, i, 0))
        out_shape = jax.ShapeDtypeStruct((b, RESIDUES, q, c), BF16)
        scratch = []
    return pl.pallas_call(
        functools.partial(_hinv_kernel, natural_out=natural_out),
        grid=(b // nb, q // tm),
        in_specs=_mat_specs(tm, mats) + [
            pl.BlockSpec((nb, 8, q, c), lambda g, i: (g, 0, 0, 0)),
            pl.BlockSpec((nb, 8, 4 * c), lambda g, i: (g, 0, 0)),
            pl.BlockSpec((nb, RESIDUES, tm, c), lambda g, i: (g, 0, i, zcol)),
            pl.BlockSpec((nb, RESIDUES, tm, c), lambda g, i: (g, 0, i, gcol)),
            pl.BlockSpec((1, 1, c), lambda g, i: (order, 0, 0)),
        ],
        out_specs=out_specs,
        out_shape=out_shape,
        scratch_shapes=scratch,
        compiler_params=_params(("arbitrary", "arbitrary")),
        name="hyena_inv",
    )(*mats, spec, ym, zf, uc, skip)


def _fnet_kernel(*refs):
    mat_refs, (u_ref, cs_ref, o_ref) = refs[:8], refs[8:]
    nb = u_ref.shape[0]
    cs = cs_ref[...]
    for bb in range(nb):
        parts = [u_ref[bb, r] for r in range(RESIDUES)]
        a = [_dot(_mat(mat_refs[2 * r]), parts[r]) for r in range(RESIDUES)]
        b = [_dot(_mat(mat_refs[2 * r + 1]), parts[r]) for r in range(RESIDUES)]
        ea, fa, ga, ha = a[0] + a[2], a[0] - a[2], a[1] + a[3], a[1] - a[3]
        eb, fb, gb, hb = b[0] + b[2], b[0] - b[2], b[1] + b[3], b[1] - b[3]
        uc = (ea + ga, fa - hb, ea - ga, fa + hb)
        us = (eb + gb, fb + ha, eb - gb, fb - ha)
        for m in range(RESIDUES):
            both = jnp.concatenate([uc[m].astype(BF16), us[m].astype(BF16)], axis=1)
            o_ref[bb, m] = _dot(both, cs).astype(BF16)


def _fnet_call(mats, u4, cs, nb, tm):
    b, _, q, c = u4.shape
    out = pl.pallas_call(
        _fnet_kernel,
        grid=(b // nb, q // tm),
        in_specs=_mat_specs(tm, mats) + [
            pl.BlockSpec((nb, RESIDUES, q, c), lambda g, i: (g, 0, 0, 0)),
            pl.BlockSpec(cs.shape, lambda g, i: (0, 0)),
        ],
        out_specs=pl.BlockSpec((nb, RESIDUES, tm, c), lambda g, i: (g, 0, i, 0)),
        out_shape=jax.ShapeDtypeStruct((b, RESIDUES, q, c), BF16),
        compiler_params=_params(("arbitrary", "arbitrary")),
        name="fnet",
    )(*mats, u4, cs)
    return out.reshape(b, RESIDUES * q, c)


def _stack_heads(q, heads):
    lane = lax.broadcasted_iota(jnp.int32, q.shape, 1)
    zero = jnp.zeros_like(q)
    return jnp.concatenate([jnp.where(lax.shift_right_logical(lane, int(math.log2(NA_HEAD_DIM))) == h, q, zero) for h in range(heads)], axis=0)


def _merge_heads(o, heads, n):
    lane = lax.broadcasted_iota(jnp.int32, (n, o.shape[1]), 1)
    out = jnp.zeros((n, o.shape[1]), F32)
    for h in range(heads):
        out = out + jnp.where(lax.shift_right_logical(lane, int(math.log2(NA_HEAD_DIM))) == h, o[h * n:(h + 1) * n], 0.0)
    return out


def _natt_kernel(q_ref, k_ref, v_ref, kc_ref, vc_ref, bias_ref, o_ref, *, rows, kh, group):
    w = GRID_W
    heads = q_ref.shape[2] // NA_HEAD_DIM
    base = pl.program_id(1) * group
    lo = kh // 2
    kc, vc = kc_ref[0], vc_ref[0]
    scale = jnp.asarray(NA_HEAD_DIM ** -0.5, BF16)

    def body(j, carry):
        r = base + j
        r0 = jnp.clip(r - lo, 0, rows - kh)
        start = pl.multiple_of(r0 * w, w)
        cls = jnp.minimum(r, lo) + jnp.maximum(r - (rows - kh + lo), 0)
        q4 = _stack_heads(q_ref[0, pl.ds(pl.multiple_of(j * w, w), w), :] * scale, heads)
        ks = k_ref[0, pl.ds(start, kh * w), :]
        vs = v_ref[0, pl.ds(start, kh * w), :]
        s_nb = _dot_t(q4, ks) + bias_ref[cls]
        s_cx = _dot_t(q4, kc)
        m = jnp.maximum(jnp.max(s_nb, axis=-1, keepdims=True), jnp.max(s_cx, axis=-1, keepdims=True))
        p_nb = jnp.exp(s_nb - m)
        p_cx = jnp.exp(s_cx - m)
        den = jnp.sum(p_nb, axis=-1, keepdims=True) + jnp.sum(p_cx, axis=-1, keepdims=True)
        o4 = (_dot(p_nb.astype(BF16), vs) + _dot(p_cx.astype(BF16), vc)) / den
        o_ref[0, pl.ds(pl.multiple_of(j * w, w), w), :] = _merge_heads(o4, heads, w).astype(BF16)
        return carry

    lax.fori_loop(0, group, body, 0, unroll=4)


def _natt_call(ob, obc, bias, c, group):
    b, l, _ = ob.shape
    lc = obc.shape[1]
    w = GRID_W
    rows = l // w
    kh = min(NA_WIN_H, rows)
    return pl.pallas_call(
        functools.partial(_natt_kernel, rows=rows, kh=kh, group=group),
        grid=(b, rows // group),
        in_specs=[
            pl.BlockSpec((1, group * w, c), lambda bi, r: (bi, r, 0)),
            pl.BlockSpec((1, l, c), lambda bi, r: (bi, 0, 1)),
            pl.BlockSpec((1, l, c), lambda bi, r: (bi, 0, 2)),
            pl.BlockSpec((1, lc, c), lambda bi, r: (bi, 0, 1)),
            pl.BlockSpec((1, lc, c), lambda bi, r: (bi, 0, 2)),
            pl.BlockSpec(bias.shape, lambda bi, r: (0, 0, 0)),
        ],
        out_specs=pl.BlockSpec((1, group * w, c), lambda bi, r: (bi, r, 0)),
        out_shape=jax.ShapeDtypeStruct((b, l, c), BF16),
        compiler_params=_params(("parallel", "arbitrary")),
        name="nb_attn",
    )(ob, ob, ob, obc, obc, bias)


def _bias_kernel(r_ref, o_ref):
    w = GRID_W
    shift = int(math.log2(w))
    n = o_ref.shape[1]
    r = r_ref[...]
    hi = r.astype(BF16)
    mid = (r - hi.astype(F32)).astype(BF16)
    lo = (r - hi.astype(F32) - mid.astype(F32)).astype(BF16)
    dc = lax.broadcasted_iota(jnp.int32, (r.shape[1], n), 0)
    p = lax.broadcasted_iota(jnp.int32, (r.shape[1], n), 1)
    idx = jnp.clip((p & (w - 1)) - lax.shift_right_logical(p, shift), 1 - NA_WIN_W, NA_WIN_W - 1) + (NA_WIN_W - 1)
    onehot = jnp.where(dc == idx, 1.0, 0.0).astype(BF16)
    val = _dot(hi, onehot) + _dot(mid, onehot) + _dot(lo, onehot)
    po = lax.broadcasted_iota(jnp.int32, val.shape, 1)
    cq = lax.shift_right_logical(po, shift)
    ck = po & (w - 1)
    c0 = jnp.clip(cq - NA_WIN_W // 2, 0, w - NA_WIN_W)
    inside = jnp.logical_and(ck >= c0, ck < c0 + NA_WIN_W)
    o_ref[...] = jnp.where(inside, val, NEG_INF)


def _bias_call(rpb2d):
    rows, cols = rpb2d.shape
    n = GRID_W * GRID_W
    return pl.pallas_call(
        _bias_kernel,
        grid=(1,),
        in_specs=[pl.BlockSpec((rows, cols), lambda i: (0, 0))],
        out_specs=pl.BlockSpec((rows, n), lambda i: (0, 0)),
        out_shape=jax.ShapeDtypeStruct((rows, n), F32),
        compiler_params=_params(("arbitrary",)),
        name="rpb_table",
    )(rpb2d)


def _catt_kernel(q_ref, k_ref, v_ref, o_ref):
    n = q_ref.shape[1]
    heads = q_ref.shape[2] // NA_HEAD_DIM
    q = q_ref[0] * jnp.asarray(NA_HEAD_DIM ** -0.5, BF16)
    q4 = _stack_heads(q, heads)
    s = _dot_t(q4, k_ref[0])
    p = jnp.exp(s - jnp.max(s, axis=-1, keepdims=True))
    den = jnp.sum(p, axis=-1, keepdims=True)
    o4 = _dot(p.astype(BF16), v_ref[0]) / den
    o_ref[0] = _merge_heads(o4, heads, n).astype(BF16)


def _catt_call(obc, c):
    b, lc, _ = obc.shape
    return pl.pallas_call(
        _catt_kernel,
        grid=(b,),
        in_specs=[pl.BlockSpec((1, lc, c), lambda bi, j=j: (bi, 0, j)) for j in range(3)],
        out_specs=pl.BlockSpec((1, lc, c), lambda bi: (bi, 0, 0)),
        out_shape=jax.ShapeDtypeStruct((b, lc, c), BF16),
        compiler_params=_params(("parallel",)),
        name="ctx_attn",
    )(obc, obc, obc)


def _merge_kernel(x_ref, sh_ref, sc_ref, gt_ref, g_ref, y0_ref, y1_ref, y2_ref, y3_ref,
                  wg0_ref, wg1_ref, wb_ref, wo_ref, o_ref):
    d = x_ref.shape[2]
    x = x_ref[0]
    h = _ada_norm(x, g_ref[...], sh_ref[0, 0], sc_ref[0, 0]).astype(BF16)
    merged = None
    for bi, y_ref in enumerate((y0_ref, y1_ref, y2_ref, y3_ref)):
        wg_ref = (wg0_ref, wg1_ref)[bi // 2]
        gate = jax.nn.sigmoid(_dot(h, wg_ref[:, (bi % 2) * d:(bi % 2 + 1) * d]))
        term = gate * _dot(y_ref[0].astype(BF16), wb_ref[bi])
        merged = term if merged is None else merged + term
    o_ref[0] = x + gt_ref[0, 0] * _dot(merged.astype(BF16), wo_ref[...])


def _merge_call(x, mod4, g, ys, w_in, wb, wo, layer, tm):
    b, l, d = x.shape
    c = ys[0].shape[2]
    gate_blk = w_in.shape[2] // (2 * d) - 2
    assert (gate_blk + 2) * 2 * d == w_in.shape[2]
    single = dict(pipeline_mode=pl.Buffered(1))
    mod_spec = lambda k: pl.BlockSpec((1, 1, 1, d), lambda bi, i: (bi, k, 0, 0))
    return pl.pallas_call(
        _merge_kernel,
        grid=(b, l // tm),
        in_specs=[
            pl.BlockSpec((1, tm, d), lambda bi, i: (bi, i, 0)),
            mod_spec(0), mod_spec(1), mod_spec(2),
            pl.BlockSpec((1, d), lambda bi, i: (0, 0)),
        ] + [pl.BlockSpec((1, tm, c), lambda bi, i: (bi, i, 0)) for _ in ys] + [
            pl.BlockSpec((None, d, 2 * d), lambda bi, i: (layer, 0, gate_blk), **single),
            pl.BlockSpec((None, d, 2 * d), lambda bi, i: (layer, 0, gate_blk + 1), **single),
            pl.BlockSpec((None,) + wb.shape[1:], lambda bi, i: (layer, 0, 0, 0), **single),
            pl.BlockSpec((None, d, d), lambda bi, i: (layer, 0, 0), **single),
        ],
        out_specs=pl.BlockSpec((1, tm, d), lambda bi, i: (bi, i, 0)),
        out_shape=jax.ShapeDtypeStruct((b, l, d), F32),
        compiler_params=_params(("parallel", "parallel")),
        name="merge",
    )(x, mod4, mod4, mod4, g, *ys, w_in, w_in, wb, wo)


def _ffn_kernel(x_ref, xp_ref, xn_ref, sh_ref, sc_ref, gt_ref, g_ref, wu_ref, wgt_ref, cw_ref, cb_ref,
                wd_ref, fg_ref, o_ref, a_ref, *, chunk, final_norm):
    i = pl.program_id(1)
    last = pl.num_programs(1) - 1
    tm = x_ref.shape[1]
    f = wu_ref.shape[1]
    n = tm + 2 * HALO
    g, sh, sc = g_ref[...], sh_ref[0, 0], sc_ref[0, 0]
    x = x_ref[0]
    h = _ada_norm(x, g, sh, sc).astype(BF16)
    hp = _ada_norm(xp_ref[0], g, sh, sc).astype(BF16)
    hn = _ada_norm(xn_ref[0], g, sh, sc).astype(BF16)
    hext = jnp.concatenate([hp, h, hn], axis=0)
    row = lax.broadcasted_iota(jnp.int32, (n, chunk), 0)
    valid = jnp.logical_and(jnp.logical_or(row >= HALO, i > 0), jnp.logical_or(row < HALO + tm, i < last))
    for j in range(f // chunk):
        cols = slice(j * chunk, (j + 1) * chunk)
        gp = jnp.where(valid, _dot(hext, wgt_ref[:, cols]), 0.0)
        cw = cw_ref[:, cols]
        gc = (pltpu.roll(gp, 1, 0) * cw[0:1] + gp * cw[1:2] + pltpu.roll(gp, n - 1, 0) * cw[2:3]
              + cb_ref[:, cols])[HALO:HALO + tm]
        u = _dot(h, wu_ref[:, cols])
        a_ref[:, cols] = (gc * jax.nn.sigmoid(gc) * u).astype(BF16)
    y = x + gt_ref[0, 0] * _dot(a_ref[...], wd_ref[...])
    if final_norm:
        y = y * lax.rsqrt(jnp.mean(y * y, axis=-1, keepdims=True) + EPS) * fg_ref[...]
    o_ref[0] = y


def _ffn_call(x, mod4, g, w_up, cw, cb, w_down, fg, layer, tm, chunk, final_norm):
    b, l, d = x.shape
    f = w_down.shape[1]
    hb = tm // HALO
    nblk = l // HALO
    const2 = lambda bi, i: (0, 0)
    mod_spec = lambda k: pl.BlockSpec((1, 1, 1, d), lambda bi, i: (bi, k, 0, 0))
    single = dict(pipeline_mode=pl.Buffered(1))
    return pl.pallas_call(
        functools.partial(_ffn_kernel, chunk=chunk, final_norm=final_norm),
        grid=(b, l // tm),
        in_specs=[
            pl.BlockSpec((1, tm, d), lambda bi, i: (bi, i, 0)),
            pl.BlockSpec((1, HALO, d), lambda bi, i: (bi, jnp.maximum(i * hb - 1, 0), 0)),
            pl.BlockSpec((1, HALO, d), lambda bi, i: (bi, jnp.minimum((i + 1) * hb, nblk - 1), 0)),
            mod_spec(3), mod_spec(4), mod_spec(5),
            pl.BlockSpec((1, d), const2),
            pl.BlockSpec((None, d, f), lambda bi, i: (layer, 0, 0), **single),
            pl.BlockSpec((None, d, f), lambda bi, i: (layer, 0, 1), **single),
            pl.BlockSpec((cw.shape[0], f), const2),
            pl.BlockSpec((1, f), const2),
            pl.BlockSpec((None, f, d), lambda bi, i: (layer, 0, 0), **single),
            pl.BlockSpec((1, d), const2),
        ],
        out_specs=pl.BlockSpec((1, tm, d), lambda bi, i: (bi, i, 0)),
        out_shape=jax.ShapeDtypeStruct((b, l, d), F32),
        scratch_shapes=[pltpu.VMEM((tm, f), BF16)],
        compiler_params=_params(("parallel", "arbitrary")),
        name="ffn",
    )(x, x, x, mod4, mod4, mod4, g, w_up, w_up, cw, cb, w_down, fg)


def _trig_tables(n, row_mul, row_add, col_mul, col_add, period):
    a_sz = 64 if n % 64 == 0 else 1
    q = (jnp.arange(n, dtype=jnp.int32) * col_mul + col_add)[None, :]
    ra = (jnp.arange(n // a_sz, dtype=jnp.int32) * (a_sz * row_mul))[:, None]
    rb = (jnp.arange(a_sz, dtype=jnp.int32) * row_mul + row_add)[:, None]
    w = 2.0 * math.pi / period
    ang_a = ((ra * q) % period).astype(F32) * w
    ang_b = ((rb * q) % period).astype(F32) * w
    ca, sa = jnp.cos(ang_a)[:, None, :], jnp.sin(ang_a)[:, None, :]
    cb, sb = jnp.cos(ang_b)[None, :, :], jnp.sin(ang_b)[None, :, :]
    return ca * cb - sa * sb, sa * cb + ca * sb


def _pad2(a, rows, cols):
    return jnp.pad(a, ((0, rows - a.shape[0]), (0, cols - a.shape[1])))


def _hyena_features(l, pad_to):
    t = jnp.linspace(0.0, 1.0, l, dtype=F32)[:, None]
    bands = jnp.linspace(1e-4, HYENA_BANDS - 1, HYENA_BANDS, dtype=F32)[None, :]
    ang = (2.0 * math.pi / l) * jnp.arange(l, dtype=F32)[:, None] * bands
    feats = jnp.concatenate([t, jnp.cos(ang), -jnp.sin(ang)], axis=-1)
    return jnp.pad(feats, ((0, 0), (0, pad_to - feats.shape[1])))


def _bias_classes(rows, kh):
    r = np.arange(rows)
    r0 = np.clip(r - kh // 2, 0, rows - kh)
    off = r0 - r
    lo = kh // 2
    reps = list(range(lo)) + [lo] + list(range(rows - kh + lo + 1, rows))
    return [int(off[i]) for i in reps]


def _attention_bias(rpb, rows, kh):
    heads, nr, nc = rpb.shape
    w = GRID_W
    flat = rpb.astype(F32).reshape(heads * nr, nc)
    table = _bias_call(_pad2(flat, -(-heads * nr // 16) * 16, 128))
    table = table[:heads * nr].reshape(heads, nr, w, w)
    by_query = jnp.transpose(table, (0, 2, 1, 3)).reshape(heads * w, nr * w)
    firsts = [off + NA_WIN_H - 1 for off in _bias_classes(rows, kh)]
    return jnp.stack([by_query[:, first * w:(first + kh) * w] for first in firsts])


def _mixer_inputs(x, mod4, lw, tm):
    return _proj_call(x, mod4, lw['norm1_g'], lw['w_in'], lw['layer'], lw['cs'].shape[1], tm)


def _hyena(oh, lw, tabs, c, nb_fwd, nb, tm, rows):
    fwd, inv = tabs['hy_fwd'], tabs['hy_inv']
    uc = _dwconv_call(oh, lw['hyena_conv_w'], lw['hyena_conv_b'], c, rows)
    h, row0, asum = _filt_call(tabs['feats'], lw['fw1'], lw['fb1'], lw['fw2'], lw['fb2'], lw['fw3'],
                               lw['fb3'], lw['freq'], tabs['deltas'], min(rows * 2, oh.shape[1]))
    kk, km = _kspec_call(fwd, h, row0, asum, c, tm)
    spec, ym = _hfwd_call(fwd, uc, kk, km, 0, c, nb_fwd, tm)
    z2 = _hinv_call(inv, spec, ym, uc, 0, uc, 1, lw['skip'], 0, False, c, nb, tm)
    spec, ym = _hfwd_call(fwd, z2, kk, km, 1, c, nb_fwd, tm)
    return _hinv_call(inv, spec, ym, z2, 0, uc, 2, lw['skip'], 1, True, c, nb, tm)


def _stream_layer(x, mod4, lw, tabs, proj, y_att, cfg, final_norm):
    c = cfg['c']
    oa, oh, ob, of = proj
    y_pool = _pool_call(oa, tabs['pool_inv'], lw['pool_blk'], lw['pool_scale'], cfg['rows'])
    y_fnet = _fnet_call(tabs['fn'], of, lw['cs'], cfg['nb'], cfg['thy'])
    y_hyena = _hyena(oh, lw, tabs, c, cfg['nb'], cfg['nb'], cfg['thy'], cfg['rows'])
    x = _merge_call(x, mod4, lw['norm1_g'], (y_pool, y_fnet, y_hyena, y_att), lw['w_in'], lw['w_branch'],
                    lw['w_out'], lw['layer'], cfg['tm'])
    return _ffn_call(x, mod4, lw['norm2_g'], lw['w_up'], lw['ffn_conv_w'], lw['ffn_conv_b'],
                     lw['w_down'], lw['final_g'], lw['layer'], cfg['tm'], cfg['chunk'], final_norm)


def _seq_tables(l, c):
    q = l // RESIDUES
    b16 = lambda t: tuple(a.astype(BF16) for a in t)
    hy_fwd = sum((_trig_tables(q, 1, 0, RESIDUES, r, 2 * l) for r in range(RESIDUES)), ())
    hy_inv = sum((_trig_tables(q, RESIDUES, r, 1, 0, 2 * l) for r in range(RESIDUES)), ())
    scale = 1.0 / math.sqrt(l * (c // FNET_GROUPS))
    fn = sum((_trig_tables(q, 1, 0, RESIDUES, r, l) for r in range(RESIDUES)), ())
    deltas = jnp.linspace(math.log(HYENA_DECAY_TARGET) / HYENA_SLOW_DECAY,
                          math.log(HYENA_DECAY_TARGET) / HYENA_FAST_DECAY, c, dtype=F32)[None, :]
    feats = _hyena_features(l, 128)
    return {
        'hy_fwd': b16(hy_fwd), 'hy_inv': b16(hy_inv),
        'fn': b16(tuple(t * scale for t in fn)),
        'feats': jnp.concatenate([feats[r::RESIDUES] for r in range(RESIDUES)], axis=0), 'deltas': deltas,
        'pool_inv': _pool_inv_counts(l, c),
    }


def kernel(x, c, ctx, c_ctx, w_mod, b_mod, norm1_g, norm2_g, w_in, pool_w, pool_scale, hyena_conv_w,
           hyena_conv_b, hyena_filt_w1, hyena_filt_b1, hyena_filt_w2, hyena_filt_b2, hyena_filt_w3,
           hyena_filt_b3, hyena_freq, hyena_skip, na_rpb, w_branch, w_out, ffn_w_up, ffn_conv_w, ffn_conv_b,
           ffn_w_down, final_norm_g):
    batch, seq, d = x.shape
    lc = ctx.shape[1]
    depth = w_mod.shape[0]
    m = d // N_BRANCH
    rows = seq // GRID_W
    kh = min(NA_WIN_H, rows)

    cvec = jnp.concatenate([c, c_ctx[None], jnp.zeros((16 - batch - 1, d), F32)], axis=0)
    mod = _mod_call(cvec, w_mod, b_mod)
    mod_x = mod[:, :batch].reshape(depth, batch, 6, 1, d)
    mod_c = jnp.broadcast_to(mod[:, batch:batch + 1], (depth, batch, 6 * d)).reshape(depth, batch, 6, 1, d)

    gsz = m // FNET_GROUPS
    cc, ss = (t.reshape(gsz, gsz) for t in _trig_tables(gsz, 1, 0, 1, 0, gsz))
    eye = jnp.eye(FNET_GROUPS, dtype=F32)
    cs = jnp.concatenate([jnp.kron(eye, cc), -jnp.kron(eye, ss)], axis=0).astype(BF16)

    tabs_x = _seq_tables(seq, m)
    tabs_c = _seq_tables(lc, m)
    assert max(POOL_WINDOWS) // 2 <= PAD and batch % 2 == 0
    cfg_x = dict(c=m, tm=min(1024, seq), rows=256, nb=2, thy=min(512, seq // RESIDUES), chunk=256,
                 group=min(16, rows))
    cfg_c = dict(c=m, tm=lc, rows=lc, nb=2, thy=lc // RESIDUES, chunk=256)

    assert w_in.shape[2] == 3 * 2 * d and 8 * m == 2 * d
    w_in_b, w_branch_b, w_out_b = w_in.astype(BF16), w_branch.astype(BF16), w_out.astype(BF16)
    w_up_b, w_down_b = ffn_w_up.astype(BF16), ffn_w_down.astype(BF16)

    for l in range(depth):
        last = l == depth - 1
        lw = {
            'layer': l, 'w_in': w_in_b, 'w_branch': w_branch_b, 'w_out': w_out_b, 'w_up': w_up_b, 'w_down': w_down_b,
            'norm1_g': norm1_g[l][None], 'norm2_g': norm2_g[l][None], 'final_g': final_norm_g[None],
            'cs': cs,
            'pool_blk': jax.scipy.linalg.block_diag(*[pool_w[l, gi] for gi in range(pool_w.shape[1])]).astype(BF16),
            'pool_scale': pool_scale[l][None],
            'hyena_conv_w': hyena_conv_w[l], 'hyena_conv_b': hyena_conv_b[l][None],
            'fw1': _pad2(hyena_filt_w1[l], 128, 128), 'fb1': _pad2(hyena_filt_b1[l][None], 1, 128),
            'fw2': _pad2(hyena_filt_w2[l], 128, 128), 'fb2': _pad2(hyena_filt_b2[l][None], 1, 128),
            'fw3': _pad2(hyena_filt_w3[l], 128, hyena_filt_w3.shape[2]), 'fb3': hyena_filt_b3[l][None],
            'freq': _pad2(hyena_freq[l][None], 1, 128),
            'skip': hyena_skip[l][:, None, :],
            'ffn_conv_w': ffn_conv_w[l], 'ffn_conv_b': ffn_conv_b[l][None],
        }
        bias = _attention_bias(na_rpb[l], rows, kh)

        proj_c = _mixer_inputs(ctx, mod_c[l], lw, cfg_c['tm'])
        proj_x = _mixer_inputs(x, mod_x[l], lw, cfg_x['tm'])
        y_att = _natt_call(proj_x[2], proj_c[2], bias, m, cfg_x['group'])
        x = _stream_layer(x, mod_x[l], lw, tabs_x, proj_x, y_att, cfg_x, last)
        if not last:
            ctx = _stream_layer(ctx, mod_c[l], lw, tabs_c, proj_c, _catt_call(proj_c[2], m), cfg_c, False)
    return x
```

```python
import functools
import math

import numpy as np
import jax
import jax.numpy as jnp
from jax import lax
from jax.experimental import pallas as pl
from jax.experimental.pallas import tpu as pltpu

F32 = jnp.float32
BF16 = jnp.bfloat16

GRID_W = 64
N_BRANCH = 4
POOL_WINDOWS = (2, 4, 8, 16)
FNET_GROUPS = 4
HYENA_ORDER = 2
HYENA_BANDS = 16
HYENA_DECAY_TARGET = 1e-2
HYENA_FAST_DECAY = 0.3
HYENA_SLOW_DECAY = 1.5
HYENA_DECAY_SHIFT = 0.05
NA_HEAD_DIM = 64
NA_WIN_H = 8
NA_WIN_W = 16
EPS = 1e-6
NEG_INF = -1e30

RESIDUES = 4
SQRT_HALF = math.sqrt(0.5)
LANES = 128
HALO = 16
PAD = 8
VMEM_LIMIT = 56 * 1024 * 1024


def _params(sem):
    return pltpu.CompilerParams(dimension_semantics=sem, vmem_limit_bytes=VMEM_LIMIT)


def _dot(a, b):
    return jnp.dot(a, b, preferred_element_type=F32)


def _dot_t(a, b):
    return lax.dot_general(a, b, (((1,), (1,)), ((), ())), preferred_element_type=F32)


def _split_bf16(a):
    hi = a.astype(BF16)
    lo = (a - hi.astype(F32)).astype(BF16)
    return hi, lo


def _mod_kernel(c_ref, w_ref, b_ref, o_ref):
    a = c_ref[...]
    a = a * jax.nn.sigmoid(a)
    a_hi, a_lo = _split_bf16(a)
    w_hi, w_lo = _split_bf16(w_ref[0])
    o_ref[0] = _dot(a_hi, w_hi) + _dot(a_lo, w_hi) + _dot(a_hi, w_lo) + b_ref[0]


def _mod_call(cvec, w_mod, b_mod):
    depth, d, n = w_mod.shape
    rows = cvec.shape[0]
    tn = 1024
    return pl.pallas_call(
        _mod_kernel,
        grid=(depth, n // tn),
        in_specs=[
            pl.BlockSpec((rows, d), lambda l, j: (0, 0)),
            pl.BlockSpec((1, d, tn), lambda l, j: (l, 0, j)),
            pl.BlockSpec((1, 1, tn), lambda l, j: (l, 0, j)),
        ],
        out_specs=pl.BlockSpec((1, rows, tn), lambda l, j: (l, 0, j)),
        out_shape=jax.ShapeDtypeStruct((depth, rows, n), F32),
        compiler_params=_params(("arbitrary", "arbitrary")),
        name="mod",
    )(cvec, w_mod, b_mod.reshape(depth, 1, n))


def _ada_norm(x, g, shift, scale):
    y = x * lax.rsqrt(jnp.mean(x * x, axis=-1, keepdims=True) + EPS)
    return y * (g * (1.0 + scale)) + shift


def _deinterleave(y, scr_ref, ways):
    n = y.shape[0]
    parts = [[] for _ in range(ways)]
    for cb in range(y.shape[1] // LANES):
        scr_ref[cb] = y[:, cb * LANES:(cb + 1) * LANES]
        for k in range(ways):
            parts[k].append(scr_ref[cb, pl.ds(k, n // ways, stride=ways), :])
    return [jnp.concatenate(p, axis=1) for p in parts]


def _interleave(parts, scr_ref):
    ways = len(parts)
    n = parts[0].shape[0]
    cols = []
    for cb in range(parts[0].shape[1] // LANES):
        for k in range(ways):
            scr_ref[cb, pl.ds(k, n, stride=ways), :] = parts[k][:, cb * LANES:(cb + 1) * LANES]
        cols.append(scr_ref[cb])
    return jnp.concatenate(cols, axis=1)


def _proj_kernel(x_ref, sh_ref, sc_ref, g_ref, w_ref, oa_ref, oh_ref, ob_ref, of_ref, scr_ref):
    m = oa_ref.shape[2]
    h = _ada_norm(x_ref[0], g_ref[...], sh_ref[0, 0], sc_ref[0, 0]).astype(BF16)
    p = _dot(h, w_ref[...])
    oa_ref[0] = p[:, :m]
    oh_ref[0] = p[:, 2 * m:5 * m].astype(BF16)
    ob_ref[0] = p[:, 5 * m:8 * m].astype(BF16)
    for k, part in enumerate(_deinterleave(p[:, m:2 * m], scr_ref, RESIDUES)):
        of_ref[0, k] = part.astype(BF16)


def _proj_call(x, mod4, g, w_in, layer, m, tm):
    b, l, d = x.shape
    const = lambda bi, i: (0, 0)
    return pl.pallas_call(
        _proj_kernel,
        grid=(b, l // tm),
        in_specs=[
            pl.BlockSpec((1, tm, d), lambda bi, i: (bi, i, 0)),
            pl.BlockSpec((1, 1, 1, d), lambda bi, i: (bi, 0, 0, 0)),
            pl.BlockSpec((1, 1, 1, d), lambda bi, i: (bi, 1, 0, 0)),
            pl.BlockSpec((1, d), const),
            pl.BlockSpec((None, d, 8 * m), lambda bi, i: (layer, 0, 0)),
        ],
        out_specs=[
            pl.BlockSpec((1, tm, m), lambda bi, i: (bi, i, 0)),
            pl.BlockSpec((1, tm, 3 * m), lambda bi, i: (bi, i, 0)),
            pl.BlockSpec((1, tm, 3 * m), lambda bi, i: (bi, i, 0)),
            pl.BlockSpec((1, RESIDUES, tm // RESIDUES, m), lambda bi, i: (bi, 0, i, 0)),
        ],
        out_shape=[
            jax.ShapeDtypeStruct((b, l, m), F32),
            jax.ShapeDtypeStruct((b, l, 3 * m), BF16),
            jax.ShapeDtypeStruct((b, l, 3 * m), BF16),
            jax.ShapeDtypeStruct((b, RESIDUES, l // RESIDUES, m), BF16),
        ],
        scratch_shapes=[pltpu.VMEM((m // LANES, tm, LANES), F32)],
        compiler_params=_params(("parallel", "parallel")),
        name="proj",
    )(x, mod4, mod4, g, w_in)


def _fill_padded(src_ref, pad_ref, l, rows):
    c = pad_ref.shape[1]
    pad_ref[0:PAD, :] = jnp.zeros((PAD, c), F32)
    pad_ref[l + PAD:l + 2 * PAD, :] = jnp.zeros((PAD, c), F32)

    def copy(i, carry):
        r = pl.multiple_of(i * rows, rows)
        pad_ref[pl.ds(r + PAD, rows), :] = src_ref[0, pl.ds(r, rows), :].astype(F32)
        return carry

    lax.fori_loop(0, l // rows, copy, 0)


def _centred_window_sums(u, windows):
    n = u.shape[0]
    fwd = {1: u}
    w = 1
    while w < max(windows):
        fwd[2 * w] = fwd[w] + pltpu.roll(fwd[w], n - w, 0)
        w *= 2
    return {w: pltpu.roll(fwd[w], w // 2, 0) for w in windows}


def _pool_kernel(u_ref, inv_ref, w_ref, s_ref, o_ref, pad_ref, *, rows):
    l, c = u_ref.shape[1], u_ref.shape[2]
    gw = c // len(POOL_WINDOWS)
    per_block = LANES // gw
    _fill_padded(u_ref, pad_ref, l, rows)
    n = rows + 2 * PAD
    lane = lax.broadcasted_iota(jnp.int32, (n, LANES), 1)
    w = w_ref[...]
    scale = s_ref[...]

    def body(i, carry):
        r = pl.multiple_of(i * rows, rows)
        cols = []
        for cb in range(c // LANES):
            lanes = slice(cb * LANES, (cb + 1) * LANES)
            win = pad_ref[pl.ds(r, n), lanes]
            wins = POOL_WINDOWS[cb * per_block:(cb + 1) * per_block]
            sums = _centred_window_sums(win, wins)
            s = sums[wins[-1]]
            for k in range(per_block - 2, -1, -1):
                s = jnp.where(lane < (k + 1) * gw, sums[wins[k]], s)
            cols.append((s * inv_ref[pl.ds(r, n), lanes] - win)[PAD:PAD + rows])
        y = jnp.concatenate(cols, axis=1)
        o_ref[0, pl.ds(r, rows), :] = (_dot(y.astype(BF16), w) * scale).astype(BF16)
        return carry

    lax.fori_loop(0, l // rows, body, 0)


def _pool_inv_counts(l, c):
    gw = c // len(POOL_WINDOWS)
    t = jnp.arange(-PAD, l + PAD, dtype=jnp.int32)[:, None]
    half = jnp.asarray(np.repeat(np.array(POOL_WINDOWS) // 2, gw), jnp.int32)[None, :]
    cnt = jnp.minimum(t + half, l) - jnp.maximum(t - half, 0)
    return 1.0 / jnp.maximum(cnt, 1).astype(F32)


def _pool_call(oa, inv_cnt, w_blk, scale, rows):
    b, l, _ = oa.shape
    c = w_blk.shape[0]
    return pl.pallas_call(
        functools.partial(_pool_kernel, rows=rows),
        grid=(b,),
        in_specs=[
            pl.BlockSpec((1, l, c), lambda bi: (bi, 0, 0)),
            pl.BlockSpec((l + 2 * PAD, c), lambda bi: (0, 0)),
            pl.BlockSpec((c, c), lambda bi: (0, 0)),
            pl.BlockSpec((1, c), lambda bi: (0, 0)),
        ],
        out_specs=pl.BlockSpec((1, l, c), lambda bi: (bi, 0, 0)),
        out_shape=jax.ShapeDtypeStruct((b, l, c), BF16),
        scratch_shapes=[pltpu.VMEM((l + 2 * PAD, c), F32)],
        compiler_params=_params(("parallel",)),
        name="pool",
    )(oa, inv_cnt, w_blk, scale)


def _dwconv_kernel(u_ref, w_ref, b_ref, o_ref, pad_ref, *, rows):
    l, c = u_ref.shape[1], u_ref.shape[2]
    nlb = c // LANES
    sub = rows // RESIDUES
    for cb in range(nlb):
        pad_ref[cb, 0:PAD, :] = jnp.zeros((PAD, LANES), F32)
        pad_ref[cb, l + PAD:l + 2 * PAD, :] = jnp.zeros((PAD, LANES), F32)

    def copy(i, carry):
        r = pl.multiple_of(i * rows, rows)
        blk = u_ref[0, pl.ds(r, rows), :].astype(F32)
        for cb in range(nlb):
            pad_ref[cb, pl.ds(r + PAD, rows), :] = blk[:, cb * LANES:(cb + 1) * LANES]
        return carry

    lax.fori_loop(0, l // rows, copy, 0)
    w = w_ref[...]
    bias = b_ref[...]

    def body(i, carry):
        r = i * rows
        rs = pl.multiple_of(i * sub, sub)
        for cb in range(nlb):
            lanes = slice(cb * LANES, (cb + 1) * LANES)
            taps = [pad_ref[cb, pl.ds(r + PAD - 1 + k, sub, stride=RESIDUES), :] for k in range(RESIDUES + 2)]
            for k in range(RESIDUES):
                y = taps[k] * w[0:1, lanes] + taps[k + 1] * w[1:2, lanes] + taps[k + 2] * w[2:3, lanes] + bias[:, lanes]
                o_ref[0, k, pl.ds(rs, sub), lanes] = y.astype(BF16)
        return carry

    lax.fori_loop(0, l // rows, body, 0)


def _dwconv_call(u, w, bias, c, rows):
    b, l, _ = u.shape
    nblk = w.shape[1] // c
    return pl.pallas_call(
        functools.partial(_dwconv_kernel, rows=rows),
        grid=(b, nblk),
        in_specs=[
            pl.BlockSpec((1, l, c), lambda bi, j: (bi, 0, j)),
            pl.BlockSpec((w.shape[0], c), lambda bi, j: (0, j)),
            pl.BlockSpec((1, c), lambda bi, j: (0, j)),
        ],
        out_specs=pl.BlockSpec((1, RESIDUES, l // RESIDUES, c), lambda bi, j: (bi, 0, 0, j)),
        out_shape=jax.ShapeDtypeStruct((b, RESIDUES, l // RESIDUES, nblk * c), BF16),
        scratch_shapes=[pltpu.VMEM((c // LANES, l + 2 * PAD, LANES), F32)],
        compiler_params=_params(("parallel", "parallel")),
        name="dwconv",
    )(u, w, bias)


def _filt_kernel(f_ref, w1_ref, b1_ref, w2_ref, b2_ref, w3_ref, b3_ref, fr_ref, dl_ref,
                 hb_ref, row0_ref, asum_ref):
    hp = lax.Precision.HIGHEST
    feats = f_ref[...]
    freq = fr_ref[...]
    h = jnp.sin(freq * (jnp.dot(feats, w1_ref[...], precision=hp, preferred_element_type=F32) + b1_ref[...]))
    h = jnp.sin(freq * (jnp.dot(h, w2_ref[...], precision=hp, preferred_element_type=F32) + b2_ref[...]))
    h = jnp.dot(h, w3_ref[...], precision=hp, preferred_element_type=F32) + b3_ref[...]
    t = feats[:, 0:1]
    win = jnp.exp(-t * jnp.abs(dl_ref[...])) + HYENA_DECAY_SHIFT
    h = h * jnp.concatenate([win] * (h.shape[1] // win.shape[1]), axis=1)
    hb = h.astype(BF16)
    hb_ref[...] = hb
    part = jnp.sum(jnp.abs(h), axis=0, keepdims=True)

    @pl.when(pl.program_id(0) == 0)
    def _():
        asum_ref[...] = jnp.zeros_like(asum_ref)
        row0_ref[...] = hb[0:16].astype(F32)[0:8]

    asum_ref[...] += part


def _filt_call(feats, w1, b1, w2, b2, w3, b3, freq, deltas, rows):
    l, fd = feats.shape
    hd = w2.shape[0]
    n = w3.shape[1]
    c = deltas.shape[1]
    const = lambda i: (0, 0)
    return pl.pallas_call(
        _filt_kernel,
        grid=(l // rows,),
        in_specs=[
            pl.BlockSpec((rows, fd), lambda i: (i, 0)),
            pl.BlockSpec((fd, hd), const), pl.BlockSpec((1, hd), const),
            pl.BlockSpec((hd, hd), const), pl.BlockSpec((1, hd), const),
            pl.BlockSpec((hd, n), const), pl.BlockSpec((1, n), const),
            pl.BlockSpec((1, hd), const), pl.BlockSpec((1, c), const),
        ],
        out_specs=[
            pl.BlockSpec((rows, n), lambda i: (i, 0)),
            pl.BlockSpec((8, n), const),
            pl.BlockSpec((1, n), const),
        ],
        out_shape=[
            jax.ShapeDtypeStruct((l, n), BF16),
            jax.ShapeDtypeStruct((8, n), F32),
            jax.ShapeDtypeStruct((1, n), F32),
        ],
        compiler_params=_params(("arbitrary",)),
        name="hyena_filt",
    )(feats, w1, b1, w2, b2, w3, b3, freq, deltas)


def _alt_signs(n):
    t = lax.broadcasted_iota(jnp.int32, (16, n), 1)
    return jnp.where((t & 1) == 0, 1.0, -1.0).astype(BF16)


def _mat_specs(tm, tables):
    n, _, a, cols = tables.shape
    return [pl.BlockSpec((n, tm // a, a, cols), lambda g, i: (0, i, 0, 0))]


def _mat(ref, k):
    return ref[k].reshape(ref.shape[1] * ref.shape[2], ref.shape[3])


def _quad_transform(tab_ref, parts):
    a = [_dot(_mat(tab_ref, 2 * r), parts[r]) for r in range(RESIDUES)]
    b = [_dot(_mat(tab_ref, 2 * r + 1), parts[r]) for r in range(RESIDUES)]
    ea, fa, ga, ha = a[0] + a[2], a[0] - a[2], a[1] + a[3], a[1] - a[3]
    eb, fb, gb, hb = b[0] + b[2], b[0] - b[2], b[1] + b[3], b[1] - b[3]
    zr = (ea + ga, fa + hb, fa - hb, ea - ga)
    zs = (eb + gb, ha - fb, ha + fb, gb - eb)
    return zr, zs


def _quad_special(alt):
    c1, c3 = SQRT_HALF * alt[1], SQRT_HALF * alt[3]
    return (alt[0] + c1 - c3, c1 + alt[2] + c3), (alt[0] - c1 + c3, c1 - alt[2] + c3)


def _kspec_kernel(*refs):
    mat_refs, (h_ref, row0_ref, asum_ref, kk_ref, km_ref) = refs[0], refs[1:]
    i = pl.program_id(1)
    tm, q = kk_ref.shape[2], mat_refs.shape[3]
    c = kk_ref.shape[3]
    l = RESIDUES * q
    parts = [h_ref[r * q:(r + 1) * q, :] for r in range(RESIDUES)]
    zr, zs = _quad_transform(mat_refs, parts)
    asum = asum_ref[...]
    inv = 1.0 / (asum[:, :c] + asum[:, c:] + EPS)
    hb0 = row0_ref[0:1, c:]
    row = lax.broadcasted_iota(jnp.int32, (tm, c), 0) + i * tm
    wj = jnp.where(row == 0, 0.5 / l, 1.0 / l) * inv
    for f in range(4):
        kk_ref[0, 2 * f] = (zr[f][:, :c] + zr[f][:, c:] - hb0) * wj
        kk_ref[0, 2 * f + 1] = (zs[f][:, :c] - zs[f][:, c:]) * wj

    @pl.when(i == 0)
    def _():
        sg = _alt_signs(q)
        alt = [_dot(sg, p)[0:8] for p in parts]
        for f, (xr, xs) in enumerate(_quad_special(alt)):
            km_ref[0, :, 2 * f * c:(2 * f + 1) * c] = (xr[:, :c] + xr[:, c:] - hb0) * inv * (1.0 / l)
            km_ref[0, :, (2 * f + 1) * c:(2 * f + 2) * c] = (xs[:, :c] - xs[:, c:]) * inv * (1.0 / l)


def _kspec_call(mats, h, row0, asum, c, tm):
    q = mats.shape[3]
    l = RESIDUES * q
    orders = h.shape[1] // (2 * c)
    return pl.pallas_call(
        _kspec_kernel,
        grid=(orders, q // tm),
        in_specs=_mat_specs(tm, mats) + [
            pl.BlockSpec((l, 2 * c), lambda o, i: (0, o)),
            pl.BlockSpec((8, 2 * c), lambda o, i: (0, o)),
            pl.BlockSpec((1, 2 * c), lambda o, i: (0, o)),
        ],
        out_specs=[
            pl.BlockSpec((1, 8, tm, c), lambda o, i: (o, 0, i, 0)),
            pl.BlockSpec((1, 8, 4 * c), lambda o, i: (o, 0, 0)),
        ],
        out_shape=[
            jax.ShapeDtypeStruct((orders, 8, q, c), F32),
            jax.ShapeDtypeStruct((orders, 8, 4 * c), F32),
        ],
        compiler_params=_params(("arbitrary", "arbitrary")),
        name="hyena_kspec",
    )(mats, h, row0, asum)


def _cmul(zr, zs, kr, ks):
    return zr * kr - zs * ks, zr * ks + zs * kr


def _hfwd_kernel(*refs):
    mat_refs, (z_ref, kk_ref, km_ref, spec_ref, ym_ref) = refs[0], refs[1:]
    nb = z_ref.shape[0]
    c = z_ref.shape[3]
    for bb in range(nb):
        zr, zs = _quad_transform(mat_refs, [z_ref[bb, r] for r in range(RESIDUES)])
        yr, ys = zip(*[_cmul(zr[f], zs[f], kk_ref[0, 2 * f], kk_ref[0, 2 * f + 1]) for f in range(4)])
        a, b, u1, u2 = yr[0] + yr[3], yr[0] - yr[3], yr[1] + yr[2], ys[1] + ys[2]
        cc, dd, d1, d2 = ys[0] - ys[3], ys[0] + ys[3], yr[1] - yr[2], ys[2] - ys[1]
        for k, v in enumerate((a + u1, cc + d2, b + u2, dd + d1, a - u1, cc - d2, b - u2, dd - d1)):
            spec_ref[bb, k] = v.astype(BF16)

    @pl.when(pl.program_id(1) == 0)
    def _():
        sg = _alt_signs(z_ref.shape[2])
        for bb in range(nb):
            alt = [_dot(sg, z_ref[bb, r])[0:8] for r in range(RESIDUES)]
            (zq_r, zq_s), (z3_r, z3_s) = _quad_special(alt)
            yq_r, yq_s = _cmul(zq_r, zq_s, km_ref[0, :, 0:c], km_ref[0, :, c:2 * c])
            y3_r, y3_s = _cmul(z3_r, z3_s, km_ref[0, :, 2 * c:3 * c], km_ref[0, :, 3 * c:4 * c])
            ym_ref[bb, :, 0:c] = yq_r + y3_r
            ym_ref[bb, :, c:2 * c] = SQRT_HALF * (yq_r + yq_s - y3_r + y3_s)
            ym_ref[bb, :, 2 * c:3 * c] = yq_s - y3_s
            ym_ref[bb, :, 3 * c:4 * c] = SQRT_HALF * (yq_s - yq_r + y3_r + y3_s)


def _hfwd_call(mats, zb, kk, km, order, c, nb, tm):
    b, _, q, _ = zb.shape
    return pl.pallas_call(
        _hfwd_kernel,
        grid=(b // nb, q // tm),
        in_specs=_mat_specs(tm, mats) + [
            pl.BlockSpec((nb, RESIDUES, q, c), lambda g, i: (g, 0, 0, 0)),
            pl.BlockSpec((1, 8, tm, c), lambda g, i: (order, 0, i, 0)),
            pl.BlockSpec((1, 8, 4 * c), lambda g, i: (order, 0, 0)),
        ],
        out_specs=[
            pl.BlockSpec((nb, 8, tm, c), lambda g, i: (g, 0, i, 0)),
            pl.BlockSpec((nb, 8, 4 * c), lambda g, i: (g, 0, 0)),
        ],
        out_shape=[
            jax.ShapeDtypeStruct((b, 8, q, c), BF16),
            jax.ShapeDtypeStruct((b, 8, 4 * c), F32),
        ],
        compiler_params=_params(("arbitrary", "arbitrary")),
        name="hyena_fwd",
    )(mats, zb, kk, km)


def _hinv_kernel(*refs, natural_out):
    mat_refs, (spec_ref, ym_ref, z_ref, gate_ref, skip_ref), out_refs = refs[0], refs[1:6], refs[6:]
    nb, _, tm, c = z_ref.shape
    row = lax.broadcasted_iota(jnp.int32, (tm, c), 0)
    even_row = (row & 1) == 0
    skip = skip_ref[0]
    for bb in range(nb):
        outs = []
        for r in range(RESIDUES):
            m = ym_ref[bb, 0:1, r * c:(r + 1) * c]
            y = (_dot(_mat(mat_refs, 2 * r), spec_ref[bb, 2 * r]) + _dot(_mat(mat_refs, 2 * r + 1), spec_ref[bb, 2 * r + 1])
                 + jnp.where(even_row, m, -m))
            outs.append(gate_ref[bb, r].astype(F32) * (y + skip * z_ref[bb, r].astype(F32)))
        if natural_out:
            o_ref, scr_ref = out_refs
            o_ref[bb] = _interleave(outs, scr_ref).astype(BF16)
        else:
            (o_ref,) = out_refs
            for r in range(RESIDUES):
                o_ref[bb, r] = outs[r].astype(BF16)


def _hinv_call(mats, spec, ym, zf, zcol, uc, gcol, skip, order, natural_out, c, nb, tm):
    b, _, q, _ = spec.shape
    if natural_out:
        out_specs = pl.BlockSpec((nb, RESIDUES * tm, c), lambda g, i: (g, i, 0))
        out_shape = jax.ShapeDtypeStruct((b, RESIDUES * q, c), BF16)
        scratch = [pltpu.VMEM((c // LANES, RESIDUES * tm, LANES), F32)]
    else:
        out_specs = pl.BlockSpec((nb, RESIDUES, tm, c), lambda g, i: (g, 0, i, 0))
        out_shape = jax.ShapeDtypeStruct((b, RESIDUES, q, c), BF16)
        scratch = []
    return pl.pallas_call(
        functools.partial(_hinv_kernel, natural_out=natural_out),
        grid=(b // nb, q // tm),
        in_specs=_mat_specs(tm, mats) + [
            pl.BlockSpec((nb, 8, q, c), lambda g, i: (g, 0, 0, 0)),
            pl.BlockSpec((nb, 8, 4 * c), lambda g, i: (g, 0, 0)),
            pl.BlockSpec((nb, RESIDUES, tm, c), lambda g, i: (g, 0, i, zcol)),
            pl.BlockSpec((nb, RESIDUES, tm, c), lambda g, i: (g, 0, i, gcol)),
            pl.BlockSpec((1, 1, c), lambda g, i: (order, 0, 0)),
        ],
        out_specs=out_specs,
        out_shape=out_shape,
        scratch_shapes=scratch,
        compiler_params=_params(("arbitrary", "arbitrary")),
        name="hyena_inv",
    )(mats, spec, ym, zf, uc, skip)


def _fnet_kernel(*refs):
    mat_refs, (u_ref, cs_ref, o_ref) = refs[0], refs[1:]
    nb = u_ref.shape[0]
    cs = cs_ref[...]
    for bb in range(nb):
        parts = [u_ref[bb, r] for r in range(RESIDUES)]
        a = [_dot(_mat(mat_refs, 2 * r), parts[r]) for r in range(RESIDUES)]
        b = [_dot(_mat(mat_refs, 2 * r + 1), parts[r]) for r in range(RESIDUES)]
        ea, fa, ga, ha = a[0] + a[2], a[0] - a[2], a[1] + a[3], a[1] - a[3]
        eb, fb, gb, hb = b[0] + b[2], b[0] - b[2], b[1] + b[3], b[1] - b[3]
        uc = (ea + ga, fa - hb, ea - ga, fa + hb)
        us = (eb + gb, fb + ha, eb - gb, fb - ha)
        for m in range(RESIDUES):
            both = jnp.concatenate([uc[m].astype(BF16), us[m].astype(BF16)], axis=1)
            o_ref[bb, m] = _dot(both, cs).astype(BF16)


def _fnet_call(mats, u4, cs, nb, tm):
    b, _, q, c = u4.shape
    out = pl.pallas_call(
        _fnet_kernel,
        grid=(b // nb, q // tm),
        in_specs=_mat_specs(tm, mats) + [
            pl.BlockSpec((nb, RESIDUES, q, c), lambda g, i: (g, 0, 0, 0)),
            pl.BlockSpec(cs.shape, lambda g, i: (0, 0)),
        ],
        out_specs=pl.BlockSpec((nb, RESIDUES, tm, c), lambda g, i: (g, 0, i, 0)),
        out_shape=jax.ShapeDtypeStruct((b, RESIDUES, q, c), BF16),
        compiler_params=_params(("arbitrary", "arbitrary")),
        name="fnet",
    )(mats, u4, cs)
    return out.reshape(b, RESIDUES * q, c)


def _stack_heads(q, heads):
    lane = lax.broadcasted_iota(jnp.int32, q.shape, 1)
    zero = jnp.zeros_like(q)
    return jnp.concatenate([jnp.where(lax.shift_right_logical(lane, int(math.log2(NA_HEAD_DIM))) == h, q, zero) for h in range(heads)], axis=0)


def _merge_heads(o, heads, n):
    lane = lax.broadcasted_iota(jnp.int32, (n, o.shape[1]), 1)
    out = jnp.zeros((n, o.shape[1]), F32)
    for h in range(heads):
        out = out + jnp.where(lax.shift_right_logical(lane, int(math.log2(NA_HEAD_DIM))) == h, o[h * n:(h + 1) * n], 0.0)
    return out


def _natt_kernel(q_ref, k_ref, v_ref, kc_ref, vc_ref, bias_ref, o_ref, *, rows, kh, group):
    w = GRID_W
    heads = q_ref.shape[2] // NA_HEAD_DIM
    base = pl.program_id(1) * group
    lo = kh // 2
    kc, vc = kc_ref[0], vc_ref[0]
    scale = jnp.asarray(NA_HEAD_DIM ** -0.5, BF16)

    def body(j, carry):
        r = base + j
        r0 = jnp.clip(r - lo, 0, rows - kh)
        start = pl.multiple_of(r0 * w, w)
        cls = jnp.minimum(r, lo) + jnp.maximum(r - (rows - kh + lo), 0)
        q4 = _stack_heads(q_ref[0, pl.ds(pl.multiple_of(j * w, w), w), :] * scale, heads)
        ks = k_ref[0, pl.ds(start, kh * w), :]
        vs = v_ref[0, pl.ds(start, kh * w), :]
        s_nb = _dot_t(q4, ks) + bias_ref[cls]
        s_cx = _dot_t(q4, kc)
        m = jnp.maximum(jnp.max(s_nb, axis=-1, keepdims=True), jnp.max(s_cx, axis=-1, keepdims=True))
        p_nb = jnp.exp(s_nb - m)
        p_cx = jnp.exp(s_cx - m)
        den = jnp.sum(p_nb, axis=-1, keepdims=True) + jnp.sum(p_cx, axis=-1, keepdims=True)
        o4 = (_dot(p_nb.astype(BF16), vs) + _dot(p_cx.astype(BF16), vc)) / den
        o_ref[0, pl.ds(pl.multiple_of(j * w, w), w), :] = _merge_heads(o4, heads, w).astype(BF16)
        return carry

    lax.fori_loop(0, group, body, 0, unroll=4)


def _natt_call(ob, obc, bias, c, group):
    b, l, _ = ob.shape
    lc = obc.shape[1]
    w = GRID_W
    rows = l // w
    kh = min(NA_WIN_H, rows)
    return pl.pallas_call(
        functools.partial(_natt_kernel, rows=rows, kh=kh, group=group),
        grid=(b, rows // group),
        in_specs=[
            pl.BlockSpec((1, group * w, c), lambda bi, r: (bi, r, 0)),
            pl.BlockSpec((1, l, c), lambda bi, r: (bi, 0, 1)),
            pl.BlockSpec((1, l, c), lambda bi, r: (bi, 0, 2)),
            pl.BlockSpec((1, lc, c), lambda bi, r: (bi, 0, 1)),
            pl.BlockSpec((1, lc, c), lambda bi, r: (bi, 0, 2)),
            pl.BlockSpec(bias.shape, lambda bi, r: (0, 0, 0)),
        ],
        out_specs=pl.BlockSpec((1, group * w, c), lambda bi, r: (bi, r, 0)),
        out_shape=jax.ShapeDtypeStruct((b, l, c), BF16),
        compiler_params=_params(("parallel", "arbitrary")),
        name="nb_attn",
    )(ob, ob, ob, obc, obc, bias)


def _bias_kernel(r_ref, o_ref):
    w = GRID_W
    shift = int(math.log2(w))
    n = o_ref.shape[1]
    r = r_ref[...]
    hi = r.astype(BF16)
    mid = (r - hi.astype(F32)).astype(BF16)
    lo = (r - hi.astype(F32) - mid.astype(F32)).astype(BF16)
    dc = lax.broadcasted_iota(jnp.int32, (r.shape[1], n), 0)
    p = lax.broadcasted_iota(jnp.int32, (r.shape[1], n), 1)
    idx = jnp.clip((p & (w - 1)) - lax.shift_right_logical(p, shift), 1 - NA_WIN_W, NA_WIN_W - 1) + (NA_WIN_W - 1)
    onehot = jnp.where(dc == idx, 1.0, 0.0).astype(BF16)
    val = _dot(hi, onehot) + _dot(mid, onehot) + _dot(lo, onehot)
    po = lax.broadcasted_iota(jnp.int32, val.shape, 1)
    cq = lax.shift_right_logical(po, shift)
    ck = po & (w - 1)
    c0 = jnp.clip(cq - NA_WIN_W // 2, 0, w - NA_WIN_W)
    inside = jnp.logical_and(ck >= c0, ck < c0 + NA_WIN_W)
    o_ref[...] = jnp.where(inside, val, NEG_INF)


def _bias_call(rpb2d):
    rows, cols = rpb2d.shape
    n = GRID_W * GRID_W
    return pl.pallas_call(
        _bias_kernel,
        grid=(1,),
        in_specs=[pl.BlockSpec((rows, cols), lambda i: (0, 0))],
        out_specs=pl.BlockSpec((rows, n), lambda i: (0, 0)),
        out_shape=jax.ShapeDtypeStruct((rows, n), F32),
        compiler_params=_params(("arbitrary",)),
        name="rpb_table",
    )(rpb2d)


def _catt_kernel(q_ref, k_ref, v_ref, o_ref):
    n = q_ref.shape[1]
    heads = q_ref.shape[2] // NA_HEAD_DIM
    q = q_ref[0] * jnp.asarray(NA_HEAD_DIM ** -0.5, BF16)
    q4 = _stack_heads(q, heads)
    s = _dot_t(q4, k_ref[0])
    p = jnp.exp(s - jnp.max(s, axis=-1, keepdims=True))
    den = jnp.sum(p, axis=-1, keepdims=True)
    o4 = _dot(p.astype(BF16), v_ref[0]) / den
    o_ref[0] = _merge_heads(o4, heads, n).astype(BF16)


def _catt_call(obc, c):
    b, lc, _ = obc.shape
    return pl.pallas_call(
        _catt_kernel,
        grid=(b,),
        in_specs=[pl.BlockSpec((1, lc, c), lambda bi, j=j: (bi, 0, j)) for j in range(3)],
        out_specs=pl.BlockSpec((1, lc, c), lambda bi: (bi, 0, 0)),
        out_shape=jax.ShapeDtypeStruct((b, lc, c), BF16),
        compiler_params=_params(("parallel",)),
        name="ctx_attn",
    )(obc, obc, obc)


def _merge_kernel(x_ref, sh_ref, sc_ref, gt_ref, g_ref, y0_ref, y1_ref, y2_ref, y3_ref,
                  wg0_ref, wg1_ref, wb_ref, wo_ref, o_ref):
    d = x_ref.shape[2]
    x = x_ref[0]
    h = _ada_norm(x, g_ref[...], sh_ref[0, 0], sc_ref[0, 0]).astype(BF16)
    merged = None
    for bi, y_ref in enumerate((y0_ref, y1_ref, y2_ref, y3_ref)):
        wg_ref = (wg0_ref, wg1_ref)[bi // 2]
        gate = jax.nn.sigmoid(_dot(h, wg_ref[:, (bi % 2) * d:(bi % 2 + 1) * d]))
        term = gate * _dot(y_ref[0].astype(BF16), wb_ref[bi])
        merged = term if merged is None else merged + term
    o_ref[0] = x + gt_ref[0, 0] * _dot(merged.astype(BF16), wo_ref[...])


def _merge_call(x, mod4, g, ys, w_in, wb, wo, layer, tm):
    b, l, d = x.shape
    c = ys[0].shape[2]
    gate_blk = w_in.shape[2] // (2 * d) - 2
    assert (gate_blk + 2) * 2 * d == w_in.shape[2]
    single = dict(pipeline_mode=pl.Buffered(1))
    mod_spec = lambda k: pl.BlockSpec((1, 1, 1, d), lambda bi, i: (bi, k, 0, 0))
    return pl.pallas_call(
        _merge_kernel,
        grid=(b, l // tm),
        in_specs=[
            pl.BlockSpec((1, tm, d), lambda bi, i: (bi, i, 0)),
            mod_spec(0), mod_spec(1), mod_spec(2),
            pl.BlockSpec((1, d), lambda bi, i: (0, 0)),
        ] + [pl.BlockSpec((1, tm, c), lambda bi, i: (bi, i, 0)) for _ in ys] + [
            pl.BlockSpec((None, d, 2 * d), lambda bi, i: (layer, 0, gate_blk), **single),
            pl.BlockSpec((None, d, 2 * d), lambda bi, i: (layer, 0, gate_blk + 1), **single),
            pl.BlockSpec((None,) + wb.shape[1:], lambda bi, i: (layer, 0, 0, 0), **single),
            pl.BlockSpec((None, d, d), lambda bi, i: (layer, 0, 0), **single),
        ],
        out_specs=pl.BlockSpec((1, tm, d), lambda bi, i: (bi, i, 0)),
        out_shape=jax.ShapeDtypeStruct((b, l, d), F32),
        compiler_params=_params(("parallel", "parallel")),
        name="merge",
    )(x, mod4, mod4, mod4, g, *ys, w_in, w_in, wb, wo)


def _ffn_kernel(x_ref, xp_ref, xn_ref, sh_ref, sc_ref, gt_ref, g_ref, wu_ref, wgt_ref, cw_ref, cb_ref,
                wd_ref, fg_ref, o_ref, a_ref, *, chunk, final_norm):
    i = pl.program_id(1)
    last = pl.num_programs(1) - 1
    tm = x_ref.shape[1]
    f = wu_ref.shape[1]
    n = tm + 2 * HALO
    g, sh, sc = g_ref[...], sh_ref[0, 0], sc_ref[0, 0]
    x = x_ref[0]
    h = _ada_norm(x, g, sh, sc).astype(BF16)
    hp = _ada_norm(xp_ref[0], g, sh, sc).astype(BF16)
    hn = _ada_norm(xn_ref[0], g, sh, sc).astype(BF16)
    hext = jnp.concatenate([hp, h, hn], axis=0)
    row = lax.broadcasted_iota(jnp.int32, (n, chunk), 0)
    valid = jnp.logical_and(jnp.logical_or(row >= HALO, i > 0), jnp.logical_or(row < HALO + tm, i < last))
    for j in range(f // chunk):
        cols = slice(j * chunk, (j + 1) * chunk)
        gp = jnp.where(valid, _dot(hext, wgt_ref[:, cols]), 0.0)
        cw = cw_ref[:, cols]
        gc = (pltpu.roll(gp, 1, 0) * cw[0:1] + gp * cw[1:2] + pltpu.roll(gp, n - 1, 0) * cw[2:3]
              + cb_ref[:, cols])[HALO:HALO + tm]
        u = _dot(h, wu_ref[:, cols])
        a_ref[:, cols] = (gc * jax.nn.sigmoid(gc) * u).astype(BF16)
    y = x + gt_ref[0, 0] * _dot(a_ref[...], wd_ref[...])
    if final_norm:
        y = y * lax.rsqrt(jnp.mean(y * y, axis=-1, keepdims=True) + EPS) * fg_ref[...]
    o_ref[0] = y


def _ffn_call(x, mod4, g, w_up, cw, cb, w_down, fg, layer, tm, chunk, final_norm):
    b, l, d = x.shape
    f = w_down.shape[1]
    hb = tm // HALO
    nblk = l // HALO
    const2 = lambda bi, i: (0, 0)
    mod_spec = lambda k: pl.BlockSpec((1, 1, 1, d), lambda bi, i: (bi, k, 0, 0))
    single = dict(pipeline_mode=pl.Buffered(1))
    return pl.pallas_call(
        functools.partial(_ffn_kernel, chunk=chunk, final_norm=final_norm),
        grid=(b, l // tm),
        in_specs=[
            pl.BlockSpec((1, tm, d), lambda bi, i: (bi, i, 0)),
            pl.BlockSpec((1, HALO, d), lambda bi, i: (bi, jnp.maximum(i * hb - 1, 0), 0)),
            pl.BlockSpec((1, HALO, d), lambda bi, i: (bi, jnp.minimum((i + 1) * hb, nblk - 1), 0)),
            mod_spec(3), mod_spec(4), mod_spec(5),
            pl.BlockSpec((1, d), const2),
            pl.BlockSpec((None, d, f), lambda bi, i: (layer, 0, 0), **single),
            pl.BlockSpec((None, d, f), lambda bi, i: (layer, 0, 1), **single),
            pl.BlockSpec((cw.shape[0], f), const2),
            pl.BlockSpec((1, f), const2),
            pl.BlockSpec((None, f, d), lambda bi, i: (layer, 0, 0), **single),
            pl.BlockSpec((1, d), const2),
        ],
        out_specs=pl.BlockSpec((1, tm, d), lambda bi, i: (bi, i, 0)),
        out_shape=jax.ShapeDtypeStruct((b, l, d), F32),
        scratch_shapes=[pltpu.VMEM((tm, f), BF16)],
        compiler_params=_params(("parallel", "arbitrary")),
        name="ffn",
    )(x, x, x, mod4, mod4, mod4, g, w_up, w_up, cw, cb, w_down, fg)


def _trig_tables(n, row_mul, row_adds, col_mul, col_adds, period, scale=1.0):
    a_sz = 64 if n % 64 == 0 else 1
    n_stack = len(row_adds)
    q = (jnp.arange(n, dtype=jnp.int32) * col_mul)[None, :] + jnp.asarray(col_adds, jnp.int32)[:, None]
    ra = jnp.arange(n // a_sz, dtype=jnp.int32) * (a_sz * row_mul)
    rb = (jnp.arange(a_sz, dtype=jnp.int32) * row_mul)[None, :] + jnp.asarray(row_adds, jnp.int32)[:, None]
    w = 2.0 * math.pi / period
    ang_a = ((ra[None, :, None] * q[:, None, :]) % period).astype(F32) * w
    ang_b = ((rb[:, :, None] * q[:, None, :]) % period).astype(F32) * w
    ca, sa = jnp.cos(ang_a)[:, :, None, :], jnp.sin(ang_a)[:, :, None, :]
    cb, sb = jnp.cos(ang_b)[:, None, :, :], jnp.sin(ang_b)[:, None, :, :]
    both = jnp.stack([(ca * cb - sa * sb) * scale, (sa * cb + ca * sb) * scale], axis=1)
    return both.reshape(2 * n_stack, n // a_sz, a_sz, n)


def _pad2(a, rows, cols):
    return jnp.pad(a, ((0, rows - a.shape[0]), (0, cols - a.shape[1])))


def _hyena_features(l, pad_to):
    t = jnp.linspace(0.0, 1.0, l, dtype=F32)[:, None]
    bands = jnp.linspace(1e-4, HYENA_BANDS - 1, HYENA_BANDS, dtype=F32)[None, :]
    ang = (2.0 * math.pi / l) * jnp.arange(l, dtype=F32)[:, None] * bands
    feats = jnp.concatenate([t, jnp.cos(ang), -jnp.sin(ang)], axis=-1)
    return jnp.pad(feats, ((0, 0), (0, pad_to - feats.shape[1])))


def _bias_classes(rows, kh):
    r = np.arange(rows)
    r0 = np.clip(r - kh // 2, 0, rows - kh)
    off = r0 - r
    lo = kh // 2
    reps = list(range(lo)) + [lo] + list(range(rows - kh + lo + 1, rows))
    return [int(off[i]) for i in reps]


def _attention_bias(rpb, rows, kh):
    heads, nr, nc = rpb.shape
    w = GRID_W
    flat = rpb.astype(F32).reshape(heads * nr, nc)
    table = _bias_call(_pad2(flat, -(-heads * nr // 16) * 16, 128))
    table = table[:heads * nr].reshape(heads, nr, w, w)
    by_query = jnp.transpose(table, (0, 2, 1, 3)).reshape(heads * w, nr * w)
    firsts = [off + NA_WIN_H - 1 for off in _bias_classes(rows, kh)]
    return jnp.stack([by_query[:, first * w:(first + kh) * w] for first in firsts])


def _mixer_inputs(x, mod4, lw, tm):
    return _proj_call(x, mod4, lw['norm1_g'], lw['w_in'], lw['layer'], lw['cs'].shape[1], tm)


def _hyena(oh, lw, tabs, c, nb_fwd, nb, tm, rows):
    fwd, inv = tabs['hy_fwd'], tabs['hy_inv']
    uc = _dwconv_call(oh, lw['hyena_conv_w'], lw['hyena_conv_b'], c, rows)
    h, row0, asum = _filt_call(tabs['feats'], lw['fw1'], lw['fb1'], lw['fw2'], lw['fb2'], lw['fw3'],
                               lw['fb3'], lw['freq'], tabs['deltas'], min(rows * 2, oh.shape[1]))
    kk, km = _kspec_call(fwd, h, row0, asum, c, tm)
    spec, ym = _hfwd_call(fwd, uc, kk, km, 0, c, nb_fwd, tm)
    z2 = _hinv_call(inv, spec, ym, uc, 0, uc, 1, lw['skip'], 0, False, c, nb, tm)
    spec, ym = _hfwd_call(fwd, z2, kk, km, 1, c, nb_fwd, tm)
    return _hinv_call(inv, spec, ym, z2, 0, uc, 2, lw['skip'], 1, True, c, nb, tm)


def _stream_layer(x, mod4, lw, tabs, proj, y_att, cfg, final_norm):
    c = cfg['c']
    oa, oh, ob, of = proj
    y_pool = _pool_call(oa, tabs['pool_inv'], lw['pool_blk'], lw['pool_scale'], cfg['rows'])
    y_fnet = _fnet_call(tabs['fn'], of, lw['cs'], cfg['nb'], cfg['thy'])
    y_hyena = _hyena(oh, lw, tabs, c, cfg['nb'], cfg['nb'], cfg['thy'], cfg['rows'])
    x = _merge_call(x, mod4, lw['norm1_g'], (y_pool, y_fnet, y_hyena, y_att), lw['w_in'], lw['w_branch'],
                    lw['w_out'], lw['layer'], cfg['tm'])
    return _ffn_call(x, mod4, lw['norm2_g'], lw['w_up'], lw['ffn_conv_w'], lw['ffn_conv_b'],
                     lw['w_down'], lw['final_g'], lw['layer'], cfg['tm'], cfg['chunk'], final_norm)


def _seq_tables(l, c):
    q = l // RESIDUES
    res = list(range(RESIDUES))
    zero = [0] * RESIDUES
    hy_fwd = _trig_tables(q, 1, zero, RESIDUES, res, 2 * l).astype(BF16)
    hy_inv = _trig_tables(q, RESIDUES, res, 1, zero, 2 * l).astype(BF16)
    fn = _trig_tables(q, 1, zero, RESIDUES, res, l, 1.0 / math.sqrt(l * (c // FNET_GROUPS))).astype(BF16)
    deltas = jnp.linspace(math.log(HYENA_DECAY_TARGET) / HYENA_SLOW_DECAY,
                          math.log(HYENA_DECAY_TARGET) / HYENA_FAST_DECAY, c, dtype=F32)[None, :]
    feats = _hyena_features(l, 128)
    return {
        'hy_fwd': hy_fwd, 'hy_inv': hy_inv, 'fn': fn,
        'feats': jnp.concatenate([feats[r::RESIDUES] for r in range(RESIDUES)], axis=0), 'deltas': deltas,
        'pool_inv': _pool_inv_counts(l, c),
    }


def kernel(x, c, ctx, c_ctx, w_mod, b_mod, norm1_g, norm2_g, w_in, pool_w, pool_scale, hyena_conv_w,
           hyena_conv_b, hyena_filt_w1, hyena_filt_b1, hyena_filt_w2, hyena_filt_b2, hyena_filt_w3,
           hyena_filt_b3, hyena_freq, hyena_skip, na_rpb, w_branch, w_out, ffn_w_up, ffn_conv_w, ffn_conv_b,
           ffn_w_down, final_norm_g):
    batch, seq, d = x.shape
    lc = ctx.shape[1]
    depth = w_mod.shape[0]
    m = d // N_BRANCH
    rows = seq // GRID_W
    kh = min(NA_WIN_H, rows)

    cvec = jnp.concatenate([c, c_ctx[None], jnp.zeros((16 - batch - 1, d), F32)], axis=0)
    mod = _mod_call(cvec, w_mod, b_mod)
    mod_x = mod[:, :batch].reshape(depth, batch, 6, 1, d)
    mod_c = jnp.broadcast_to(mod[:, batch:batch + 1], (depth, batch, 6 * d)).reshape(depth, batch, 6, 1, d)

    gsz = m // FNET_GROUPS
    cc, ss = _trig_tables(gsz, 1, [0], 1, [0], gsz).reshape(2, gsz, gsz)
    eye = jnp.eye(FNET_GROUPS, dtype=F32)
    cs = jnp.concatenate([jnp.kron(eye, cc), -jnp.kron(eye, ss)], axis=0).astype(BF16)

    tabs_x = _seq_tables(seq, m)
    tabs_c = _seq_tables(lc, m)
    assert max(POOL_WINDOWS) // 2 <= PAD and batch % 2 == 0
    cfg_x = dict(c=m, tm=min(1024, seq), rows=256, nb=2, thy=min(512, seq // RESIDUES), chunk=256,
                 group=min(16, rows))
    cfg_c = dict(c=m, tm=lc, rows=lc, nb=2, thy=lc // RESIDUES, chunk=256)

    assert w_in.shape[2] == 3 * 2 * d and 8 * m == 2 * d
    w_in_b, w_branch_b, w_out_b = w_in.astype(BF16), w_branch.astype(BF16), w_out.astype(BF16)
    w_up_b, w_down_b = ffn_w_up.astype(BF16), ffn_w_down.astype(BF16)

    for l in range(depth):
        last = l == depth - 1
        lw = {
            'layer': l, 'w_in': w_in_b, 'w_branch': w_branch_b, 'w_out': w_out_b, 'w_up': w_up_b, 'w_down': w_down_b,
            'norm1_g': norm1_g[l][None], 'norm2_g': norm2_g[l][None], 'final_g': final_norm_g[None],
            'cs': cs,
            'pool_blk': jax.scipy.linalg.block_diag(*[pool_w[l, gi] for gi in range(pool_w.shape[1])]).astype(BF16),
            'pool_scale': pool_scale[l][None],
            'hyena_conv_w': hyena_conv_w[l], 'hyena_conv_b': hyena_conv_b[l][None],
            'fw1': _pad2(hyena_filt_w1[l], 128, 128), 'fb1': _pad2(hyena_filt_b1[l][None], 1, 128),
            'fw2': _pad2(hyena_filt_w2[l], 128, 128), 'fb2': _pad2(hyena_filt_b2[l][None], 1, 128),
            'fw3': _pad2(hyena_filt_w3[l], 128, hyena_filt_w3.shape[2]), 'fb3': hyena_filt_b3[l][None],
            'freq': _pad2(hyena_freq[l][None], 1, 128),
            'skip': hyena_skip[l][:, None, :],
            'ffn_conv_w': ffn_conv_w[l], 'ffn_conv_b': ffn_conv_b[l][None],
        }
        bias = _attention_bias(na_rpb[l], rows, kh)

        proj_c = _mixer_inputs(ctx, mod_c[l], lw, cfg_c['tm'])
        proj_x = _mixer_inputs(x, mod_x[l], lw, cfg_x['tm'])
        y_att = _natt_call(proj_x[2], proj_c[2], bias, m, cfg_x['group'])
        x = _stream_layer(x, mod_x[l], lw, tabs_x, proj_x, y_att, cfg_x, last)
        if not last:
            ctx = _stream_layer(ctx, mod_c[l], lw, tabs_c, proj_c, _catt_call(proj_c[2], m), cfg_c, False)
    return x
```

```python
import functools
import math

import numpy as np
import jax
import jax.numpy as jnp
from jax import lax
from jax.experimental import pallas as pl
from jax.experimental.pallas import tpu as pltpu

F32 = jnp.float32
BF16 = jnp.bfloat16

GRID_W = 64
N_BRANCH = 4
POOL_WINDOWS = (2, 4, 8, 16)
FNET_GROUPS = 4
HYENA_ORDER = 2
HYENA_BANDS = 16
HYENA_DECAY_TARGET = 1e-2
HYENA_FAST_DECAY = 0.3
HYENA_SLOW_DECAY = 1.5
HYENA_DECAY_SHIFT = 0.05
NA_HEAD_DIM = 64
NA_WIN_H = 8
NA_WIN_W = 16
EPS = 1e-6
NEG_INF = -1e30

RESIDUES = 4
SQRT_HALF = math.sqrt(0.5)
LANES = 128
HALO = 16
PAD = 8
VMEM_LIMIT = 56 * 1024 * 1024


def _params(sem):
    return pltpu.CompilerParams(dimension_semantics=sem, vmem_limit_bytes=VMEM_LIMIT)


def _dot(a, b):
    return jnp.dot(a, b, preferred_element_type=F32)


def _dot_t(a, b):
    return lax.dot_general(a, b, (((1,), (1,)), ((), ())), preferred_element_type=F32)


def _split_bf16(a):
    hi = a.astype(BF16)
    lo = (a - hi.astype(F32)).astype(BF16)
    return hi, lo


def _mod_kernel(c_ref, w_ref, b_ref, o_ref):
    a = c_ref[...]
    a = a * jax.nn.sigmoid(a)
    a_hi, a_lo = _split_bf16(a)
    w_hi, w_lo = _split_bf16(w_ref[0])
    o_ref[0] = _dot(a_hi, w_hi) + _dot(a_lo, w_hi) + _dot(a_hi, w_lo) + b_ref[0]


def _mod_call(cvec, w_mod, b_mod):
    depth, d, n = w_mod.shape
    rows = cvec.shape[0]
    tn = 1024
    return pl.pallas_call(
        _mod_kernel,
        grid=(depth, n // tn),
        in_specs=[
            pl.BlockSpec((rows, d), lambda l, j: (0, 0)),
            pl.BlockSpec((1, d, tn), lambda l, j: (l, 0, j)),
            pl.BlockSpec((1, 1, tn), lambda l, j: (l, 0, j)),
        ],
        out_specs=pl.BlockSpec((1, rows, tn), lambda l, j: (l, 0, j)),
        out_shape=jax.ShapeDtypeStruct((depth, rows, n), F32),
        compiler_params=_params(("arbitrary", "arbitrary")),
        name="mod",
    )(cvec, w_mod, b_mod.reshape(depth, 1, n))


def _ada_norm(x, g, shift, scale):
    y = x * lax.rsqrt(jnp.mean(x * x, axis=-1, keepdims=True) + EPS)
    return y * (g * (1.0 + scale)) + shift


def _deinterleave(y, scr_ref, ways):
    n = y.shape[0]
    parts = [[] for _ in range(ways)]
    for cb in range(y.shape[1] // LANES):
        scr_ref[cb] = y[:, cb * LANES:(cb + 1) * LANES]
        for k in range(ways):
            parts[k].append(scr_ref[cb, pl.ds(k, n // ways, stride=ways), :])
    return [jnp.concatenate(p, axis=1) for p in parts]


def _interleave(parts, scr_ref):
    ways = len(parts)
    n = parts[0].shape[0]
    cols = []
    for cb in range(parts[0].shape[1] // LANES):
        for k in range(ways):
            scr_ref[cb, pl.ds(k, n, stride=ways), :] = parts[k][:, cb * LANES:(cb + 1) * LANES]
        cols.append(scr_ref[cb])
    return jnp.concatenate(cols, axis=1)


def _proj_kernel(x_ref, sh_ref, sc_ref, g_ref, w_ref, oa_ref, oh_ref, ob_ref, of_ref, scr_ref):
    m = oa_ref.shape[2]
    h = _ada_norm(x_ref[0], g_ref[...], sh_ref[0, 0], sc_ref[0, 0]).astype(BF16)
    p = _dot(h, w_ref[...])
    oa_ref[0] = p[:, :m]
    oh_ref[0] = p[:, 2 * m:5 * m].astype(BF16)
    ob_ref[0] = p[:, 5 * m:8 * m].astype(BF16)
    for k, part in enumerate(_deinterleave(p[:, m:2 * m], scr_ref, RESIDUES)):
        of_ref[0, k] = part.astype(BF16)


def _proj_call(x, mod4, g, w_in, layer, m, tm):
    b, l, d = x.shape
    const = lambda bi, i: (0, 0)
    return pl.pallas_call(
        _proj_kernel,
        grid=(b, l // tm),
        in_specs=[
            pl.BlockSpec((1, tm, d), lambda bi, i: (bi, i, 0)),
            pl.BlockSpec((1, 1, 1, d), lambda bi, i: (bi, 0, 0, 0)),
            pl.BlockSpec((1, 1, 1, d), lambda bi, i: (bi, 1, 0, 0)),
            pl.BlockSpec((1, d), const),
            pl.BlockSpec((None, d, 8 * m), lambda bi, i: (layer, 0, 0)),
        ],
        out_specs=[
            pl.BlockSpec((1, tm, m), lambda bi, i: (bi, i, 0)),
            pl.BlockSpec((1, tm, 3 * m), lambda bi, i: (bi, i, 0)),
            pl.BlockSpec((1, tm, 3 * m), lambda bi, i: (bi, i, 0)),
            pl.BlockSpec((1, RESIDUES, tm // RESIDUES, m), lambda bi, i: (bi, 0, i, 0)),
        ],
        out_shape=[
            jax.ShapeDtypeStruct((b, l, m), F32),
            jax.ShapeDtypeStruct((b, l, 3 * m), BF16),
            jax.ShapeDtypeStruct((b, l, 3 * m), BF16),
            jax.ShapeDtypeStruct((b, RESIDUES, l // RESIDUES, m), BF16),
        ],
        scratch_shapes=[pltpu.VMEM((m // LANES, tm, LANES), F32)],
        compiler_params=_params(("parallel", "parallel")),
        name="proj",
    )(x, mod4, mod4, g, w_in)


def _fill_padded(src_ref, pad_ref, l, rows):
    c = pad_ref.shape[1]
    pad_ref[0:PAD, :] = jnp.zeros((PAD, c), F32)
    pad_ref[l + PAD:l + 2 * PAD, :] = jnp.zeros((PAD, c), F32)

    def copy(i, carry):
        r = pl.multiple_of(i * rows, rows)
        pad_ref[pl.ds(r + PAD, rows), :] = src_ref[0, pl.ds(r, rows), :].astype(F32)
        return carry

    lax.fori_loop(0, l // rows, copy, 0)


def _centred_window_sums(u, windows):
    n = u.shape[0]
    fwd = {1: u}
    w = 1
    while w < max(windows):
        fwd[2 * w] = fwd[w] + pltpu.roll(fwd[w], n - w, 0)
        w *= 2
    return {w: pltpu.roll(fwd[w], w // 2, 0) for w in windows}


def _pool_kernel(u_ref, inv_ref, w_ref, s_ref, o_ref, pad_ref, *, rows):
    l, c = u_ref.shape[1], u_ref.shape[2]
    gw = c // len(POOL_WINDOWS)
    per_block = LANES // gw
    _fill_padded(u_ref, pad_ref, l, rows)
    n = rows + 2 * PAD
    lane = lax.broadcasted_iota(jnp.int32, (n, LANES), 1)
    w = w_ref[...]
    scale = s_ref[...]

    def body(i, carry):
        r = pl.multiple_of(i * rows, rows)
        cols = []
        for cb in range(c // LANES):
            lanes = slice(cb * LANES, (cb + 1) * LANES)
            win = pad_ref[pl.ds(r, n), lanes]
            wins = POOL_WINDOWS[cb * per_block:(cb + 1) * per_block]
            sums = _centred_window_sums(win, wins)
            s = sums[wins[-1]]
            for k in range(per_block - 2, -1, -1):
                s = jnp.where(lane < (k + 1) * gw, sums[wins[k]], s)
            cols.append((s * inv_ref[pl.ds(r, n), lanes] - win)[PAD:PAD + rows])
        y = jnp.concatenate(cols, axis=1)
        o_ref[0, pl.ds(r, rows), :] = (_dot(y.astype(BF16), w) * scale).astype(BF16)
        return carry

    lax.fori_loop(0, l // rows, body, 0)


def _pool_inv_counts(l, c):
    gw = c // len(POOL_WINDOWS)
    t = jnp.arange(-PAD, l + PAD, dtype=jnp.int32)[:, None]
    half = jnp.asarray(np.repeat(np.array(POOL_WINDOWS) // 2, gw), jnp.int32)[None, :]
    cnt = jnp.minimum(t + half, l) - jnp.maximum(t - half, 0)
    return 1.0 / jnp.maximum(cnt, 1).astype(F32)


def _pool_call(oa, inv_cnt, w_blk, scale, rows):
    b, l, _ = oa.shape
    c = w_blk.shape[0]
    return pl.pallas_call(
        functools.partial(_pool_kernel, rows=rows),
        grid=(b,),
        in_specs=[
            pl.BlockSpec((1, l, c), lambda bi: (bi, 0, 0)),
            pl.BlockSpec((l + 2 * PAD, c), lambda bi: (0, 0)),
            pl.BlockSpec((c, c), lambda bi: (0, 0)),
            pl.BlockSpec((1, c), lambda bi: (0, 0)),
        ],
        out_specs=pl.BlockSpec((1, l, c), lambda bi: (bi, 0, 0)),
        out_shape=jax.ShapeDtypeStruct((b, l, c), BF16),
        scratch_shapes=[pltpu.VMEM((l + 2 * PAD, c), F32)],
        compiler_params=_params(("parallel",)),
        name="pool",
    )(oa, inv_cnt, w_blk, scale)


def _dwconv_kernel(u_ref, w_ref, b_ref, o_ref, pad_ref, *, rows):
    l, c = u_ref.shape[1], u_ref.shape[2]
    nlb = c // LANES
    sub = rows // RESIDUES
    for cb in range(nlb):
        pad_ref[cb, 0:PAD, :] = jnp.zeros((PAD, LANES), F32)
        pad_ref[cb, l + PAD:l + 2 * PAD, :] = jnp.zeros((PAD, LANES), F32)

    def copy(i, carry):
        r = pl.multiple_of(i * rows, rows)
        blk = u_ref[0, pl.ds(r, rows), :].astype(F32)
        for cb in range(nlb):
            pad_ref[cb, pl.ds(r + PAD, rows), :] = blk[:, cb * LANES:(cb + 1) * LANES]
        return carry

    lax.fori_loop(0, l // rows, copy, 0)
    w = w_ref[...]
    bias = b_ref[...]

    def body(i, carry):
        r = i * rows
        rs = pl.multiple_of(i * sub, sub)
        for cb in range(nlb):
            lanes = slice(cb * LANES, (cb + 1) * LANES)
            taps = [pad_ref[cb, pl.ds(r + PAD - 1 + k, sub, stride=RESIDUES), :] for k in range(RESIDUES + 2)]
            for k in range(RESIDUES):
                y = taps[k] * w[0:1, lanes] + taps[k + 1] * w[1:2, lanes] + taps[k + 2] * w[2:3, lanes] + bias[:, lanes]
                o_ref[0, k, pl.ds(rs, sub), lanes] = y.astype(BF16)
        return carry

    lax.fori_loop(0, l // rows, body, 0)


def _dwconv_call(u, w, bias, c, rows):
    b, l, _ = u.shape
    nblk = w.shape[1] // c
    return pl.pallas_call(
        functools.partial(_dwconv_kernel, rows=rows),
        grid=(b, nblk),
        in_specs=[
            pl.BlockSpec((1, l, c), lambda bi, j: (bi, 0, j)),
            pl.BlockSpec((w.shape[0], c), lambda bi, j: (0, j)),
            pl.BlockSpec((1, c), lambda bi, j: (0, j)),
        ],
        out_specs=pl.BlockSpec((1, RESIDUES, l // RESIDUES, c), lambda bi, j: (bi, 0, 0, j)),
        out_shape=jax.ShapeDtypeStruct((b, RESIDUES, l // RESIDUES, nblk * c), BF16),
        scratch_shapes=[pltpu.VMEM((c // LANES, l + 2 * PAD, LANES), F32)],
        compiler_params=_params(("parallel", "parallel")),
        name="dwconv",
    )(u, w, bias)


def _filt_kernel(f_ref, w1_ref, b1_ref, w2_ref, b2_ref, w3_ref, b3_ref, fr_ref, dl_ref,
                 hb_ref, row0_ref, asum_ref):
    hp = lax.Precision.HIGHEST
    feats = f_ref[...]
    freq = fr_ref[...]
    h = jnp.sin(freq * (jnp.dot(feats, w1_ref[...], precision=hp, preferred_element_type=F32) + b1_ref[...]))
    h = jnp.sin(freq * (jnp.dot(h, w2_ref[...], precision=hp, preferred_element_type=F32) + b2_ref[...]))
    h = jnp.dot(h, w3_ref[...], precision=hp, preferred_element_type=F32) + b3_ref[...]
    t = feats[:, 0:1]
    win = jnp.exp(-t * jnp.abs(dl_ref[...])) + HYENA_DECAY_SHIFT
    h = h * jnp.concatenate([win] * (h.shape[1] // win.shape[1]), axis=1)
    hb = h.astype(BF16)
    hb_ref[...] = hb
    part = jnp.sum(jnp.abs(h), axis=0, keepdims=True)

    @pl.when(pl.program_id(0) == 0)
    def _():
        asum_ref[...] = jnp.zeros_like(asum_ref)
        row0_ref[...] = hb[0:16].astype(F32)[0:8]

    asum_ref[...] += part


def _filt_call(feats, w1, b1, w2, b2, w3, b3, freq, deltas, rows):
    l, fd = feats.shape
    hd = w2.shape[0]
    n = w3.shape[1]
    c = deltas.shape[1]
    const = lambda i: (0, 0)
    return pl.pallas_call(
        _filt_kernel,
        grid=(l // rows,),
        in_specs=[
            pl.BlockSpec((rows, fd), lambda i: (i, 0)),
            pl.BlockSpec((fd, hd), const), pl.BlockSpec((1, hd), const),
            pl.BlockSpec((hd, hd), const), pl.BlockSpec((1, hd), const),
            pl.BlockSpec((hd, n), const), pl.BlockSpec((1, n), const),
            pl.BlockSpec((1, hd), const), pl.BlockSpec((1, c), const),
        ],
        out_specs=[
            pl.BlockSpec((rows, n), lambda i: (i, 0)),
            pl.BlockSpec((8, n), const),
            pl.BlockSpec((1, n), const),
        ],
        out_shape=[
            jax.ShapeDtypeStruct((l, n), BF16),
            jax.ShapeDtypeStruct((8, n), F32),
            jax.ShapeDtypeStruct((1, n), F32),
        ],
        compiler_params=_params(("arbitrary",)),
        name="hyena_filt",
    )(feats, w1, b1, w2, b2, w3, b3, freq, deltas)


def _alt_signs(n):
    t = lax.broadcasted_iota(jnp.int32, (16, n), 1)
    return jnp.where((t & 1) == 0, 1.0, -1.0).astype(BF16)


def _mat_specs(tm, tables):
    n, _, a, cols = tables.shape
    return [pl.BlockSpec((n, tm // a, a, cols), lambda g, i: (0, i, 0, 0))]


def _mat(ref, k):
    return ref[k].reshape(ref.shape[1] * ref.shape[2], ref.shape[3])


def _quad_transform(tab_ref, parts):
    a = [_dot(_mat(tab_ref, 2 * r), parts[r]) for r in range(RESIDUES)]
    b = [_dot(_mat(tab_ref, 2 * r + 1), parts[r]) for r in range(RESIDUES)]
    ea, fa, ga, ha = a[0] + a[2], a[0] - a[2], a[1] + a[3], a[1] - a[3]
    eb, fb, gb, hb = b[0] + b[2], b[0] - b[2], b[1] + b[3], b[1] - b[3]
    zr = (ea + ga, fa + hb, fa - hb, ea - ga)
    zs = (eb + gb, ha - fb, ha + fb, gb - eb)
    return zr, zs


def _quad_special(alt):
    c1, c3 = SQRT_HALF * alt[1], SQRT_HALF * alt[3]
    return (alt[0] + c1 - c3, c1 + alt[2] + c3), (alt[0] - c1 + c3, c1 - alt[2] + c3)


def _kspec_kernel(*refs):
    mat_refs, (h_ref, row0_ref, asum_ref, kk_ref, km_ref) = refs[0], refs[1:]
    i = pl.program_id(1)
    tm, q = kk_ref.shape[2], mat_refs.shape[3]
    c = kk_ref.shape[3]
    l = RESIDUES * q
    parts = [h_ref[r * q:(r + 1) * q, :] for r in range(RESIDUES)]
    zr, zs = _quad_transform(mat_refs, parts)
    asum = asum_ref[...]
    inv = 1.0 / (asum[:, :c] + asum[:, c:] + EPS)
    hb0 = row0_ref[0:1, c:]
    row = lax.broadcasted_iota(jnp.int32, (tm, c), 0) + i * tm
    wj = jnp.where(row == 0, 0.5 / l, 1.0 / l) * inv
    for f in range(4):
        kk_ref[0, 2 * f] = (zr[f][:, :c] + zr[f][:, c:] - hb0) * wj
        kk_ref[0, 2 * f + 1] = (zs[f][:, :c] - zs[f][:, c:]) * wj

    @pl.when(i == 0)
    def _():
        sg = _alt_signs(q)
        alt = [_dot(sg, p)[0:8] for p in parts]
        for f, (xr, xs) in enumerate(_quad_special(alt)):
            km_ref[0, :, 2 * f * c:(2 * f + 1) * c] = (xr[:, :c] + xr[:, c:] - hb0) * inv * (1.0 / l)
            km_ref[0, :, (2 * f + 1) * c:(2 * f + 2) * c] = (xs[:, :c] - xs[:, c:]) * inv * (1.0 / l)


def _kspec_call(mats, h, row0, asum, c, tm):
    q = mats.shape[3]
    l = RESIDUES * q
    orders = h.shape[1] // (2 * c)
    return pl.pallas_call(
        _kspec_kernel,
        grid=(orders, q // tm),
        in_specs=_mat_specs(tm, mats) + [
            pl.BlockSpec((l, 2 * c), lambda o, i: (0, o)),
            pl.BlockSpec((8, 2 * c), lambda o, i: (0, o)),
            pl.BlockSpec((1, 2 * c), lambda o, i: (0, o)),
        ],
        out_specs=[
            pl.BlockSpec((1, 8, tm, c), lambda o, i: (o, 0, i, 0)),
            pl.BlockSpec((1, 8, 4 * c), lambda o, i: (o, 0, 0)),
        ],
        out_shape=[
            jax.ShapeDtypeStruct((orders, 8, q, c), F32),
            jax.ShapeDtypeStruct((orders, 8, 4 * c), F32),
        ],
        compiler_params=_params(("arbitrary", "arbitrary")),
        name="hyena_kspec",
    )(mats, h, row0, asum)


def _cmul(zr, zs, kr, ks):
    return zr * kr - zs * ks, zr * ks + zs * kr


def _hfwd_kernel(*refs):
    mat_refs, (z_ref, kk_ref, km_ref, spec_ref, ym_ref) = refs[0], refs[1:]
    nb = z_ref.shape[0]
    c = z_ref.shape[3]
    for bb in range(nb):
        zr, zs = _quad_transform(mat_refs, [z_ref[bb, r] for r in range(RESIDUES)])
        yr, ys = zip(*[_cmul(zr[f], zs[f], kk_ref[0, 2 * f], kk_ref[0, 2 * f + 1]) for f in range(4)])
        a, b, u1, u2 = yr[0] + yr[3], yr[0] - yr[3], yr[1] + yr[2], ys[1] + ys[2]
        cc, dd, d1, d2 = ys[0] - ys[3], ys[0] + ys[3], yr[1] - yr[2], ys[2] - ys[1]
        for k, v in enumerate((a + u1, cc + d2, b + u2, dd + d1, a - u1, cc - d2, b - u2, dd - d1)):
            spec_ref[bb, k] = v.astype(BF16)

    @pl.when(pl.program_id(1) == 0)
    def _():
        sg = _alt_signs(z_ref.shape[2])
        for bb in range(nb):
            alt = [_dot(sg, z_ref[bb, r])[0:8] for r in range(RESIDUES)]
            (zq_r, zq_s), (z3_r, z3_s) = _quad_special(alt)
            yq_r, yq_s = _cmul(zq_r, zq_s, km_ref[0, :, 0:c], km_ref[0, :, c:2 * c])
            y3_r, y3_s = _cmul(z3_r, z3_s, km_ref[0, :, 2 * c:3 * c], km_ref[0, :, 3 * c:4 * c])
            ym_ref[bb, :, 0:c] = yq_r + y3_r
            ym_ref[bb, :, c:2 * c] = SQRT_HALF * (yq_r + yq_s - y3_r + y3_s)
            ym_ref[bb, :, 2 * c:3 * c] = yq_s - y3_s
            ym_ref[bb, :, 3 * c:4 * c] = SQRT_HALF * (yq_s - yq_r + y3_r + y3_s)


def _hfwd_call(mats, zb, kk, km, order, c, nb, tm):
    b, _, q, _ = zb.shape
    return pl.pallas_call(
        _hfwd_kernel,
        grid=(b // nb, q // tm),
        in_specs=_mat_specs(tm, mats) + [
            pl.BlockSpec((nb, RESIDUES, q, c), lambda g, i: (g, 0, 0, 0)),
            pl.BlockSpec((1, 8, tm, c), lambda g, i: (order, 0, i, 0)),
            pl.BlockSpec((1, 8, 4 * c), lambda g, i: (order, 0, 0)),
        ],
        out_specs=[
            pl.BlockSpec((nb, 8, tm, c), lambda g, i: (g, 0, i, 0)),
            pl.BlockSpec((nb, 8, 4 * c), lambda g, i: (g, 0, 0)),
        ],
        out_shape=[
            jax.ShapeDtypeStruct((b, 8, q, c), BF16),
            jax.ShapeDtypeStruct((b, 8, 4 * c), F32),
        ],
        compiler_params=_params(("arbitrary", "arbitrary")),
        name="hyena_fwd",
    )(mats, zb, kk, km)


def _hinv_kernel(*refs, natural_out):
    mat_refs, (spec_ref, ym_ref, z_ref, gate_ref, skip_ref), out_refs = refs[0], refs[1:6], refs[6:]
    nb, _, tm, c = z_ref.shape
    row = lax.broadcasted_iota(jnp.int32, (tm, c), 0)
    even_row = (row & 1) == 0
    skip = skip_ref[0]
    for bb in range(nb):
        outs = []
        for r in range(RESIDUES):
            m = ym_ref[bb, 0:1, r * c:(r + 1) * c]
            y = (_dot(_mat(mat_refs, 2 * r), spec_ref[bb, 2 * r]) + _dot(_mat(mat_refs, 2 * r + 1), spec_ref[bb, 2 * r + 1])
                 + jnp.where(even_row, m, -m))
            outs.append(gate_ref[bb, r].astype(F32) * (y + skip * z_ref[bb, r].astype(F32)))
        if natural_out:
            o_ref, scr_ref = out_refs
            o_ref[bb] = _interleave(outs, scr_ref).astype(BF16)
        else:
            (o_ref,) = out_refs
            for r in range(RESIDUES):
                o_ref[bb, r] = outs[r].astype(BF16)


def _hinv_call(mats, spec, ym, zf, zcol, uc, gcol, skip, order, natural_out, c, nb, tm):
    b, _, q, _ = spec.shape
    if natural_out:
        out_specs = pl.BlockSpec((nb, RESIDUES * tm, c), lambda g, i: (g, i, 0))
        out_shape = jax.ShapeDtypeStruct((b, RESIDUES * q, c), BF16)
        scratch = [pltpu.VMEM((c // LANES, RESIDUES * tm, LANES), F32)]
    else:
        out_specs = pl.BlockSpec((nb, RESIDUES, tm, c), lambda g, i: (g, 0, i, 0))
        out_shape = jax.ShapeDtypeStruct((b, RESIDUES, q, c), BF16)
        scratch = []
    return pl.pallas_call(
        functools.partial(_hinv_kernel, natural_out=natural_out),
        grid=(b // nb, q // tm),
        in_specs=_mat_specs(tm, mats) + [
            pl.BlockSpec((nb, 8, q, c), lambda g, i: (g, 0, 0, 0)),
            pl.BlockSpec((nb, 8, 4 * c), lambda g, i: (g, 0, 0)),
            pl.BlockSpec((nb, RESIDUES, tm, c), lambda g, i: (g, 0, i, zcol)),
            pl.BlockSpec((nb, RESIDUES, tm, c), lambda g, i: (g, 0, i, gcol)),
            pl.BlockSpec((1, 1, c), lambda g, i: (order, 0, 0)),
        ],
        out_specs=out_specs,
        out_shape=out_shape,
        scratch_shapes=scratch,
        compiler_params=_params(("arbitrary", "arbitrary")),
        name="hyena_inv",
    )(mats, spec, ym, zf, uc, skip)


def _fnet_kernel(*refs):
    mat_refs, (u_ref, cs_ref, o_ref) = refs[0], refs[1:]
    nb = u_ref.shape[0]
    cs = cs_ref[...]
    for bb in range(nb):
        parts = [u_ref[bb, r] for r in range(RESIDUES)]
        a = [_dot(_mat(mat_refs, 2 * r), parts[r]) for r in range(RESIDUES)]
        b = [_dot(_mat(mat_refs, 2 * r + 1), parts[r]) for r in range(RESIDUES)]
        ea, fa, ga, ha = a[0] + a[2], a[0] - a[2], a[1] + a[3], a[1] - a[3]
        eb, fb, gb, hb = b[0] + b[2], b[0] - b[2], b[1] + b[3], b[1] - b[3]
        uc = (ea + ga, fa - hb, ea - ga, fa + hb)
        us = (eb + gb, fb + ha, eb - gb, fb - ha)
        for m in range(RESIDUES):
            both = jnp.concatenate([uc[m].astype(BF16), us[m].astype(BF16)], axis=1)
            o_ref[bb, m] = _dot(both, cs).astype(BF16)


def _fnet_call(mats, u4, cs, nb, tm):
    b, _, q, c = u4.shape
    out = pl.pallas_call(
        _fnet_kernel,
        grid=(b // nb, q // tm),
        in_specs=_mat_specs(tm, mats) + [
            pl.BlockSpec((nb, RESIDUES, q, c), lambda g, i: (g, 0, 0, 0)),
            pl.BlockSpec(cs.shape, lambda g, i: (0, 0)),
        ],
        out_specs=pl.BlockSpec((nb, RESIDUES, tm, c), lambda g, i: (g, 0, i, 0)),
        out_shape=jax.ShapeDtypeStruct((b, RESIDUES, q, c), BF16),
        compiler_params=_params(("arbitrary", "arbitrary")),
        name="fnet",
    )(mats, u4, cs)
    return out.reshape(b, RESIDUES * q, c)


def _stack_heads(q, heads):
    lane = lax.broadcasted_iota(jnp.int32, q.shape, 1)
    zero = jnp.zeros_like(q)
    return jnp.concatenate([jnp.where(lax.shift_right_logical(lane, int(math.log2(NA_HEAD_DIM))) == h, q, zero) for h in range(heads)], axis=0)


def _merge_heads(o, heads, n):
    lane = lax.broadcasted_iota(jnp.int32, (n, o.shape[1]), 1)
    out = jnp.zeros((n, o.shape[1]), F32)
    for h in range(heads):
        out = out + jnp.where(lax.shift_right_logical(lane, int(math.log2(NA_HEAD_DIM))) == h, o[h * n:(h + 1) * n], 0.0)
    return out


def _natt_kernel(q_ref, k_ref, v_ref, kc_ref, vc_ref, bias_ref, o_ref, *, rows, kh, group):
    w = GRID_W
    heads = q_ref.shape[2] // NA_HEAD_DIM
    base = pl.program_id(1) * group
    lo = kh // 2
    kc, vc = kc_ref[0], vc_ref[0]
    scale = jnp.asarray(NA_HEAD_DIM ** -0.5, BF16)

    def body(j, carry):
        r = base + j
        r0 = jnp.clip(r - lo, 0, rows - kh)
        start = pl.multiple_of(r0 * w, w)
        cls = jnp.minimum(r, lo) + jnp.maximum(r - (rows - kh + lo), 0)
        q4 = _stack_heads(q_ref[0, pl.ds(pl.multiple_of(j * w, w), w), :] * scale, heads)
        ks = k_ref[0, pl.ds(start, kh * w), :]
        vs = v_ref[0, pl.ds(start, kh * w), :]
        s_nb = _dot_t(q4, ks) + bias_ref[cls]
        s_cx = _dot_t(q4, kc)
        m = jnp.maximum(jnp.max(s_nb, axis=-1, keepdims=True), jnp.max(s_cx, axis=-1, keepdims=True))
        p_nb = jnp.exp(s_nb - m)
        p_cx = jnp.exp(s_cx - m)
        den = jnp.sum(p_nb, axis=-1, keepdims=True) + jnp.sum(p_cx, axis=-1, keepdims=True)
        o4 = (_dot(p_nb.astype(BF16), vs) + _dot(p_cx.astype(BF16), vc)) / den
        o_ref[0, pl.ds(pl.multiple_of(j * w, w), w), :] = _merge_heads(o4, heads, w).astype(BF16)
        return carry

    lax.fori_loop(0, group, body, 0, unroll=4)


def _natt_call(ob, obc, bias, c, group):
    b, l, _ = ob.shape
    lc = obc.shape[1]
    w = GRID_W
    rows = l // w
    kh = min(NA_WIN_H, rows)
    return pl.pallas_call(
        functools.partial(_natt_kernel, rows=rows, kh=kh, group=group),
        grid=(b, rows // group),
        in_specs=[
            pl.BlockSpec((1, group * w, c), lambda bi, r: (bi, r, 0)),
            pl.BlockSpec((1, l, c), lambda bi, r: (bi, 0, 1)),
            pl.BlockSpec((1, l, c), lambda bi, r: (bi, 0, 2)),
            pl.BlockSpec((1, lc, c), lambda bi, r: (bi, 0, 1)),
            pl.BlockSpec((1, lc, c), lambda bi, r: (bi, 0, 2)),
            pl.BlockSpec(bias.shape, lambda bi, r: (0, 0, 0)),
        ],
        out_specs=pl.BlockSpec((1, group * w, c), lambda bi, r: (bi, r, 0)),
        out_shape=jax.ShapeDtypeStruct((b, l, c), BF16),
        compiler_params=_params(("parallel", "arbitrary")),
        name="nb_attn",
    )(ob, ob, ob, obc, obc, bias)


def _bias_kernel(r_ref, o_ref):
    w = GRID_W
    shift = int(math.log2(w))
    n = o_ref.shape[1]
    r = r_ref[...]
    hi = r.astype(BF16)
    mid = (r - hi.astype(F32)).astype(BF16)
    lo = (r - hi.astype(F32) - mid.astype(F32)).astype(BF16)
    dc = lax.broadcasted_iota(jnp.int32, (r.shape[1], n), 0)
    p = lax.broadcasted_iota(jnp.int32, (r.shape[1], n), 1)
    idx = jnp.clip((p & (w - 1)) - lax.shift_right_logical(p, shift), 1 - NA_WIN_W, NA_WIN_W - 1) + (NA_WIN_W - 1)
    onehot = jnp.where(dc == idx, 1.0, 0.0).astype(BF16)
    val = _dot(hi, onehot) + _dot(mid, onehot) + _dot(lo, onehot)
    po = lax.broadcasted_iota(jnp.int32, val.shape, 1)
    cq = lax.shift_right_logical(po, shift)
    ck = po & (w - 1)
    c0 = jnp.clip(cq - NA_WIN_W // 2, 0, w - NA_WIN_W)
    inside = jnp.logical_and(ck >= c0, ck < c0 + NA_WIN_W)
    o_ref[...] = jnp.where(inside, val, NEG_INF)


def _bias_call(rpb2d):
    rows, cols = rpb2d.shape
    n = GRID_W * GRID_W
    return pl.pallas_call(
        _bias_kernel,
        grid=(1,),
        in_specs=[pl.BlockSpec((rows, cols), lambda i: (0, 0))],
        out_specs=pl.BlockSpec((rows, n), lambda i: (0, 0)),
        out_shape=jax.ShapeDtypeStruct((rows, n), F32),
        compiler_params=_params(("arbitrary",)),
        name="rpb_table",
    )(rpb2d)


def _catt_kernel(q_ref, k_ref, v_ref, o_ref):
    n = q_ref.shape[1]
    heads = q_ref.shape[2] // NA_HEAD_DIM
    q = q_ref[0] * jnp.asarray(NA_HEAD_DIM ** -0.5, BF16)
    q4 = _stack_heads(q, heads)
    s = _dot_t(q4, k_ref[0])
    p = jnp.exp(s - jnp.max(s, axis=-1, keepdims=True))
    den = jnp.sum(p, axis=-1, keepdims=True)
    o4 = _dot(p.astype(BF16), v_ref[0]) / den
    o_ref[0] = _merge_heads(o4, heads, n).astype(BF16)


def _catt_call(obc, c):
    b, lc, _ = obc.shape
    return pl.pallas_call(
        _catt_kernel,
        grid=(b,),
        in_specs=[pl.BlockSpec((1, lc, c), lambda bi, j=j: (bi, 0, j)) for j in range(3)],
        out_specs=pl.BlockSpec((1, lc, c), lambda bi: (bi, 0, 0)),
        out_shape=jax.ShapeDtypeStruct((b, lc, c), BF16),
        compiler_params=_params(("parallel",)),
        name="ctx_attn",
    )(obc, obc, obc)


def _merge_kernel(x_ref, sh_ref, sc_ref, gt_ref, g_ref, y0_ref, y1_ref, y2_ref, y3_ref,
                  wg0_ref, wg1_ref, wb_ref, wo_ref, o_ref):
    d = x_ref.shape[2]
    x = x_ref[0]
    h = _ada_norm(x, g_ref[...], sh_ref[0, 0], sc_ref[0, 0]).astype(BF16)
    merged = None
    for bi, y_ref in enumerate((y0_ref, y1_ref, y2_ref, y3_ref)):
        wg_ref = (wg0_ref, wg1_ref)[bi // 2]
        gate = jax.nn.sigmoid(_dot(h, wg_ref[:, (bi % 2) * d:(bi % 2 + 1) * d]))
        term = gate * _dot(y_ref[0].astype(BF16), wb_ref[bi])
        merged = term if merged is None else merged + term
    o_ref[0] = x + gt_ref[0, 0] * _dot(merged.astype(BF16), wo_ref[...])


def _merge_call(x, mod4, g, ys, w_in, wb, wo, layer, tm):
    b, l, d = x.shape
    c = ys[0].shape[2]
    gate_blk = w_in.shape[2] // (2 * d) - 2
    assert (gate_blk + 2) * 2 * d == w_in.shape[2]
    single = dict(pipeline_mode=pl.Buffered(1))
    mod_spec = lambda k: pl.BlockSpec((1, 1, 1, d), lambda bi, i: (bi, k, 0, 0))
    return pl.pallas_call(
        _merge_kernel,
        grid=(b, l // tm),
        in_specs=[
            pl.BlockSpec((1, tm, d), lambda bi, i: (bi, i, 0)),
            mod_spec(0), mod_spec(1), mod_spec(2),
            pl.BlockSpec((1, d), lambda bi, i: (0, 0)),
        ] + [pl.BlockSpec((1, tm, c), lambda bi, i: (bi, i, 0)) for _ in ys] + [
            pl.BlockSpec((None, d, 2 * d), lambda bi, i: (layer, 0, gate_blk), **single),
            pl.BlockSpec((None, d, 2 * d), lambda bi, i: (layer, 0, gate_blk + 1), **single),
            pl.BlockSpec((None,) + wb.shape[1:], lambda bi, i: (layer, 0, 0, 0), **single),
            pl.BlockSpec((None, d, d), lambda bi, i: (layer, 0, 0), **single),
        ],
        out_specs=pl.BlockSpec((1, tm, d), lambda bi, i: (bi, i, 0)),
        out_shape=jax.ShapeDtypeStruct((b, l, d), F32),
        compiler_params=_params(("parallel", "parallel")),
        name="merge",
    )(x, mod4, mod4, mod4, g, *ys, w_in, w_in, wb, wo)


def _ffn_kernel(x_ref, xp_ref, xn_ref, sh_ref, sc_ref, gt_ref, g_ref, wu_ref, wgt_ref, cw_ref, cb_ref,
                wd_ref, fg_ref, o_ref, a_ref, *, chunk, final_norm):
    i = pl.program_id(1)
    last = pl.num_programs(1) - 1
    tm = x_ref.shape[1]
    f = wu_ref.shape[1]
    n = tm + 2 * HALO
    g, sh, sc = g_ref[...], sh_ref[0, 0], sc_ref[0, 0]
    x = x_ref[0]
    h = _ada_norm(x, g, sh, sc).astype(BF16)
    hp = _ada_norm(xp_ref[0], g, sh, sc).astype(BF16)
    hn = _ada_norm(xn_ref[0], g, sh, sc).astype(BF16)
    hext = jnp.concatenate([hp, h, hn], axis=0)
    row = lax.broadcasted_iota(jnp.int32, (n, chunk), 0)
    valid = jnp.logical_and(jnp.logical_or(row >= HALO, i > 0), jnp.logical_or(row < HALO + tm, i < last))
    for j in range(f // chunk):
        cols = slice(j * chunk, (j + 1) * chunk)
        gp = jnp.where(valid, _dot(hext, wgt_ref[:, cols]), 0.0)
        cw = cw_ref[:, cols]
        gc = (pltpu.roll(gp, 1, 0) * cw[0:1] + gp * cw[1:2] + pltpu.roll(gp, n - 1, 0) * cw[2:3]
              + cb_ref[:, cols])[HALO:HALO + tm]
        u = _dot(h, wu_ref[:, cols])
        a_ref[:, cols] = (gc * jax.nn.sigmoid(gc) * u).astype(BF16)
    y = x + gt_ref[0, 0] * _dot(a_ref[...], wd_ref[...])
    if final_norm:
        y = y * lax.rsqrt(jnp.mean(y * y, axis=-1, keepdims=True) + EPS) * fg_ref[...]
    o_ref[0] = y


def _ffn_call(x, mod4, g, w_up, cw, cb, w_down, fg, layer, tm, chunk, final_norm):
    b, l, d = x.shape
    f = w_down.shape[1]
    hb = tm // HALO
    nblk = l // HALO
    const2 = lambda bi, i: (0, 0)
    mod_spec = lambda k: pl.BlockSpec((1, 1, 1, d), lambda bi, i: (bi, k, 0, 0))
    single = dict(pipeline_mode=pl.Buffered(1))
    return pl.pallas_call(
        functools.partial(_ffn_kernel, chunk=chunk, final_norm=final_norm),
        grid=(b, l // tm),
        in_specs=[
            pl.BlockSpec((1, tm, d), lambda bi, i: (bi, i, 0)),
            pl.BlockSpec((1, HALO, d), lambda bi, i: (bi, jnp.maximum(i * hb - 1, 0), 0)),
            pl.BlockSpec((1, HALO, d), lambda bi, i: (bi, jnp.minimum((i + 1) * hb, nblk - 1), 0)),
            mod_spec(3), mod_spec(4), mod_spec(5),
            pl.BlockSpec((1, d), const2),
            pl.BlockSpec((None, d, f), lambda bi, i: (layer, 0, 0), **single),
            pl.BlockSpec((None, d, f), lambda bi, i: (layer, 0, 1), **single),
            pl.BlockSpec((cw.shape[0], f), const2),
            pl.BlockSpec((1, f), const2),
            pl.BlockSpec((None, f, d), lambda bi, i: (layer, 0, 0), **single),
            pl.BlockSpec((1, d), const2),
        ],
        out_specs=pl.BlockSpec((1, tm, d), lambda bi, i: (bi, i, 0)),
        out_shape=jax.ShapeDtypeStruct((b, l, d), F32),
        scratch_shapes=[pltpu.VMEM((tm, f), BF16)],
        compiler_params=_params(("parallel", "arbitrary")),
        name="ffn",
    )(x, x, x, mod4, mod4, mod4, g, w_up, w_up, cw, cb, w_down, fg)


def _trig_tables(n, row_mul, row_adds, col_mul, col_adds, period, scale=1.0):
    a_sz = 64 if n % 64 == 0 else 1
    n_stack = len(row_adds)
    q = (jnp.arange(n, dtype=jnp.int32) * col_mul)[None, :] + jnp.asarray(col_adds, jnp.int32)[:, None]
    ra = jnp.arange(n // a_sz, dtype=jnp.int32) * (a_sz * row_mul)
    rb = (jnp.arange(a_sz, dtype=jnp.int32) * row_mul)[None, :] + jnp.asarray(row_adds, jnp.int32)[:, None]
    w = 2.0 * math.pi / period
    ang_a = ((ra[None, :, None] * q[:, None, :]) % period).astype(F32) * w
    ang_b = ((rb[:, :, None] * q[:, None, :]) % period).astype(F32) * w
    ang_b = ang_b[:, None] - jnp.asarray([0.0, 0.5 * math.pi], F32)[None, :, None, None]
    ca, sa = jnp.cos(ang_a)[:, None, :, None, :], jnp.sin(ang_a)[:, None, :, None, :]
    cb, sb = jnp.cos(ang_b)[:, :, None, :, :], jnp.sin(ang_b)[:, :, None, :, :]
    both = (ca * cb - sa * sb) * scale
    return both.reshape(2 * n_stack, n // a_sz, a_sz, n)


def _pad2(a, rows, cols):
    return jnp.pad(a, ((0, rows - a.shape[0]), (0, cols - a.shape[1])))


def _hyena_features(l, pad_to):
    t = jnp.linspace(0.0, 1.0, l, dtype=F32)[:, None]
    bands = jnp.linspace(1e-4, HYENA_BANDS - 1, HYENA_BANDS, dtype=F32)[None, :]
    ang = (2.0 * math.pi / l) * jnp.arange(l, dtype=F32)[:, None] * bands
    feats = jnp.concatenate([t, jnp.cos(ang), -jnp.sin(ang)], axis=-1)
    return jnp.pad(feats, ((0, 0), (0, pad_to - feats.shape[1])))


def _bias_classes(rows, kh):
    r = np.arange(rows)
    r0 = np.clip(r - kh // 2, 0, rows - kh)
    off = r0 - r
    lo = kh // 2
    reps = list(range(lo)) + [lo] + list(range(rows - kh + lo + 1, rows))
    return [int(off[i]) for i in reps]


def _attention_bias(rpb, rows, kh):
    heads, nr, nc = rpb.shape
    w = GRID_W
    flat = rpb.astype(F32).reshape(heads * nr, nc)
    table = _bias_call(_pad2(flat, -(-heads * nr // 16) * 16, 128))
    table = table[:heads * nr].reshape(heads, nr, w, w)
    by_query = jnp.transpose(table, (0, 2, 1, 3)).reshape(heads * w, nr * w)
    firsts = [off + NA_WIN_H - 1 for off in _bias_classes(rows, kh)]
    return jnp.stack([by_query[:, first * w:(first + kh) * w] for first in firsts])


def _mixer_inputs(x, mod4, lw, tm):
    return _proj_call(x, mod4, lw['norm1_g'], lw['w_in'], lw['layer'], lw['cs'].shape[1], tm)


def _hyena(oh, lw, tabs, c, nb_fwd, nb, tm, rows):
    fwd, inv = tabs['hy_fwd'], tabs['hy_inv']
    uc = _dwconv_call(oh, lw['hyena_conv_w'], lw['hyena_conv_b'], c, rows)
    h, row0, asum = _filt_call(tabs['feats'], lw['fw1'], lw['fb1'], lw['fw2'], lw['fb2'], lw['fw3'],
                               lw['fb3'], lw['freq'], tabs['deltas'], min(rows * 2, oh.shape[1]))
    kk, km = _kspec_call(fwd, h, row0, asum, c, tm)
    spec, ym = _hfwd_call(fwd, uc, kk, km, 0, c, nb_fwd, tm)
    z2 = _hinv_call(inv, spec, ym, uc, 0, uc, 1, lw['skip'], 0, False, c, nb, tm)
    spec, ym = _hfwd_call(fwd, z2, kk, km, 1, c, nb_fwd, tm)
    return _hinv_call(inv, spec, ym, z2, 0, uc, 2, lw['skip'], 1, True, c, nb, tm)


def _stream_layer(x, mod4, lw, tabs, proj, y_att, cfg, final_norm):
    c = cfg['c']
    oa, oh, ob, of = proj
    y_pool = _pool_call(oa, tabs['pool_inv'], lw['pool_blk'], lw['pool_scale'], cfg['rows'])
    y_fnet = _fnet_call(tabs['fn'], of, lw['cs'], cfg['nb'], cfg['thy'])
    y_hyena = _hyena(oh, lw, tabs, c, cfg['nb'], cfg['nb'], cfg['thy'], cfg['rows'])
    x = _merge_call(x, mod4, lw['norm1_g'], (y_pool, y_fnet, y_hyena, y_att), lw['w_in'], lw['w_branch'],
                    lw['w_out'], lw['layer'], cfg['tm'])
    return _ffn_call(x, mod4, lw['norm2_g'], lw['w_up'], lw['ffn_conv_w'], lw['ffn_conv_b'],
                     lw['w_down'], lw['final_g'], lw['layer'], cfg['tm'], cfg['chunk'], final_norm)


def _seq_tables(l, c):
    q = l // RESIDUES
    res = list(range(RESIDUES))
    zero = [0] * RESIDUES
    hy_fwd = _trig_tables(q, 1, zero, RESIDUES, res, 2 * l).astype(BF16)
    hy_inv = _trig_tables(q, RESIDUES, res, 1, zero, 2 * l).astype(BF16)
    fn = _trig_tables(q, 1, zero, RESIDUES, res, l, 1.0 / math.sqrt(l * (c // FNET_GROUPS))).astype(BF16)
    deltas = jnp.linspace(math.log(HYENA_DECAY_TARGET) / HYENA_SLOW_DECAY,
                          math.log(HYENA_DECAY_TARGET) / HYENA_FAST_DECAY, c, dtype=F32)[None, :]
    feats = _hyena_features(l, 128)
    return {
        'hy_fwd': hy_fwd, 'hy_inv': hy_inv, 'fn': fn,
        'feats': jnp.concatenate([feats[r::RESIDUES] for r in range(RESIDUES)], axis=0), 'deltas': deltas,
        'pool_inv': _pool_inv_counts(l, c),
    }


def kernel(x, c, ctx, c_ctx, w_mod, b_mod, norm1_g, norm2_g, w_in, pool_w, pool_scale, hyena_conv_w,
           hyena_conv_b, hyena_filt_w1, hyena_filt_b1, hyena_filt_w2, hyena_filt_b2, hyena_filt_w3,
           hyena_filt_b3, hyena_freq, hyena_skip, na_rpb, w_branch, w_out, ffn_w_up, ffn_conv_w, ffn_conv_b,
           ffn_w_down, final_norm_g):
    batch, seq, d = x.shape
    lc = ctx.shape[1]
    depth = w_mod.shape[0]
    m = d // N_BRANCH
    rows = seq // GRID_W
    kh = min(NA_WIN_H, rows)

    cvec = jnp.concatenate([c, c_ctx[None], jnp.zeros((16 - batch - 1, d), F32)], axis=0)
    mod = _mod_call(cvec, w_mod, b_mod)
    mod_x = mod[:, :batch].reshape(depth, batch, 6, 1, d)
    mod_c = jnp.broadcast_to(mod[:, batch:batch + 1], (depth, batch, 6 * d)).reshape(depth, batch, 6, 1, d)

    gsz = m // FNET_GROUPS
    cc, ss = _trig_tables(gsz, 1, [0], 1, [0], gsz).reshape(2, gsz, gsz)
    eye = jnp.eye(FNET_GROUPS, dtype=F32)
    cs = jnp.concatenate([jnp.kron(eye, cc), -jnp.kron(eye, ss)], axis=0).astype(BF16)

    tabs_x = _seq_tables(seq, m)
    tabs_c = _seq_tables(lc, m)
    assert max(POOL_WINDOWS) // 2 <= PAD and batch % 2 == 0
    cfg_x = dict(c=m, tm=min(1024, seq), rows=256, nb=2, thy=min(512, seq // RESIDUES), chunk=256,
                 group=min(16, rows))
    cfg_c = dict(c=m, tm=lc, rows=lc, nb=2, thy=lc // RESIDUES, chunk=256)

    assert w_in.shape[2] == 3 * 2 * d and 8 * m == 2 * d
    w_in_b, w_branch_b, w_out_b = w_in.astype(BF16), w_branch.astype(BF16), w_out.astype(BF16)
    w_up_b, w_down_b = ffn_w_up.astype(BF16), ffn_w_down.astype(BF16)

    for l in range(depth):
        last = l == depth - 1
        lw = {
            'layer': l, 'w_in': w_in_b, 'w_branch': w_branch_b, 'w_out': w_out_b, 'w_up': w_up_b, 'w_down': w_down_b,
            'norm1_g': norm1_g[l][None], 'norm2_g': norm2_g[l][None], 'final_g': final_norm_g[None],
            'cs': cs,
            'pool_blk': jax.scipy.linalg.block_diag(*[pool_w[l, gi] for gi in range(pool_w.shape[1])]).astype(BF16),
            'pool_scale': pool_scale[l][None],
            'hyena_conv_w': hyena_conv_w[l], 'hyena_conv_b': hyena_conv_b[l][None],
            'fw1': _pad2(hyena_filt_w1[l], 128, 128), 'fb1': _pad2(hyena_filt_b1[l][None], 1, 128),
            'fw2': _pad2(hyena_filt_w2[l], 128, 128), 'fb2': _pad2(hyena_filt_b2[l][None], 1, 128),
            'fw3': _pad2(hyena_filt_w3[l], 128, hyena_filt_w3.shape[2]), 'fb3': hyena_filt_b3[l][None],
            'freq': _pad2(hyena_freq[l][None], 1, 128),
            'skip': hyena_skip[l][:, None, :],
            'ffn_conv_w': ffn_conv_w[l], 'ffn_conv_b': ffn_conv_b[l][None],
        }
        bias = _attention_bias(na_rpb[l], rows, kh)

        proj_c = _mixer_inputs(ctx, mod_c[l], lw, cfg_c['tm'])
        proj_x = _mixer_inputs(x, mod_x[l], lw, cfg_x['tm'])
        y_att = _natt_call(proj_x[2], proj_c[2], bias, m, cfg_x['group'])
        x = _stream_layer(x, mod_x[l], lw, tabs_x, proj_x, y_att, cfg_x, last)
        if not last:
            ctx = _stream_layer(ctx, mod_c[l], lw, tabs_c, proj_c, _catt_call(proj_c[2], m), cfg_c, False)
    return x
```

```python
import functools
import math

import numpy as np
import jax
import jax.numpy as jnp
from jax import lax
from jax.experimental import pallas as pl
from jax.experimental.pallas import tpu as pltpu

F32 = jnp.float32
BF16 = jnp.bfloat16

GRID_W = 64
N_BRANCH = 4
POOL_WINDOWS = (2, 4, 8, 16)
FNET_GROUPS = 4
HYENA_ORDER = 2
HYENA_BANDS = 16
HYENA_DECAY_TARGET = 1e-2
HYENA_FAST_DECAY = 0.3
HYENA_SLOW_DECAY = 1.5
HYENA_DECAY_SHIFT = 0.05
NA_HEAD_DIM = 64
NA_WIN_H = 8
NA_WIN_W = 16
EPS = 1e-6
NEG_INF = -1e30

RESIDUES = 4
SQRT_HALF = math.sqrt(0.5)
LANES = 128
HALO = 16
PAD = 8
VMEM_LIMIT = 56 * 1024 * 1024


def _params(sem):
    return pltpu.CompilerParams(dimension_semantics=sem, vmem_limit_bytes=VMEM_LIMIT)


def _dot(a, b):
    return jnp.dot(a, b, preferred_element_type=F32)


def _dot_t(a, b):
    return lax.dot_general(a, b, (((1,), (1,)), ((), ())), preferred_element_type=F32)


def _split_bf16(a):
    hi = a.astype(BF16)
    lo = (a - hi.astype(F32)).astype(BF16)
    return hi, lo


def _mod_kernel(c_ref, w_ref, b_ref, o_ref):
    a = c_ref[...]
    a = a * jax.nn.sigmoid(a)
    a_hi, a_lo = _split_bf16(a)
    w_hi, w_lo = _split_bf16(w_ref[0])
    o_ref[0] = _dot(a_hi, w_hi) + _dot(a_lo, w_hi) + _dot(a_hi, w_lo) + b_ref[0]


def _mod_call(cvec, w_mod, b_mod):
    depth, d, n = w_mod.shape
    rows = cvec.shape[0]
    tn = 1024
    return pl.pallas_call(
        _mod_kernel,
        grid=(depth, n // tn),
        in_specs=[
            pl.BlockSpec((rows, d), lambda l, j: (0, 0)),
            pl.BlockSpec((1, d, tn), lambda l, j: (l, 0, j)),
            pl.BlockSpec((1, 1, tn), lambda l, j: (l, 0, j)),
        ],
        out_specs=pl.BlockSpec((1, rows, tn), lambda l, j: (l, 0, j)),
        out_shape=jax.ShapeDtypeStruct((depth, rows, n), F32),
        compiler_params=_params(("arbitrary", "arbitrary")),
        name="mod",
    )(cvec, w_mod, b_mod.reshape(depth, 1, n))


def _ada_norm(x, g, shift, scale):
    y = x * lax.rsqrt(jnp.mean(x * x, axis=-1, keepdims=True) + EPS)
    return y * (g * (1.0 + scale)) + shift


def _deinterleave(y, scr_ref, ways):
    n = y.shape[0]
    parts = [[] for _ in range(ways)]
    for cb in range(y.shape[1] // LANES):
        scr_ref[cb] = y[:, cb * LANES:(cb + 1) * LANES]
        for k in range(ways):
            parts[k].append(scr_ref[cb, pl.ds(k, n // ways, stride=ways), :])
    return [jnp.concatenate(p, axis=1) for p in parts]


def _interleave(parts, scr_ref):
    ways = len(parts)
    n = parts[0].shape[0]
    cols = []
    for cb in range(parts[0].shape[1] // LANES):
        for k in range(ways):
            scr_ref[cb, pl.ds(k, n, stride=ways), :] = parts[k][:, cb * LANES:(cb + 1) * LANES]
        cols.append(scr_ref[cb])
    return jnp.concatenate(cols, axis=1)


def _proj_kernel(x_ref, sh_ref, sc_ref, g_ref, w_ref, oa_ref, oh_ref, ob_ref, of_ref, scr_ref):
    m = oa_ref.shape[2]
    h = _ada_norm(x_ref[0], g_ref[...], sh_ref[0, 0], sc_ref[0, 0]).astype(BF16)
    p = _dot(h, w_ref[...])
    oa_ref[0] = p[:, :m]
    oh_ref[0] = p[:, 2 * m:5 * m].astype(BF16)
    ob_ref[0] = p[:, 5 * m:8 * m].astype(BF16)
    for k, part in enumerate(_deinterleave(p[:, m:2 * m], scr_ref, RESIDUES)):
        of_ref[0, k] = part.astype(BF16)


def _proj_call(x, mod4, g, w_in, layer, m, tm):
    b, l, d = x.shape
    const = lambda bi, i: (0, 0)
    return pl.pallas_call(
        _proj_kernel,
        grid=(b, l // tm),
        in_specs=[
            pl.BlockSpec((1, tm, d), lambda bi, i: (bi, i, 0)),
            pl.BlockSpec((1, 1, 1, d), lambda bi, i: (bi, 0, 0, 0)),
            pl.BlockSpec((1, 1, 1, d), lambda bi, i: (bi, 1, 0, 0)),
            pl.BlockSpec((1, d), const),
            pl.BlockSpec((None, d, 8 * m), lambda bi, i: (layer, 0, 0)),
        ],
        out_specs=[
            pl.BlockSpec((1, tm, m), lambda bi, i: (bi, i, 0)),
            pl.BlockSpec((1, tm, 3 * m), lambda bi, i: (bi, i, 0)),
            pl.BlockSpec((1, tm, 3 * m), lambda bi, i: (bi, i, 0)),
            pl.BlockSpec((1, RESIDUES, tm // RESIDUES, m), lambda bi, i: (bi, 0, i, 0)),
        ],
        out_shape=[
            jax.ShapeDtypeStruct((b, l, m), F32),
            jax.ShapeDtypeStruct((b, l, 3 * m), BF16),
            jax.ShapeDtypeStruct((b, l, 3 * m), BF16),
            jax.ShapeDtypeStruct((b, RESIDUES, l // RESIDUES, m), BF16),
        ],
        scratch_shapes=[pltpu.VMEM((m // LANES, tm, LANES), F32)],
        compiler_params=_params(("parallel", "parallel")),
        name="proj",
    )(x, mod4, mod4, g, w_in)


def _fill_padded(src_ref, pad_ref, l, rows):
    c = pad_ref.shape[1]
    pad_ref[0:PAD, :] = jnp.zeros((PAD, c), F32)
    pad_ref[l + PAD:l + 2 * PAD, :] = jnp.zeros((PAD, c), F32)

    def copy(i, carry):
        r = pl.multiple_of(i * rows, rows)
        pad_ref[pl.ds(r + PAD, rows), :] = src_ref[0, pl.ds(r, rows), :].astype(F32)
        return carry

    lax.fori_loop(0, l // rows, copy, 0)


def _centred_window_sums(u, windows):
    n = u.shape[0]
    fwd = {1: u}
    w = 1
    while w < max(windows):
        fwd[2 * w] = fwd[w] + pltpu.roll(fwd[w], n - w, 0)
        w *= 2
    return {w: pltpu.roll(fwd[w], w // 2, 0) for w in windows}


def _pool_kernel(u_ref, inv_ref, w_ref, s_ref, o_ref, pad_ref, *, rows):
    l, c = u_ref.shape[1], u_ref.shape[2]
    gw = c // len(POOL_WINDOWS)
    per_block = LANES // gw
    _fill_padded(u_ref, pad_ref, l, rows)
    n = rows + 2 * PAD
    lane = lax.broadcasted_iota(jnp.int32, (n, LANES), 1)
    w = w_ref[...]
    scale = s_ref[...]

    def body(i, carry):
        r = pl.multiple_of(i * rows, rows)
        cols = []
        for cb in range(c // LANES):
            lanes = slice(cb * LANES, (cb + 1) * LANES)
            win = pad_ref[pl.ds(r, n), lanes]
            wins = POOL_WINDOWS[cb * per_block:(cb + 1) * per_block]
            sums = _centred_window_sums(win, wins)
            s = sums[wins[-1]]
            for k in range(per_block - 2, -1, -1):
                s = jnp.where(lane < (k + 1) * gw, sums[wins[k]], s)
            cols.append((s * inv_ref[pl.ds(r, n), lanes] - win)[PAD:PAD + rows])
        y = jnp.concatenate(cols, axis=1)
        o_ref[0, pl.ds(r, rows), :] = (_dot(y.astype(BF16), w) * scale).astype(BF16)
        return carry

    lax.fori_loop(0, l // rows, body, 0)


def _pool_inv_counts(l, c):
    gw = c // len(POOL_WINDOWS)
    t = jnp.arange(-PAD, l + PAD, dtype=jnp.int32)[:, None]
    half = jnp.asarray(np.repeat(np.array(POOL_WINDOWS) // 2, gw), jnp.int32)[None, :]
    cnt = jnp.minimum(t + half, l) - jnp.maximum(t - half, 0)
    return 1.0 / jnp.maximum(cnt, 1).astype(F32)


def _pool_call(oa, inv_cnt, w_blk, scale, rows):
    b, l, _ = oa.shape
    c = w_blk.shape[0]
    return pl.pallas_call(
        functools.partial(_pool_kernel, rows=rows),
        grid=(b,),
        in_specs=[
            pl.BlockSpec((1, l, c), lambda bi: (bi, 0, 0)),
            pl.BlockSpec((l + 2 * PAD, c), lambda bi: (0, 0)),
            pl.BlockSpec((c, c), lambda bi: (0, 0)),
            pl.BlockSpec((1, c), lambda bi: (0, 0)),
        ],
        out_specs=pl.BlockSpec((1, l, c), lambda bi: (bi, 0, 0)),
        out_shape=jax.ShapeDtypeStruct((b, l, c), BF16),
        scratch_shapes=[pltpu.VMEM((l + 2 * PAD, c), F32)],
        compiler_params=_params(("parallel",)),
        name="pool",
    )(oa, inv_cnt, w_blk, scale)


def _dwconv_kernel(u_ref, w_ref, b_ref, o_ref, pad_ref, *, rows):
    l, c = u_ref.shape[1], u_ref.shape[2]
    nlb = c // LANES
    sub = rows // RESIDUES
    for cb in range(nlb):
        pad_ref[cb, 0:PAD, :] = jnp.zeros((PAD, LANES), F32)
        pad_ref[cb, l + PAD:l + 2 * PAD, :] = jnp.zeros((PAD, LANES), F32)

    def copy(i, carry):
        r = pl.multiple_of(i * rows, rows)
        blk = u_ref[0, pl.ds(r, rows), :].astype(F32)
        for cb in range(nlb):
            pad_ref[cb, pl.ds(r + PAD, rows), :] = blk[:, cb * LANES:(cb + 1) * LANES]
        return carry

    lax.fori_loop(0, l // rows, copy, 0)
    w = w_ref[...]
    bias = b_ref[...]

    def body(i, carry):
        r = i * rows
        rs = pl.multiple_of(i * sub, sub)
        for cb in range(nlb):
            lanes = slice(cb * LANES, (cb + 1) * LANES)
            taps = [pad_ref[cb, pl.ds(r + PAD - 1 + k, sub, stride=RESIDUES), :] for k in range(RESIDUES + 2)]
            for k in range(RESIDUES):
                y = taps[k] * w[0:1, lanes] + taps[k + 1] * w[1:2, lanes] + taps[k + 2] * w[2:3, lanes] + bias[:, lanes]
                o_ref[0, k, pl.ds(rs, sub), lanes] = y.astype(BF16)
        return carry

    lax.fori_loop(0, l // rows, body, 0)


def _dwconv_call(u, w, bias, c, rows):
    b, l, _ = u.shape
    nblk = w.shape[1] // c
    return pl.pallas_call(
        functools.partial(_dwconv_kernel, rows=rows),
        grid=(b, nblk),
        in_specs=[
            pl.BlockSpec((1, l, c), lambda bi, j: (bi, 0, j)),
            pl.BlockSpec((w.shape[0], c), lambda bi, j: (0, j)),
            pl.BlockSpec((1, c), lambda bi, j: (0, j)),
        ],
        out_specs=pl.BlockSpec((1, RESIDUES, l // RESIDUES, c), lambda bi, j: (bi, 0, 0, j)),
        out_shape=jax.ShapeDtypeStruct((b, RESIDUES, l // RESIDUES, nblk * c), BF16),
        scratch_shapes=[pltpu.VMEM((c // LANES, l + 2 * PAD, LANES), F32)],
        compiler_params=_params(("parallel", "parallel")),
        name="dwconv",
    )(u, w, bias)


def _filt_kernel(f_ref, w1_ref, b1_ref, w2_ref, b2_ref, w3_ref, b3_ref, fr_ref, dl_ref,
                 hb_ref, row0_ref, asum_ref):
    hp = lax.Precision.HIGHEST
    feats = f_ref[...]
    freq = fr_ref[...]
    h = jnp.sin(freq * (jnp.dot(feats, w1_ref[...], precision=hp, preferred_element_type=F32) + b1_ref[...]))
    h = jnp.sin(freq * (jnp.dot(h, w2_ref[...], precision=hp, preferred_element_type=F32) + b2_ref[...]))
    h = jnp.dot(h, w3_ref[...], precision=hp, preferred_element_type=F32) + b3_ref[...]
    t = feats[:, 0:1]
    win = jnp.exp(-t * jnp.abs(dl_ref[...])) + HYENA_DECAY_SHIFT
    h = h * jnp.concatenate([win] * (h.shape[1] // win.shape[1]), axis=1)
    hb = h.astype(BF16)
    hb_ref[...] = hb
    part = jnp.sum(jnp.abs(h), axis=0, keepdims=True)

    @pl.when(pl.program_id(0) == 0)
    def _():
        asum_ref[...] = jnp.zeros_like(asum_ref)
        row0_ref[...] = hb[0:16].astype(F32)[0:8]

    asum_ref[...] += part


def _filt_call(feats, w1, b1, w2, b2, w3, b3, freq, deltas, rows):
    l, fd = feats.shape
    hd = w2.shape[0]
    n = w3.shape[1]
    c = deltas.shape[1]
    const = lambda i: (0, 0)
    return pl.pallas_call(
        _filt_kernel,
        grid=(l // rows,),
        in_specs=[
            pl.BlockSpec((rows, fd), lambda i: (i, 0)),
            pl.BlockSpec((fd, hd), const), pl.BlockSpec((1, hd), const),
            pl.BlockSpec((hd, hd), const), pl.BlockSpec((1, hd), const),
            pl.BlockSpec((hd, n), const), pl.BlockSpec((1, n), const),
            pl.BlockSpec((1, hd), const), pl.BlockSpec((1, c), const),
        ],
        out_specs=[
            pl.BlockSpec((rows, n), lambda i: (i, 0)),
            pl.BlockSpec((8, n), const),
            pl.BlockSpec((1, n), const),
        ],
        out_shape=[
            jax.ShapeDtypeStruct((l, n), BF16),
            jax.ShapeDtypeStruct((8, n), F32),
            jax.ShapeDtypeStruct((1, n), F32),
        ],
        compiler_params=_params(("arbitrary",)),
        name="hyena_filt",
    )(feats, w1, b1, w2, b2, w3, b3, freq, deltas)


def _alt_signs(n):
    t = lax.broadcasted_iota(jnp.int32, (16, n), 1)
    return jnp.where((t & 1) == 0, 1.0, -1.0).astype(BF16)


def _mat_specs(tm, tables):
    n, _, a, cols = tables.shape
    return [pl.BlockSpec((n, tm // a, a, cols), lambda g, i: (0, i, 0, 0))]


def _mat(ref, k):
    return ref[k].reshape(ref.shape[1] * ref.shape[2], ref.shape[3])


def _quad_transform(tab_ref, parts):
    a = [_dot(_mat(tab_ref, 2 * r), parts[r]) for r in range(RESIDUES)]
    b = [_dot(_mat(tab_ref, 2 * r + 1), parts[r]) for r in range(RESIDUES)]
    ea, fa, ga, ha = a[0] + a[2], a[0] - a[2], a[1] + a[3], a[1] - a[3]
    eb, fb, gb, hb = b[0] + b[2], b[0] - b[2], b[1] + b[3], b[1] - b[3]
    zr = (ea + ga, fa + hb, fa - hb, ea - ga)
    zs = (eb + gb, ha - fb, ha + fb, gb - eb)
    return zr, zs


def _quad_special(alt):
    c1, c3 = SQRT_HALF * alt[1], SQRT_HALF * alt[3]
    return (alt[0] + c1 - c3, c1 + alt[2] + c3), (alt[0] - c1 + c3, c1 - alt[2] + c3)


def _kspec_kernel(*refs):
    mat_refs, (h_ref, row0_ref, asum_ref, kk_ref, km_ref) = refs[0], refs[1:]
    i = pl.program_id(1)
    tm, q = kk_ref.shape[2], mat_refs.shape[3]
    c = kk_ref.shape[3]
    l = RESIDUES * q
    parts = [h_ref[r * q:(r + 1) * q, :] for r in range(RESIDUES)]
    zr, zs = _quad_transform(mat_refs, parts)
    asum = asum_ref[...]
    inv = 1.0 / (asum[:, :c] + asum[:, c:] + EPS)
    hb0 = row0_ref[0:1, c:]
    row = lax.broadcasted_iota(jnp.int32, (tm, c), 0) + i * tm
    wj = jnp.where(row == 0, 0.5 / l, 1.0 / l) * inv
    for f in range(4):
        kk_ref[0, 2 * f] = (zr[f][:, :c] + zr[f][:, c:] - hb0) * wj
        kk_ref[0, 2 * f + 1] = (zs[f][:, :c] - zs[f][:, c:]) * wj

    @pl.when(i == 0)
    def _():
        sg = _alt_signs(q)
        alt = [_dot(sg, p)[0:8] for p in parts]
        for f, (xr, xs) in enumerate(_quad_special(alt)):
            km_ref[0, :, 2 * f * c:(2 * f + 1) * c] = (xr[:, :c] + xr[:, c:] - hb0) * inv * (1.0 / l)
            km_ref[0, :, (2 * f + 1) * c:(2 * f + 2) * c] = (xs[:, :c] - xs[:, c:]) * inv * (1.0 / l)


def _kspec_call(mats, h, row0, asum, c, tm):
    q = mats.shape[3]
    l = RESIDUES * q
    orders = h.shape[1] // (2 * c)
    return pl.pallas_call(
        _kspec_kernel,
        grid=(orders, q // tm),
        in_specs=_mat_specs(tm, mats) + [
            pl.BlockSpec((l, 2 * c), lambda o, i: (0, o)),
            pl.BlockSpec((8, 2 * c), lambda o, i: (0, o)),
            pl.BlockSpec((1, 2 * c), lambda o, i: (0, o)),
        ],
        out_specs=[
            pl.BlockSpec((1, 8, tm, c), lambda o, i: (o, 0, i, 0)),
            pl.BlockSpec((1, 8, 4 * c), lambda o, i: (o, 0, 0)),
        ],
        out_shape=[
            jax.ShapeDtypeStruct((orders, 8, q, c), F32),
            jax.ShapeDtypeStruct((orders, 8, 4 * c), F32),
        ],
        compiler_params=_params(("arbitrary", "arbitrary")),
        name="hyena_kspec",
    )(mats, h, row0, asum)


def _cmul(zr, zs, kr, ks):
    return zr * kr - zs * ks, zr * ks + zs * kr


def _hfwd_kernel(*refs):
    mat_refs, (z_ref, kk_ref, km_ref, spec_ref, ym_ref) = refs[0], refs[1:]
    nb = z_ref.shape[0]
    c = z_ref.shape[3]
    for bb in range(nb):
        zr, zs = _quad_transform(mat_refs, [z_ref[bb, r] for r in range(RESIDUES)])
        yr, ys = zip(*[_cmul(zr[f], zs[f], kk_ref[0, 2 * f], kk_ref[0, 2 * f + 1]) for f in range(4)])
        a, b, u1, u2 = yr[0] + yr[3], yr[0] - yr[3], yr[1] + yr[2], ys[1] + ys[2]
        cc, dd, d1, d2 = ys[0] - ys[3], ys[0] + ys[3], yr[1] - yr[2], ys[2] - ys[1]
        for k, v in enumerate((a + u1, cc + d2, b + u2, dd + d1, a - u1, cc - d2, b - u2, dd - d1)):
            spec_ref[bb, k] = v.astype(BF16)

    @pl.when(pl.program_id(1) == 0)
    def _():
        sg = _alt_signs(z_ref.shape[2])
        for bb in range(nb):
            alt = [_dot(sg, z_ref[bb, r])[0:8] for r in range(RESIDUES)]
            (zq_r, zq_s), (z3_r, z3_s) = _quad_special(alt)
            yq_r, yq_s = _cmul(zq_r, zq_s, km_ref[0, :, 0:c], km_ref[0, :, c:2 * c])
            y3_r, y3_s = _cmul(z3_r, z3_s, km_ref[0, :, 2 * c:3 * c], km_ref[0, :, 3 * c:4 * c])
            ym_ref[bb, :, 0:c] = yq_r + y3_r
            ym_ref[bb, :, c:2 * c] = SQRT_HALF * (yq_r + yq_s - y3_r + y3_s)
            ym_ref[bb, :, 2 * c:3 * c] = yq_s - y3_s
            ym_ref[bb, :, 3 * c:4 * c] = SQRT_HALF * (yq_s - yq_r + y3_r + y3_s)


def _hfwd_call(mats, zb, kk, km, order, c, nb, tm):
    b, _, q, _ = zb.shape
    return pl.pallas_call(
        _hfwd_kernel,
        grid=(b // nb, q // tm),
        in_specs=_mat_specs(tm, mats) + [
            pl.BlockSpec((nb, RESIDUES, q, c), lambda g, i: (g, 0, 0, 0)),
            pl.BlockSpec((1, 8, tm, c), lambda g, i: (order, 0, i, 0)),
            pl.BlockSpec((1, 8, 4 * c), lambda g, i: (order, 0, 0)),
        ],
        out_specs=[
            pl.BlockSpec((nb, 8, tm, c), lambda g, i: (g, 0, i, 0)),
            pl.BlockSpec((nb, 8, 4 * c), lambda g, i: (g, 0, 0)),
        ],
        out_shape=[
            jax.ShapeDtypeStruct((b, 8, q, c), BF16),
            jax.ShapeDtypeStruct((b, 8, 4 * c), F32),
        ],
        compiler_params=_params(("arbitrary", "arbitrary")),
        name="hyena_fwd",
    )(mats, zb, kk, km)


def _hinv_kernel(*refs, natural_out):
    mat_refs, (spec_ref, ym_ref, z_ref, gate_ref, skip_ref), out_refs = refs[0], refs[1:6], refs[6:]
    nb, _, tm, c = z_ref.shape
    row = lax.broadcasted_iota(jnp.int32, (tm, c), 0)
    even_row = (row & 1) == 0
    skip = skip_ref[0]
    for bb in range(nb):
        outs = []
        for r in range(RESIDUES):
            m = ym_ref[bb, 0:1, r * c:(r + 1) * c]
            y = (_dot(_mat(mat_refs, 2 * r), spec_ref[bb, 2 * r]) + _dot(_mat(mat_refs, 2 * r + 1), spec_ref[bb, 2 * r + 1])
                 + jnp.where(even_row, m, -m))
            outs.append(gate_ref[bb, r].astype(F32) * (y + skip * z_ref[bb, r].astype(F32)))
        if natural_out:
            o_ref, scr_ref = out_refs
            o_ref[bb] = _interleave(outs, scr_ref).astype(BF16)
        else:
            (o_ref,) = out_refs
            for r in range(RESIDUES):
                o_ref[bb, r] = outs[r].astype(BF16)


def _hinv_call(mats, spec, ym, zf, zcol, uc, gcol, skip, order, natural_out, c, nb, tm):
    b, _, q, _ = spec.shape
    if natural_out:
        out_specs = pl.BlockSpec((nb, RESIDUES * tm, c), lambda g, i: (g, i, 0))
        out_shape = jax.ShapeDtypeStruct((b, RESIDUES * q, c), BF16)
        scratch = [pltpu.VMEM((c // LANES, RESIDUES * tm, LANES), F32)]
    else:
        out_specs = pl.BlockSpec((nb, RESIDUES, tm, c), lambda g, i: (g, 0, i, 0))
        out_shape = jax.ShapeDtypeStruct((b, RESIDUES, q, c), BF16)
        scratch = []
    return pl.pallas_call(
        functools.partial(_hinv_kernel, natural_out=natural_out),
        grid=(b // nb, q // tm),
        in_specs=_mat_specs(tm, mats) + [
            pl.BlockSpec((nb, 8, q, c), lambda g, i: (g, 0, 0, 0)),
            pl.BlockSpec((nb, 8, 4 * c), lambda g, i: (g, 0, 0)),
            pl.BlockSpec((nb, RESIDUES, tm, c), lambda g, i: (g, 0, i, zcol)),
            pl.BlockSpec((nb, RESIDUES, tm, c), lambda g, i: (g, 0, i, gcol)),
            pl.BlockSpec((1, 1, c), lambda g, i: (order, 0, 0)),
        ],
        out_specs=out_specs,
        out_shape=out_shape,
        scratch_shapes=scratch,
        compiler_params=_params(("arbitrary", "arbitrary")),
        name="hyena_inv",
    )(mats, spec, ym, zf, uc, skip)


def _fnet_kernel(*refs):
    mat_refs, (u_ref, cs_ref, o_ref) = refs[0], refs[1:]
    nb = u_ref.shape[0]
    cs = cs_ref[...]
    for bb in range(nb):
        parts = [u_ref[bb, r] for r in range(RESIDUES)]
        a = [_dot(_mat(mat_refs, 2 * r), parts[r]) for r in range(RESIDUES)]
        b = [_dot(_mat(mat_refs, 2 * r + 1), parts[r]) for r in range(RESIDUES)]
        ea, fa, ga, ha = a[0] + a[2], a[0] - a[2], a[1] + a[3], a[1] - a[3]
        eb, fb, gb, hb = b[0] + b[2], b[0] - b[2], b[1] + b[3], b[1] - b[3]
        uc = (ea + ga, fa - hb, ea - ga, fa + hb)
        us = (eb + gb, fb + ha, eb - gb, fb - ha)
        for m in range(RESIDUES):
            both = jnp.concatenate([uc[m].astype(BF16), us[m].astype(BF16)], axis=1)
            o_ref[bb, m] = _dot(both, cs).astype(BF16)


def _fnet_call(mats, u4, cs, nb, tm):
    b, _, q, c = u4.shape
    out = pl.pallas_call(
        _fnet_kernel,
        grid=(b // nb, q // tm),
        in_specs=_mat_specs(tm, mats) + [
            pl.BlockSpec((nb, RESIDUES, q, c), lambda g, i: (g, 0, 0, 0)),
            pl.BlockSpec(cs.shape, lambda g, i: (0, 0)),
        ],
        out_specs=pl.BlockSpec((nb, RESIDUES, tm, c), lambda g, i: (g, 0, i, 0)),
        out_shape=jax.ShapeDtypeStruct((b, RESIDUES, q, c), BF16),
        compiler_params=_params(("arbitrary", "arbitrary")),
        name="fnet",
    )(mats, u4, cs)
    return out.reshape(b, RESIDUES * q, c)


def _stack_heads(q, heads):
    lane = lax.broadcasted_iota(jnp.int32, q.shape, 1)
    zero = jnp.zeros_like(q)
    return jnp.concatenate([jnp.where(lax.shift_right_logical(lane, int(math.log2(NA_HEAD_DIM))) == h, q, zero) for h in range(heads)], axis=0)


def _merge_heads(o, heads, n):
    lane = lax.broadcasted_iota(jnp.int32, (n, o.shape[1]), 1)
    out = jnp.zeros((n, o.shape[1]), F32)
    for h in range(heads):
        out = out + jnp.where(lax.shift_right_logical(lane, int(math.log2(NA_HEAD_DIM))) == h, o[h * n:(h + 1) * n], 0.0)
    return out


def _natt_kernel(q_ref, k_ref, v_ref, kc_ref, vc_ref, bias_ref, o_ref, scx_ref, *, rows, kh, group):
    w = GRID_W
    heads = q_ref.shape[2] // NA_HEAD_DIM
    base = pl.program_id(1) * group
    lo = kh // 2
    kc, vc = kc_ref[0], vc_ref[0]
    scale = jnp.asarray(NA_HEAD_DIM ** -0.5, BF16)
    gw = group * w
    scx_all = _dot_t(_stack_heads(q_ref[0] * scale, heads), kc)
    for h in range(heads):
        scx_ref[h] = scx_all[h * gw:(h + 1) * gw]

    def body(j, carry):
        r = base + j
        r0 = jnp.clip(r - lo, 0, rows - kh)
        start = pl.multiple_of(r0 * w, w)
        cls = jnp.minimum(r, lo) + jnp.maximum(r - (rows - kh + lo), 0)
        q4 = _stack_heads(q_ref[0, pl.ds(pl.multiple_of(j * w, w), w), :] * scale, heads)
        ks = k_ref[0, pl.ds(start, kh * w), :]
        vs = v_ref[0, pl.ds(start, kh * w), :]
        s_nb = _dot_t(q4, ks) + bias_ref[cls]
        s_cx = jnp.concatenate([scx_ref[h, pl.ds(pl.multiple_of(j * w, w), w), :] for h in range(heads)], axis=0)
        m = jnp.maximum(jnp.max(s_nb, axis=-1, keepdims=True), jnp.max(s_cx, axis=-1, keepdims=True))
        p_nb = jnp.exp(s_nb - m)
        p_cx = jnp.exp(s_cx - m)
        den = jnp.sum(p_nb, axis=-1, keepdims=True) + jnp.sum(p_cx, axis=-1, keepdims=True)
        o4 = (_dot(p_nb.astype(BF16), vs) + _dot(p_cx.astype(BF16), vc)) / den
        o_ref[0, pl.ds(pl.multiple_of(j * w, w), w), :] = _merge_heads(o4, heads, w).astype(BF16)
        return carry

    lax.fori_loop(0, group, body, 0, unroll=4)


def _natt_call(ob, obc, bias, c, group):
    b, l, _ = ob.shape
    lc = obc.shape[1]
    w = GRID_W
    rows = l // w
    kh = min(NA_WIN_H, rows)
    return pl.pallas_call(
        functools.partial(_natt_kernel, rows=rows, kh=kh, group=group),
        grid=(b, rows // group),
        in_specs=[
            pl.BlockSpec((1, group * w, c), lambda bi, r: (bi, r, 0)),
            pl.BlockSpec((1, l, c), lambda bi, r: (bi, 0, 1)),
            pl.BlockSpec((1, l, c), lambda bi, r: (bi, 0, 2)),
            pl.BlockSpec((1, lc, c), lambda bi, r: (bi, 0, 1)),
            pl.BlockSpec((1, lc, c), lambda bi, r: (bi, 0, 2)),
            pl.BlockSpec(bias.shape, lambda bi, r: (0, 0, 0)),
        ],
        out_specs=pl.BlockSpec((1, group * w, c), lambda bi, r: (bi, r, 0)),
        out_shape=jax.ShapeDtypeStruct((b, l, c), BF16),
        scratch_shapes=[pltpu.VMEM((c // NA_HEAD_DIM, group * w, lc), F32)],
        compiler_params=_params(("parallel", "arbitrary")),
        name="nb_attn",
    )(ob, ob, ob, obc, obc, bias)


def _bias_kernel(r_ref, o_ref):
    w = GRID_W
    shift = int(math.log2(w))
    n = o_ref.shape[1]
    r = r_ref[...]
    hi = r.astype(BF16)
    mid = (r - hi.astype(F32)).astype(BF16)
    lo = (r - hi.astype(F32) - mid.astype(F32)).astype(BF16)
    dc = lax.broadcasted_iota(jnp.int32, (r.shape[1], n), 0)
    p = lax.broadcasted_iota(jnp.int32, (r.shape[1], n), 1)
    idx = jnp.clip((p & (w - 1)) - lax.shift_right_logical(p, shift), 1 - NA_WIN_W, NA_WIN_W - 1) + (NA_WIN_W - 1)
    onehot = jnp.where(dc == idx, 1.0, 0.0).astype(BF16)
    val = _dot(hi, onehot) + _dot(mid, onehot) + _dot(lo, onehot)
    po = lax.broadcasted_iota(jnp.int32, val.shape, 1)
    cq = lax.shift_right_logical(po, shift)
    ck = po & (w - 1)
    c0 = jnp.clip(cq - NA_WIN_W // 2, 0, w - NA_WIN_W)
    inside = jnp.logical_and(ck >= c0, ck < c0 + NA_WIN_W)
    o_ref[...] = jnp.where(inside, val, NEG_INF)


def _bias_call(rpb2d):
    rows, cols = rpb2d.shape
    n = GRID_W * GRID_W
    return pl.pallas_call(
        _bias_kernel,
        grid=(1,),
        in_specs=[pl.BlockSpec((rows, cols), lambda i: (0, 0))],
        out_specs=pl.BlockSpec((rows, n), lambda i: (0, 0)),
        out_shape=jax.ShapeDtypeStruct((rows, n), F32),
        compiler_params=_params(("arbitrary",)),
        name="rpb_table",
    )(rpb2d)


def _catt_kernel(q_ref, k_ref, v_ref, o_ref):
    n = q_ref.shape[1]
    heads = q_ref.shape[2] // NA_HEAD_DIM
    q = q_ref[0] * jnp.asarray(NA_HEAD_DIM ** -0.5, BF16)
    q4 = _stack_heads(q, heads)
    s = _dot_t(q4, k_ref[0])
    p = jnp.exp(s - jnp.max(s, axis=-1, keepdims=True))
    den = jnp.sum(p, axis=-1, keepdims=True)
    o4 = _dot(p.astype(BF16), v_ref[0]) / den
    o_ref[0] = _merge_heads(o4, heads, n).astype(BF16)


def _catt_call(obc, c):
    b, lc, _ = obc.shape
    return pl.pallas_call(
        _catt_kernel,
        grid=(b,),
        in_specs=[pl.BlockSpec((1, lc, c), lambda bi, j=j: (bi, 0, j)) for j in range(3)],
        out_specs=pl.BlockSpec((1, lc, c), lambda bi: (bi, 0, 0)),
        out_shape=jax.ShapeDtypeStruct((b, lc, c), BF16),
        compiler_params=_params(("parallel",)),
        name="ctx_attn",
    )(obc, obc, obc)


def _merge_kernel(x_ref, sh_ref, sc_ref, gt_ref, g_ref, y0_ref, y1_ref, y2_ref, y3_ref,
                  wg0_ref, wg1_ref, wb_ref, wo_ref, o_ref):
    d = x_ref.shape[2]
    x = x_ref[0]
    h = _ada_norm(x, g_ref[...], sh_ref[0, 0], sc_ref[0, 0]).astype(BF16)
    merged = None
    for bi, y_ref in enumerate((y0_ref, y1_ref, y2_ref, y3_ref)):
        wg_ref = (wg0_ref, wg1_ref)[bi // 2]
        gate = jax.nn.sigmoid(_dot(h, wg_ref[:, (bi % 2) * d:(bi % 2 + 1) * d]))
        term = gate * _dot(y_ref[0].astype(BF16), wb_ref[bi])
        merged = term if merged is None else merged + term
    o_ref[0] = x + gt_ref[0, 0] * _dot(merged.astype(BF16), wo_ref[...])


def _merge_call(x, mod4, g, ys, w_in, wb, wo, layer, tm):
    b, l, d = x.shape
    c = ys[0].shape[2]
    gate_blk = w_in.shape[2] // (2 * d) - 2
    assert (gate_blk + 2) * 2 * d == w_in.shape[2]
    single = dict(pipeline_mode=pl.Buffered(1))
    mod_spec = lambda k: pl.BlockSpec((1, 1, 1, d), lambda bi, i: (bi, k, 0, 0))
    return pl.pallas_call(
        _merge_kernel,
        grid=(b, l // tm),
        in_specs=[
            pl.BlockSpec((1, tm, d), lambda bi, i: (bi, i, 0)),
            mod_spec(0), mod_spec(1), mod_spec(2),
            pl.BlockSpec((1, d), lambda bi, i: (0, 0)),
        ] + [pl.BlockSpec((1, tm, c), lambda bi, i: (bi, i, 0)) for _ in ys] + [
            pl.BlockSpec((None, d, 2 * d), lambda bi, i: (layer, 0, gate_blk), **single),
            pl.BlockSpec((None, d, 2 * d), lambda bi, i: (layer, 0, gate_blk + 1), **single),
            pl.BlockSpec((None,) + wb.shape[1:], lambda bi, i: (layer, 0, 0, 0), **single),
            pl.BlockSpec((None, d, d), lambda bi, i: (layer, 0, 0), **single),
        ],
        out_specs=pl.BlockSpec((1, tm, d), lambda bi, i: (bi, i, 0)),
        out_shape=jax.ShapeDtypeStruct((b, l, d), F32),
        compiler_params=_params(("parallel", "parallel")),
        name="merge",
    )(x, mod4, mod4, mod4, g, *ys, w_in, w_in, wb, wo)


def _ffn_kernel(x_ref, xp_ref, xn_ref, sh_ref, sc_ref, gt_ref, g_ref, wu_ref, wgt_ref, cw_ref, cb_ref,
                wd_ref, fg_ref, o_ref, a_ref, *, chunk, final_norm):
    i = pl.program_id(1)
    last = pl.num_programs(1) - 1
    tm = x_ref.shape[1]
    f = wu_ref.shape[1]
    n = tm + 2 * HALO
    g, sh, sc = g_ref[...], sh_ref[0, 0], sc_ref[0, 0]
    x = x_ref[0]
    h = _ada_norm(x, g, sh, sc).astype(BF16)
    hp = _ada_norm(xp_ref[0], g, sh, sc).astype(BF16)
    hn = _ada_norm(xn_ref[0], g, sh, sc).astype(BF16)
    hext = jnp.concatenate([hp, h, hn], axis=0)
    row = lax.broadcasted_iota(jnp.int32, (n, chunk), 0)
    valid = jnp.logical_and(jnp.logical_or(row >= HALO, i > 0), jnp.logical_or(row < HALO + tm, i < last))
    for j in range(f // chunk):
        cols = slice(j * chunk, (j + 1) * chunk)
        gp = jnp.where(valid, _dot(hext, wgt_ref[:, cols]), 0.0)
        cw = cw_ref[:, cols]
        gc = (pltpu.roll(gp, 1, 0) * cw[0:1] + gp * cw[1:2] + pltpu.roll(gp, n - 1, 0) * cw[2:3]
              + cb_ref[:, cols])[HALO:HALO + tm]
        u = _dot(h, wu_ref[:, cols])
        a_ref[:, cols] = (gc * jax.nn.sigmoid(gc) * u).astype(BF16)
    y = x + gt_ref[0, 0] * _dot(a_ref[...], wd_ref[...])
    if final_norm:
        y = y * lax.rsqrt(jnp.mean(y * y, axis=-1, keepdims=True) + EPS) * fg_ref[...]
    o_ref[0] = y


def _ffn_call(x, mod4, g, w_up, cw, cb, w_down, fg, layer, tm, chunk, final_norm):
    b, l, d = x.shape
    f = w_down.shape[1]
    hb = tm // HALO
    nblk = l // HALO
    const2 = lambda bi, i: (0, 0)
    mod_spec = lambda k: pl.BlockSpec((1, 1, 1, d), lambda bi, i: (bi, k, 0, 0))
    single = dict(pipeline_mode=pl.Buffered(1))
    return pl.pallas_call(
        functools.partial(_ffn_kernel, chunk=chunk, final_norm=final_norm),
        grid=(b, l // tm),
        in_specs=[
            pl.BlockSpec((1, tm, d), lambda bi, i: (bi, i, 0)),
            pl.BlockSpec((1, HALO, d), lambda bi, i: (bi, jnp.maximum(i * hb - 1, 0), 0)),
            pl.BlockSpec((1, HALO, d), lambda bi, i: (bi, jnp.minimum((i + 1) * hb, nblk - 1), 0)),
            mod_spec(3), mod_spec(4), mod_spec(5),
            pl.BlockSpec((1, d), const2),
            pl.BlockSpec((None, d, f), lambda bi, i: (layer, 0, 0), **single),
            pl.BlockSpec((None, d, f), lambda bi, i: (layer, 0, 1), **single),
            pl.BlockSpec((cw.shape[0], f), const2),
            pl.BlockSpec((1, f), const2),
            pl.BlockSpec((None, f, d), lambda bi, i: (layer, 0, 0), **single),
            pl.BlockSpec((1, d), const2),
        ],
        out_specs=pl.BlockSpec((1, tm, d), lambda bi, i: (bi, i, 0)),
        out_shape=jax.ShapeDtypeStruct((b, l, d), F32),
        scratch_shapes=[pltpu.VMEM((tm, f), BF16)],
        compiler_params=_params(("parallel", "arbitrary")),
        name="ffn",
    )(x, x, x, mod4, mod4, mod4, g, w_up, w_up, cw, cb, w_down, fg)


def _trig_tables(n, row_mul, row_adds, col_mul, col_adds, period, scale=1.0):
    a_sz = 64 if n % 64 == 0 else 1
    n_stack = len(row_adds)
    q = (jnp.arange(n, dtype=jnp.int32) * col_mul)[None, :] + jnp.asarray(col_adds, jnp.int32)[:, None]
    ra = jnp.arange(n // a_sz, dtype=jnp.int32) * (a_sz * row_mul)
    rb = (jnp.arange(a_sz, dtype=jnp.int32) * row_mul)[None, :] + jnp.asarray(row_adds, jnp.int32)[:, None]
    w = 2.0 * math.pi / period
    ang_a = ((ra[None, :, None] * q[:, None, :]) % period).astype(F32) * w
    ang_b = ((rb[:, :, None] * q[:, None, :]) % period).astype(F32) * w
    ang_b = ang_b[:, None] - jnp.asarray([0.0, 0.5 * math.pi], F32)[None, :, None, None]
    ca, sa = jnp.cos(ang_a)[:, None, :, None, :], jnp.sin(ang_a)[:, None, :, None, :]
    cb, sb = jnp.cos(ang_b)[:, :, None, :, :], jnp.sin(ang_b)[:, :, None, :, :]
    both = (ca * cb - sa * sb) * scale
    return both.reshape(2 * n_stack, n // a_sz, a_sz, n)


def _pad2(a, rows, cols):
    return jnp.pad(a, ((0, rows - a.shape[0]), (0, cols - a.shape[1])))


def _hyena_features(l, pad_to):
    t = jnp.linspace(0.0, 1.0, l, dtype=F32)[:, None]
    bands = jnp.linspace(1e-4, HYENA_BANDS - 1, HYENA_BANDS, dtype=F32)[None, :]
    ang = (2.0 * math.pi / l) * jnp.arange(l, dtype=F32)[:, None] * bands
    feats = jnp.concatenate([t, jnp.cos(ang), -jnp.sin(ang)], axis=-1)
    return jnp.pad(feats, ((0, 0), (0, pad_to - feats.shape[1])))


def _bias_classes(rows, kh):
    r = np.arange(rows)
    r0 = np.clip(r - kh // 2, 0, rows - kh)
    off = r0 - r
    lo = kh // 2
    reps = list(range(lo)) + [lo] + list(range(rows - kh + lo + 1, rows))
    return [int(off[i]) for i in reps]


def _attention_bias(rpb, rows, kh):
    heads, nr, nc = rpb.shape
    w = GRID_W
    flat = rpb.astype(F32).reshape(heads * nr, nc)
    table = _bias_call(_pad2(flat, -(-heads * nr // 16) * 16, 128))
    table = table[:heads * nr].reshape(heads, nr, w, w)
    by_query = jnp.transpose(table, (0, 2, 1, 3)).reshape(heads * w, nr * w)
    firsts = [off + NA_WIN_H - 1 for off in _bias_classes(rows, kh)]
    return jnp.stack([by_query[:, first * w:(first + kh) * w] for first in firsts])


def _mixer_inputs(x, mod4, lw, tm):
    return _proj_call(x, mod4, lw['norm1_g'], lw['w_in'], lw['layer'], lw['cs'].shape[1], tm)


def _hyena(oh, lw, tabs, c, nb_fwd, nb, tm, rows):
    fwd, inv = tabs['hy_fwd'], tabs['hy_inv']
    uc = _dwconv_call(oh, lw['hyena_conv_w'], lw['hyena_conv_b'], c, rows)
    h, row0, asum = _filt_call(tabs['feats'], lw['fw1'], lw['fb1'], lw['fw2'], lw['fb2'], lw['fw3'],
                               lw['fb3'], lw['freq'], tabs['deltas'], min(rows * 2, oh.shape[1]))
    kk, km = _kspec_call(fwd, h, row0, asum, c, tm)
    spec, ym = _hfwd_call(fwd, uc, kk, km, 0, c, nb_fwd, tm)
    z2 = _hinv_call(inv, spec, ym, uc, 0, uc, 1, lw['skip'], 0, False, c, nb, tm)
    spec, ym = _hfwd_call(fwd, z2, kk, km, 1, c, nb_fwd, tm)
    return _hinv_call(inv, spec, ym, z2, 0, uc, 2, lw['skip'], 1, True, c, nb, tm)


def _stream_layer(x, mod4, lw, tabs, proj, y_att, cfg, final_norm):
    c = cfg['c']
    oa, oh, ob, of = proj
    y_pool = _pool_call(oa, tabs['pool_inv'], lw['pool_blk'], lw['pool_scale'], cfg['rows'])
    y_fnet = _fnet_call(tabs['fn'], of, lw['cs'], cfg['nb'], cfg['thy'])
    y_hyena = _hyena(oh, lw, tabs, c, cfg['nb'], cfg['nb'], cfg['thy'], cfg['rows'])
    x = _merge_call(x, mod4, lw['norm1_g'], (y_pool, y_fnet, y_hyena, y_att), lw['w_in'], lw['w_branch'],
                    lw['w_out'], lw['layer'], cfg['tm'])
    return _ffn_call(x, mod4, lw['norm2_g'], lw['w_up'], lw['ffn_conv_w'], lw['ffn_conv_b'],
                     lw['w_down'], lw['final_g'], lw['layer'], cfg['tm'], cfg['chunk'], final_norm)


def _seq_tables(l, c):
    q = l // RESIDUES
    res = list(range(RESIDUES))
    zero = [0] * RESIDUES
    hy_fwd = _trig_tables(q, 1, zero, RESIDUES, res, 2 * l).astype(BF16)
    hy_inv = _trig_tables(q, RESIDUES, res, 1, zero, 2 * l).astype(BF16)
    fn = _trig_tables(q, 1, zero, RESIDUES, res, l, 1.0 / math.sqrt(l * (c // FNET_GROUPS))).astype(BF16)
    deltas = jnp.linspace(math.log(HYENA_DECAY_TARGET) / HYENA_SLOW_DECAY,
                          math.log(HYENA_DECAY_TARGET) / HYENA_FAST_DECAY, c, dtype=F32)[None, :]
    feats = _hyena_features(l, 128)
    return {
        'hy_fwd': hy_fwd, 'hy_inv': hy_inv, 'fn': fn,
        'feats': jnp.concatenate([feats[r::RESIDUES] for r in range(RESIDUES)], axis=0), 'deltas': deltas,
        'pool_inv': _pool_inv_counts(l, c),
    }


def kernel(x, c, ctx, c_ctx, w_mod, b_mod, norm1_g, norm2_g, w_in, pool_w, pool_scale, hyena_conv_w,
           hyena_conv_b, hyena_filt_w1, hyena_filt_b1, hyena_filt_w2, hyena_filt_b2, hyena_filt_w3,
           hyena_filt_b3, hyena_freq, hyena_skip, na_rpb, w_branch, w_out, ffn_w_up, ffn_conv_w, ffn_conv_b,
           ffn_w_down, final_norm_g):
    batch, seq, d = x.shape
    lc = ctx.shape[1]
    depth = w_mod.shape[0]
    m = d // N_BRANCH
    rows = seq // GRID_W
    kh = min(NA_WIN_H, rows)

    cvec = jnp.concatenate([c, c_ctx[None], jnp.zeros((16 - batch - 1, d), F32)], axis=0)
    mod = _mod_call(cvec, w_mod, b_mod)
    mod_x = mod[:, :batch].reshape(depth, batch, 6, 1, d)
    mod_c = jnp.broadcast_to(mod[:, batch:batch + 1], (depth, batch, 6 * d)).reshape(depth, batch, 6, 1, d)

    gsz = m // FNET_GROUPS
    cc, ss = _trig_tables(gsz, 1, [0], 1, [0], gsz).reshape(2, gsz, gsz)
    eye = jnp.eye(FNET_GROUPS, dtype=F32)
    cs = jnp.concatenate([jnp.kron(eye, cc), -jnp.kron(eye, ss)], axis=0).astype(BF16)

    tabs_x = _seq_tables(seq, m)
    tabs_c = _seq_tables(lc, m)
    assert max(POOL_WINDOWS) // 2 <= PAD and batch % 2 == 0
    cfg_x = dict(c=m, tm=min(1024, seq), rows=256, nb=2, thy=min(512, seq // RESIDUES), chunk=256,
                 group=min(16, rows))
    cfg_c = dict(c=m, tm=lc, rows=lc, nb=2, thy=lc // RESIDUES, chunk=256)

    assert w_in.shape[2] == 3 * 2 * d and 8 * m == 2 * d
    w_in_b, w_branch_b, w_out_b = w_in.astype(BF16), w_branch.astype(BF16), w_out.astype(BF16)
    w_up_b, w_down_b = ffn_w_up.astype(BF16), ffn_w_down.astype(BF16)

    for l in range(depth):
        last = l == depth - 1
        lw = {
            'layer': l, 'w_in': w_in_b, 'w_branch': w_branch_b, 'w_out': w_out_b, 'w_up': w_up_b, 'w_down': w_down_b,
            'norm1_g': norm1_g[l][None], 'norm2_g': norm2_g[l][None], 'final_g': final_norm_g[None],
            'cs': cs,
            'pool_blk': jax.scipy.linalg.block_diag(*[pool_w[l, gi] for gi in range(pool_w.shape[1])]).astype(BF16),
            'pool_scale': pool_scale[l][None],
            'hyena_conv_w': hyena_conv_w[l], 'hyena_conv_b': hyena_conv_b[l][None],
            'fw1': _pad2(hyena_filt_w1[l], 128, 128), 'fb1': _pad2(hyena_filt_b1[l][None], 1, 128),
            'fw2': _pad2(hyena_filt_w2[l], 128, 128), 'fb2': _pad2(hyena_filt_b2[l][None], 1, 128),
            'fw3': _pad2(hyena_filt_w3[l], 128, hyena_filt_w3.shape[2]), 'fb3': hyena_filt_b3[l][None],
            'freq': _pad2(hyena_freq[l][None], 1, 128),
            'skip': hyena_skip[l][:, None, :],
            'ffn_conv_w': ffn_conv_w[l], 'ffn_conv_b': ffn_conv_b[l][None],
        }
        bias = _attention_bias(na_rpb[l], rows, kh)

        proj_c = _mixer_inputs(ctx, mod_c[l], lw, cfg_c['tm'])
        proj_x = _mixer_inputs(x, mod_x[l], lw, cfg_x['tm'])
        y_att = _natt_call(proj_x[2], proj_c[2], bias, m, cfg_x['group'])
        x = _stream_layer(x, mod_x[l], lw, tabs_x, proj_x, y_att, cfg_x, last)
        if not last:
            ctx = _stream_layer(ctx, mod_c[l], lw, tabs_c, proj_c, _catt_call(proj_c[2], m), cfg_c, False)
    return x
```
